```python
import math
import jax, jax.numpy as jnp
from jax import lax
import numpy as np

D_MODEL = 2048
BATCH = 1
SEQ = 8192
DEPTH = 4

HEAD_DIM = 128
N_HEADS_A = 8
N_HEADS_B = 8
Q_BLOCK = 128
MOBA_BLOCK = 256
MOBA_TOPK = 3
MOBA_Q_CHUNK = 64
N_BUCKETS = 32
MAX_DISTANCE = 128
MLA_HEADS = 16
MLA_Q_RANK = 512
MLA_KV_RANK = 512
MLA_NOPE = 128
MLA_ROPE = 64
MLA_V = 128
ROPE_THETA = 10000.0
D_FF = -(-8 * D_MODEL // (3 * 256)) * 256
DEEPNORM_ALPHA = (2 * DEPTH) ** 0.25
DEEPNORM_BETA = (8 * DEPTH) ** -0.25
N_EVEN = (DEPTH + 1) // 2
N_ODD = DEPTH // 2

DA = N_HEADS_A * HEAD_DIM
DB = N_HEADS_B * HEAD_DIM
AB_SPLITS = (DA, 2 * DA, 3 * DA, 3 * DA + N_HEADS_A,
             3 * DA + N_HEADS_A + DB, 3 * DA + N_HEADS_A + 2 * DB)
AB_IN = 3 * DA + N_HEADS_A + 3 * DB
AB_OUT = DA + DB
MLA_IN = MLA_Q_RANK + MLA_KV_RANK + MLA_ROPE

kernel_name = "fox_moba_mla_deepnorm_hybrid"


def layer_norm(x, g, b, eps=1e-5):
    xf = x.astype(jnp.float32)
    mu = jnp.mean(xf, axis=-1, keepdims=True)
    var = jnp.mean(jnp.square(xf - mu), axis=-1, keepdims=True)
    return ((xf - mu) * lax.rsqrt(var + eps)).astype(x.dtype) * g + b


def rms_norm(x, g, eps=1e-6):
    xf = x.astype(jnp.float32)
    return (xf * lax.rsqrt(jnp.mean(jnp.square(xf), axis=-1, keepdims=True) + eps)).astype(x.dtype) * g


def t5_bucket(rel):
    n = jnp.maximum(rel, 0)
    max_exact = N_BUCKETS // 2
    nf = jnp.maximum(n, 1).astype(jnp.float32)
    large = max_exact + (jnp.log(nf / max_exact) / math.log(MAX_DISTANCE / max_exact)
                         * (N_BUCKETS - max_exact)).astype(jnp.int32)
    large = jnp.minimum(large, N_BUCKETS - 1)
    return jnp.where(n < max_exact, n, large)


def rope_tables(S, dtype):
    inv = ROPE_THETA ** (-jnp.arange(0, MLA_ROPE, 2, dtype=jnp.float32) / MLA_ROPE)
    ang = jnp.arange(S, dtype=jnp.float32)[:, None] * inv[None, :]
    return jnp.cos(ang).astype(dtype), jnp.sin(ang).astype(dtype)


def apply_rope(x, cos, sin):
    x1, x2 = jnp.split(x, 2, axis=-1)
    c = cos[None, :, None, :]
    s = sin[None, :, None, :]
    return jnp.concatenate([x1 * c - x2 * s, x1 * s + x2 * c], axis=-1)


def causal_attention(q, k, v, log_decay=None):
    B, S, H, Dq = q.shape
    Dv = v.shape[-1]
    scale = Dq ** -0.5
    kpos = jnp.arange(S)
    cT = None if log_decay is None else jnp.cumsum(log_decay, axis=1).transpose(0, 2, 1)

    def block(i):
        start = i * Q_BLOCK
        qb = lax.dynamic_slice_in_dim(q, start, Q_BLOCK, axis=1)
        qpos = start + jnp.arange(Q_BLOCK)
        s = jnp.einsum('bqhd,bkhd->bhqk', qb, k, preferred_element_type=jnp.float32) * scale
        if cT is not None:
            cq = lax.dynamic_slice_in_dim(cT, start, Q_BLOCK, axis=2)
            s = s + (cq[..., :, None] - cT[..., None, :])
        s = jnp.where(qpos[:, None] >= kpos[None, :], s, -jnp.inf)
        p = jax.nn.softmax(s, axis=-1).astype(v.dtype)
        return jnp.einsum('bhqk,bkhd->bqhd', p, v)

    out = lax.map(block, jnp.arange(S // Q_BLOCK))
    return out.transpose(1, 0, 2, 3, 4).reshape(B, S, H, Dv)


def moba_attention(q, k, v, rel_bias):
    B, S, H, D = q.shape
    nb = -(-S // MOBA_BLOCK)
    pad = nb * MOBA_BLOCK - S
    kp = jnp.pad(k, ((0, 0), (0, pad), (0, 0), (0, 0)))
    vp = jnp.pad(v, ((0, 0), (0, pad), (0, 0), (0, 0)))
    kbt = kp.reshape(B, nb, MOBA_BLOCK, H, D).transpose(0, 3, 1, 2, 4)
    vbt = vp.reshape(B, nb, MOBA_BLOCK, H, D).transpose(0, 3, 1, 2, 4)
    k_mean = jnp.mean(kbt.astype(jnp.float32), axis=3)
    topk = min(MOBA_TOPK, nb)
    scale = D ** -0.5
    offs = jnp.arange(MOBA_BLOCK)
    bias_t = rel_bias.T.astype(jnp.float32)
    bi = jnp.arange(B)[:, None, None, None]
    hi = jnp.arange(H)[None, :, None, None]
    hi5 = jnp.arange(H)[None, :, None, None, None]

    def chunk(i):
        start = i * MOBA_Q_CHUNK
        qc = lax.dynamic_slice_in_dim(q, start, MOBA_Q_CHUNK, axis=1)
        qpos = start + jnp.arange(MOBA_Q_CHUNK)
        own = start // MOBA_BLOCK
        g = jnp.einsum('bqhd,bhnd->bhqn', qc.astype(jnp.float32), k_mean)
        g = jnp.where(jnp.arange(nb) < own, g, -jnp.inf)
        _, sel = lax.top_k(g, topk)
        valid = sel < own
        k_sel = kbt[bi, hi, sel]
        v_sel = vbt[bi, hi, sel]
        s_sel = jnp.einsum('bqhd,bhqnkd->bhqnk', qc, k_sel,
                           preferred_element_type=jnp.float32) * scale
        kpos_sel = sel[..., None] * MOBA_BLOCK + offs
        rel_sel = qpos[None, None, :, None, None] - kpos_sel
        s_sel = s_sel + bias_t[hi5, t5_bucket(rel_sel)]
        s_sel = jnp.where(valid[..., None], s_sel, -jnp.inf)
        k_own = lax.dynamic_slice_in_dim(kp, own * MOBA_BLOCK, MOBA_BLOCK, axis=1)
        v_own = lax.dynamic_slice_in_dim(vp, own * MOBA_BLOCK, MOBA_BLOCK, axis=1)
        rel_own = qpos[:, None] - (own * MOBA_BLOCK + offs)[None, :]
        s_own = jnp.einsum('bqhd,bkhd->bhqk', qc, k_own,
                           preferred_element_type=jnp.float32) * scale
        s_own = s_own + bias_t[:, t5_bucket(rel_own)][None]
        s_own = jnp.where(rel_own >= 0, s_own, -jnp.inf)
        s_all = jnp.concatenate(
            [s_sel.reshape(B, H, MOBA_Q_CHUNK, topk * MOBA_BLOCK), s_own], axis=-1)
        p = jax.nn.softmax(s_all, axis=-1).astype(v.dtype)
        p_sel = p[..., :topk * MOBA_BLOCK].reshape(B, H, MOBA_Q_CHUNK, topk, MOBA_BLOCK)
        p_own = p[..., topk * MOBA_BLOCK:]
        return (jnp.einsum('bhqnk,bhqnkd->bqhd', p_sel, v_sel)
                + jnp.einsum('bhqk,bkhd->bqhd', p_own, v_own))

    out = lax.map(chunk, jnp.arange(S // MOBA_Q_CHUNK))
    return out.transpose(1, 0, 2, 3, 4).reshape(B, S, H, D)


def fox_moba_mixer(x, w_in, b_f, w_out, rel_bias):
    B, S, _ = x.shape
    h = x @ w_in
    qa, ka, va, fa, qb, kb, vb = jnp.split(h, AB_SPLITS, axis=-1)
    shp_a = (B, S, N_HEADS_A, HEAD_DIM)
    shp_b = (B, S, N_HEADS_B, HEAD_DIM)
    log_f = jax.nn.log_sigmoid((fa + b_f).astype(jnp.float32))
    ya = causal_attention(qa.reshape(shp_a), ka.reshape(shp_a), va.reshape(shp_a), log_f)
    yb = moba_attention(qb.reshape(shp_b), kb.reshape(shp_b), vb.reshape(shp_b), rel_bias)
    y = jnp.concatenate([ya.reshape(B, S, DA), yb.reshape(B, S, DB)], axis=-1)
    return y @ w_out


def mla_mixer(x, w_in, q_norm_g, kv_norm_g, w_uq, w_ukv, w_out, cos, sin):
    B, S, _ = x.shape
    h = x @ w_in
    cq, ckv, k_rope = jnp.split(h, (MLA_Q_RANK, MLA_Q_RANK + MLA_KV_RANK), axis=-1)
    cq = rms_norm(cq, q_norm_g)
    ckv = rms_norm(ckv, kv_norm_g)
    q = (cq @ w_uq).reshape(B, S, MLA_HEADS, MLA_NOPE + MLA_ROPE)
    kv = (ckv @ w_ukv).reshape(B, S, MLA_HEADS, MLA_NOPE + MLA_V)
    q_nope, q_rope = jnp.split(q, (MLA_NOPE,), axis=-1)
    k_nope, v = jnp.split(kv, (MLA_NOPE,), axis=-1)
    q_rope = apply_rope(q_rope, cos, sin)
    k_rope = apply_rope(k_rope[:, :, None, :], cos, sin)
    q = jnp.concatenate([q_nope, q_rope], axis=-1)
    k = jnp.concatenate(
        [k_nope, jnp.broadcast_to(k_rope, (B, S, MLA_HEADS, MLA_ROPE))], axis=-1)
    y = causal_attention(q, k, v)
    return y.reshape(B, S, MLA_HEADS * MLA_V) @ w_out


def swiglu(x, w_gate, w_up, w_down):
    return (jax.nn.silu(x @ w_gate) * (x @ w_up)) @ w_down


def setup_inputs(seed: int = 0) -> dict:
    key = jax.random.key(seed)
    ks = jax.random.split(key, 16)
    f32 = jnp.float32

    def w(k, shape, fan_in, gain=1.0):
        return jax.random.normal(k, shape, f32) * (gain * fan_in ** -0.5)

    x = jax.random.normal(ks[0], (BATCH, SEQ, D_MODEL), f32)
    ab_w_in = w(ks[1], (N_EVEN, D_MODEL, AB_IN), D_MODEL)
    ab_forget_bias = 3.0 + 0.1 * jax.random.normal(ks[2], (N_EVEN, N_HEADS_A), f32)
    ab_w_out = w(ks[3], (N_EVEN, AB_OUT, D_MODEL), AB_OUT, DEEPNORM_BETA)
    rel_bias = 0.2 * jax.random.normal(ks[4], (N_BUCKETS, N_HEADS_B), f32)
    mla_w_in = w(ks[5], (N_ODD, D_MODEL, MLA_IN), D_MODEL)
    mla_q_norm = 1.0 + 0.02 * jax.random.normal(ks[6], (N_ODD, MLA_Q_RANK), f32)
    mla_kv_norm = 1.0 + 0.02 * jax.random.normal(ks[7], (N_ODD, MLA_KV_RANK), f32)
    mla_w_uq = w(ks[8], (N_ODD, MLA_Q_RANK, MLA_HEADS * (MLA_NOPE + MLA_ROPE)), MLA_Q_RANK)
    mla_w_ukv = w(ks[9], (N_ODD, MLA_KV_RANK, MLA_HEADS * (MLA_NOPE + MLA_V)), MLA_KV_RANK)
    mla_w_out = w(ks[10], (N_ODD, MLA_HEADS * MLA_V, D_MODEL), MLA_HEADS * MLA_V, DEEPNORM_BETA)
    ffn_w_gate = w(ks[11], (DEPTH, D_MODEL, D_FF), D_MODEL)
    ffn_w_up = w(ks[12], (DEPTH, D_MODEL, D_FF), D_MODEL)
    ffn_w_down = w(ks[13], (DEPTH, D_FF, D_MODEL), D_FF, DEEPNORM_BETA)
    ln_g = 1.0 + 0.02 * jax.random.normal(ks[14], (DEPTH, 2, D_MODEL), f32)
    ln_b = 0.02 * jax.random.normal(ks[15], (DEPTH, 2, D_MODEL), f32)
    return {"x": x, "ab_w_in": ab_w_in, "ab_forget_bias": ab_forget_bias,
            "ab_w_out": ab_w_out, "rel_bias": rel_bias, "mla_w_in": mla_w_in,
            "mla_q_norm": mla_q_norm, "mla_kv_norm": mla_kv_norm, "mla_w_uq": mla_w_uq,
            "mla_w_ukv": mla_w_ukv, "mla_w_out": mla_w_out, "ffn_w_gate": ffn_w_gate,
            "ffn_w_up": ffn_w_up, "ffn_w_down": ffn_w_down, "ln_g": ln_g, "ln_b": ln_b}


def reference(x, ab_w_in, ab_forget_bias, ab_w_out, rel_bias, mla_w_in, mla_q_norm,
              mla_kv_norm, mla_w_uq, mla_w_ukv, mla_w_out, ffn_w_gate, ffn_w_up,
              ffn_w_down, ln_g, ln_b):
    S = x.shape[1]
    cos, sin = rope_tables(S, x.dtype)
    for layer in range(DEPTH):
        j = layer // 2
        if layer % 2 == 0:
            y = fox_moba_mixer(x, ab_w_in[j], ab_forget_bias[j], ab_w_out[j], rel_bias)
        else:
            y = mla_mixer(x, mla_w_in[j], mla_q_norm[j], mla_kv_norm[j], mla_w_uq[j],
                          mla_w_ukv[j], mla_w_out[j], cos, sin)
        x = layer_norm(DEEPNORM_ALPHA * x + y, ln_g[layer, 0], ln_b[layer, 0])
        y = swiglu(x, ffn_w_gate[layer], ffn_w_up[layer], ffn_w_down[layer])
        x = layer_norm(DEEPNORM_ALPHA * x + y, ln_g[layer, 1], ln_b[layer, 1])
    return x
```

```python
import functools
import math

import numpy as np
import jax
import jax.numpy as jnp
from jax import lax
from jax.experimental import pallas as pl
from jax.experimental.pallas import tpu as pltpu

F32 = jnp.float32
BF16 = jnp.bfloat16

D_MODEL = 2048
DEPTH = 4
HEAD_DIM = 128
N_HEADS_A = 8
N_HEADS_B = 8
MOBA_BLOCK = 256
MOBA_TOPK = 3
N_BUCKETS = 32
MAX_DISTANCE = 128
MLA_HEADS = 16
MLA_Q_RANK = 512
MLA_KV_RANK = 512
MLA_NOPE = 128
MLA_ROPE = 64
MLA_V = 128
ROPE_THETA = 10000.0
DEEPNORM_ALPHA = (2 * DEPTH) ** 0.25
DA = N_HEADS_A * HEAD_DIM
DB = N_HEADS_B * HEAD_DIM
LN_EPS = 1e-5
RMS_EPS = 1e-6

LANES = 128
VMEM_BUDGET_BYTES = 56 * 2**20
MASK_VALUE = -1e30


def _vmem_limit(block_bytes, scratch_bytes=0, temp_bytes=0):
    est = 2 * block_bytes + scratch_bytes + temp_bytes + (4 << 20)
    return int(min(max(est, 16 << 20), VMEM_BUDGET_BYTES))


def _params(vmem_bytes, ngrid):
    return pltpu.CompilerParams(
        dimension_semantics=("arbitrary",) * ngrid, vmem_limit_bytes=vmem_bytes)


def _mm_heads_kernel(x_ref, w_ref, o_ref, *, width):
    acc = jnp.dot(x_ref[...], w_ref[...], preferred_element_type=F32)
    for hh in range(o_ref.shape[0]):
        o_ref[hh] = acc[:, hh * width:(hh + 1) * width].astype(o_ref.dtype)


def mm_heads(x, w, *, tm, tn, width=HEAD_DIM):
    M, K = x.shape
    N = w.shape[1]
    nh = tn // width
    blk = tm * K * 2 + K * tn * 2 + tm * tn * 2
    return pl.pallas_call(
        functools.partial(_mm_heads_kernel, width=width),
        grid=(M // tm, N // tn),
        in_specs=[pl.BlockSpec((tm, K), lambda i, j: (i, 0)),
                  pl.BlockSpec((K, tn), lambda i, j: (0, j))],
        out_specs=pl.BlockSpec((nh, tm, width), lambda i, j: (j, i, 0)),
        out_shape=jax.ShapeDtypeStruct((N // width, M, width), BF16),
        compiler_params=_params(_vmem_limit(blk, temp_bytes=tm * tn * 4), 2),
        name="mm_heads",
    )(x, w)


def _fox_gate_kernel(x_ref, wf_ref, bf_ref, ccol_ref, crow_ref, carry_ref, *, tb, nh):
    @pl.when(pl.program_id(0) == 0)
    def _():
        carry_ref[...] = jnp.zeros_like(carry_ref)

    z = jnp.dot(x_ref[...], wf_ref[...], preferred_element_type=F32) + bf_ref[...]
    lf = jnp.minimum(z, 0.0) - jnp.log1p(jnp.exp(-jnp.abs(z)))
    row = lax.broadcasted_iota(jnp.int32, (tb, tb), 0)
    col = lax.broadcasted_iota(jnp.int32, (tb, tb), 1)
    tri = jnp.where(row >= col, 1.0, 0.0).astype(F32)
    cs = jnp.dot(tri, lf, preferred_element_type=F32,
                 precision=lax.Precision.HIGHEST) + carry_ref[...]
    carry_ref[...] = cs[tb - 1:tb, :]
    for h in range(nh):
        ccol_ref[h] = jnp.broadcast_to(cs[:, h:h + 1], (tb, LANES))
    crow_ref[...] = jnp.transpose(cs)[:nh, :]


def fox_gate(xb, wf_pad, bf_pad, *, tb=512, nh=N_HEADS_A):
    S, K = xb.shape
    blk = tb * K * 2 + K * LANES * 2 + nh * tb * LANES * 4 + nh * tb * 4
    return pl.pallas_call(
        functools.partial(_fox_gate_kernel, tb=tb, nh=nh),
        grid=(S // tb,),
        in_specs=[pl.BlockSpec((tb, K), lambda i: (i, 0)),
                  pl.BlockSpec((K, LANES), lambda i: (0, 0)),
                  pl.BlockSpec((1, LANES), lambda i: (0, 0))],
        out_specs=[pl.BlockSpec((nh, tb, LANES), lambda i: (0, i, 0)),
                   pl.BlockSpec((nh, tb), lambda i: (0, i))],
        out_shape=[jax.ShapeDtypeStruct((nh, S, LANES), F32),
                   jax.ShapeDtypeStruct((nh, S), F32)],
        scratch_shapes=[pltpu.VMEM((1, LANES), F32)],
        compiler_params=_params(_vmem_limit(blk, temp_bytes=4 * tb * tb * 4), 1),
        name="fox_gate",
    )(xb, wf_pad, bf_pad)


def _flash_kernel(*refs, t, scale, decay):
    if decay:
        q_ref, k_ref, v_ref, ccol_ref, crow_ref, o_ref = refs
    else:
        q_ref, k_ref, v_ref, o_ref = refs
    qi = pl.program_id(1)
    dv = v_ref.shape[-1]
    q = q_ref[0]
    if decay:
        cq = ccol_ref[0][:, :1]

    def step(kb, carry, masked):
        m, l, acc = carry
        ks = pl.multiple_of(kb * t, t)
        k = k_ref[0, pl.ds(ks, t), :]
        v = v_ref[0, pl.ds(ks, t), :]
        s = lax.dot_general(q, k, (((1,), (1,)), ((), ())),
                            preferred_element_type=F32) * scale
        if decay:
            s = s + (cq - crow_ref[0, :, pl.ds(ks, t)])
        if masked:
            row = lax.broadcasted_iota(jnp.int32, (t, t), 0)
            col = lax.broadcasted_iota(jnp.int32, (t, t), 1)
            s = jnp.where(row >= col, s, MASK_VALUE)
        m_new = jnp.maximum(m, jnp.max(s, axis=1, keepdims=True))
        a = jnp.exp(m - m_new)
        p = jnp.exp(s - m_new)
        l = a * l + jnp.sum(p, axis=1, keepdims=True)
        acc = a * acc + jnp.dot(p.astype(BF16), v, preferred_element_type=F32)
        return m_new, l, acc

    init = (jnp.full((t, 1), MASK_VALUE, F32), jnp.zeros((t, 1), F32),
            jnp.zeros((t, dv), F32))
    carry = lax.fori_loop(0, qi, functools.partial(step, masked=False), init)
    _, l, acc = step(qi, carry, True)
    o_ref[...] = (acc / l).astype(o_ref.dtype)


def flash_attention(q_arr, k_arr, v_arr, *, n_heads, q_off, k_off, v_off, scale,
                    ccol=None, crow=None, t=512):
    S, dq = q_arr.shape[1], q_arr.shape[2]
    dv = v_arr.shape[2]
    decay = ccol is not None
    in_specs = [pl.BlockSpec((1, t, dq), lambda h, i: (h + q_off, i, 0)),
                pl.BlockSpec((1, S, dq), lambda h, i: (h + k_off, 0, 0)),
                pl.BlockSpec((1, S, dv), lambda h, i: (h + v_off, 0, 0))]
    args = [q_arr, k_arr, v_arr]
    blk = t * dq * 2 + S * dq * 2 + S * dv * 2 + t * dv * 2
    if decay:
        in_specs += [pl.BlockSpec((1, t, LANES), lambda h, i: (h, i, 0)),
                     pl.BlockSpec((1, 1, S), lambda h, i: (h, 0, 0))]
        args += [ccol, crow]
        blk += t * LANES * 4 + S * 4 * 8
    return pl.pallas_call(
        functools.partial(_flash_kernel, t=t, scale=scale, decay=decay),
        grid=(n_heads, S // t),
        in_specs=in_specs,
        out_specs=pl.BlockSpec((t, dv), lambda h, i: (i, h)),
        out_shape=jax.ShapeDtypeStruct((S, n_heads * dv), BF16),
        compiler_params=_params(_vmem_limit(blk, temp_bytes=6 * t * t * 4), 2),
        name="flash_decay" if decay else "flash",
    )(*args)


def _moba_kernel(rb_ref, q_ref, k_ref, v_ref, bidx0_ref, bidx1_ref, o_ref,
                 kmean_ref, t0_ref, t1_ref, m_ref, l_ref, acc_ref, *, nb, scale):
    B = MOBA_BLOCK
    h = pl.program_id(0)
    i = pl.program_id(1)
    far_bias = rb_ref[N_BUCKETS - 1, h]

    @pl.when(i == 0)
    def _():
        kf = k_ref[0].astype(F32).reshape(nb, B, HEAD_DIM)
        kmean_ref[...] = jnp.zeros_like(kmean_ref)
        kmean_ref[0:nb, :] = jnp.sum(kf, axis=1) * (1.0 / B)
        t0 = jnp.full((B, B), MASK_VALUE, F32)
        t1 = jnp.zeros((B, B), F32)
        bi0 = bidx0_ref[...]
        bi1 = bidx1_ref[...]
        for b in range(N_BUCKETS):
            val = rb_ref[b, h]
            t0 = jnp.where(bi0 == b, val, t0)
            t1 = jnp.where(bi1 == b, val - far_bias, t1)
        t0_ref[...] = t0
        t1_ref[...] = t1

    q = q_ref[0]
    lane = lax.broadcasted_iota(jnp.int32, (B, LANES), 1)
    g = lax.dot_general(q.astype(F32), kmean_ref[...], (((1,), (1,)), ((), ())),
                        preferred_element_type=F32, precision=lax.Precision.HIGHEST)
    neg_inf = -jnp.inf
    g = jnp.where(lane < i, g, neg_inf)
    mb = jnp.full((B, LANES), MASK_VALUE, F32)
    lane_f = lane.astype(F32)
    for _ in range(MOBA_TOPK):
        mx = jnp.max(g, axis=1, keepdims=True)
        hit = jnp.logical_and(g == mx, mx > neg_inf)
        idx = jnp.min(jnp.where(hit, lane_f, float(LANES)), axis=1, keepdims=True)
        pick = lane_f == idx
        mb = jnp.where(pick, far_bias, mb)
        g = jnp.where(pick, neg_inf, g)

    m_ref[...] = jnp.full_like(m_ref, MASK_VALUE)
    l_ref[...] = jnp.zeros_like(l_ref)
    acc_ref[...] = jnp.zeros_like(acc_ref)

    def scores(n):
        ks = pl.multiple_of(n * B, B)
        k = k_ref[0, pl.ds(ks, B), :]
        v = v_ref[0, pl.ds(ks, B), :]
        s = lax.dot_general(q, k, (((1,), (1,)), ((), ())),
                            preferred_element_type=F32) * scale
        return s, v

    def update(s, v):
        m_prev = m_ref[...]
        m_new = jnp.maximum(m_prev, jnp.max(s, axis=1, keepdims=True))
        a = jnp.exp(m_prev - m_new)
        p = jnp.exp(s - m_new)
        l_ref[...] = a * l_ref[...] + jnp.sum(p, axis=1, keepdims=True)
        acc_ref[...] = a * acc_ref[...] + jnp.dot(p.astype(BF16), v,
                                                  preferred_element_type=F32)
        m_ref[...] = m_new

    def rowmask(n):
        return jnp.sum(jnp.where(lane == n, mb, 0.0), axis=1, keepdims=True)

    def far(n, c):
        s, v = scores(n)
        update(s + rowmask(n), v)
        return c

    lax.fori_loop(0, jnp.maximum(i - 1, 0), far, 0)

    @pl.when(i >= 1)
    def _():
        s, v = scores(i - 1)
        update(s + t1_ref[...] + rowmask(i - 1), v)

    s, v = scores(i)
    update(s + t0_ref[...], v)
    o_ref[...] = (acc_ref[...] / l_ref[...]).astype(o_ref.dtype)


def _t5_bucket_table(n):
    rel = np.arange(n)
    max_exact = N_BUCKETS // 2
    nf = np.maximum(rel, 1).astype(np.float32)
    large = max_exact + (np.log(nf / np.float32(max_exact))
                         / np.float32(math.log(MAX_DISTANCE / max_exact))
                         * np.float32(N_BUCKETS - max_exact)).astype(np.int32)
    large = np.minimum(large, N_BUCKETS - 1)
    return np.where(rel < max_exact, rel, large).astype(np.int32)


def moba_attention(heads, rel_bias, *, n_heads, q_off, k_off, v_off):
    S = heads.shape[1]
    B = MOBA_BLOCK
    nb = S // B
    bucket = _t5_bucket_table(2 * B)
    d = np.arange(B)[:, None] - np.arange(B)[None, :]
    bidx0 = np.where(d >= 0, bucket[np.maximum(d, 0)], -1).astype(np.int32)
    bidx1 = bucket[d + B].astype(np.int32)
    blk = B * HEAD_DIM * 2 + 2 * S * HEAD_DIM * 2 + 2 * B * B * 4 + B * HEAD_DIM * 2
    scr = (LANES * HEAD_DIM + 2 * B * B + 2 * B * LANES + B * HEAD_DIM) * 4
    return pl.pallas_call(
        functools.partial(_moba_kernel, nb=nb, scale=HEAD_DIM ** -0.5),
        grid=(n_heads, nb),
        in_specs=[pl.BlockSpec(memory_space=pltpu.SMEM),
                  pl.BlockSpec((1, B, HEAD_DIM), lambda h, i: (h + q_off, i, 0)),
                  pl.BlockSpec((1, S, HEAD_DIM), lambda h, i: (h + k_off, 0, 0)),
                  pl.BlockSpec((1, S, HEAD_DIM), lambda h, i: (h + v_off, 0, 0)),
                  pl.BlockSpec((B, B), lambda h, i: (0, 0)),
                  pl.BlockSpec((B, B), lambda h, i: (0, 0))],
        out_specs=pl.BlockSpec((B, HEAD_DIM), lambda h, i: (i, h)),
        out_shape=jax.ShapeDtypeStruct((S, n_heads * HEAD_DIM), BF16),
        scratch_shapes=[pltpu.VMEM((LANES, HEAD_DIM), F32),
                        pltpu.VMEM((B, B), F32), pltpu.VMEM((B, B), F32),
                        pltpu.VMEM((B, 1), F32), pltpu.VMEM((B, 1), F32),
                        pltpu.VMEM((B, HEAD_DIM), F32)],
        compiler_params=_params(
            _vmem_limit(blk, scr, temp_bytes=S * HEAD_DIM * 4 + 8 * B * B * 4), 2),
        name="moba",
    )(rel_bias, heads, heads, heads, jnp.asarray(bidx0), jnp.asarray(bidx1))


def _mm_res_ln_kernel(a_ref, w_ref, x_ref, g_ref, b_ref, o_ref, obf_ref, *scratch, nk):
    part = jnp.dot(a_ref[...], w_ref[...], preferred_element_type=F32)

    def finish(y):
        z = DEEPNORM_ALPHA * x_ref[...] + y
        mu = jnp.mean(z, axis=-1, keepdims=True)
        zc = z - mu
        var = jnp.mean(zc * zc, axis=-1, keepdims=True)
        out = zc * lax.rsqrt(var + LN_EPS) * g_ref[...] + b_ref[...]
        o_ref[...] = out
        obf_ref[...] = out.astype(BF16)

    if nk == 1:
        finish(part)
        return
    acc_ref, = scratch
    k = pl.program_id(1)

    @pl.when(k == 0)
    def _():
        acc_ref[...] = part

    @pl.when(k > 0)
    def _():
        acc_ref[...] += part

    @pl.when(k == nk - 1)
    def _():
        finish(acc_ref[...])


def mm_res_ln(a, w, x, g, b, *, tm, tk):
    M, K = a.shape
    N = w.shape[1]
    nk = K // tk
    blk = tm * tk * 2 + tk * N * 2 + tm * N * 4 + 2 * N * 4 + tm * N * 4 + tm * N * 2
    scratch = [pltpu.VMEM((tm, N), F32)] if nk > 1 else []
    return pl.pallas_call(
        functools.partial(_mm_res_ln_kernel, nk=nk),
        grid=(M // tm, nk),
        in_specs=[pl.BlockSpec((tm, tk), lambda i, k: (i, k)),
                  pl.BlockSpec((tk, N), lambda i, k: (k, 0)),
                  pl.BlockSpec((tm, N), lambda i, k: (i, 0)),
                  pl.BlockSpec((1, N), lambda i, k: (0, 0)),
                  pl.BlockSpec((1, N), lambda i, k: (0, 0))],
        out_specs=[pl.BlockSpec((tm, N), lambda i, k: (i, 0)),
                   pl.BlockSpec((tm, N), lambda i, k: (i, 0))],
        out_shape=[jax.ShapeDtypeStruct((M, N), F32),
                   jax.ShapeDtypeStruct((M, N), BF16)],
        scratch_shapes=scratch,
        compiler_params=_params(
            _vmem_limit(blk, tm * N * 4 if nk > 1 else 0, temp_bytes=3 * tm * N * 4), 2),
        name="mm_res_ln",
    )(a, w, x, g, b)


def _ffn_gu_kernel(x_ref, wg_ref, wu_ref, o_ref):
    x = x_ref[...]
    g = jnp.dot(x, wg_ref[...], preferred_element_type=F32)
    u = jnp.dot(x, wu_ref[...], preferred_element_type=F32)
    o_ref[...] = (g * jax.nn.sigmoid(g) * u).astype(o_ref.dtype)


def ffn_gate_up(xb, wg, wu, *, tm, tf):
    M, K = xb.shape
    F = wg.shape[1]
    blk = tm * K * 2 + 2 * K * tf * 2 + tm * tf * 2
    return pl.pallas_call(
        _ffn_gu_kernel,
        grid=(M // tm, F // tf),
        in_specs=[pl.BlockSpec((tm, K), lambda i, j: (i, 0)),
                  pl.BlockSpec((K, tf), lambda i, j: (0, j)),
                  pl.BlockSpec((K, tf), lambda i, j: (0, j))],
        out_specs=pl.BlockSpec((tm, tf), lambda i, j: (i, j)),
        out_shape=jax.ShapeDtypeStruct((M, F), BF16),
        compiler_params=_params(_vmem_limit(blk, temp_bytes=4 * tm * tf * 4), 2),
        name="ffn_gate_up",
    )(xb, wg, wu)


def _rope_lanes(r, cos_t, sin_a, sin_b):
    return (r * cos_t + pltpu.roll(r, LANES - MLA_ROPE // 2, 1) * sin_a
            + pltpu.roll(r, MLA_ROPE // 2, 1) * sin_b)


def _rms(x, g):
    return x * lax.rsqrt(jnp.mean(x * x, axis=-1, keepdims=True) + RMS_EPS) * g


def _mla_in_kernel(x_ref, w_ref, gq_ref, gkv_ref, cos_ref, sa_ref, sb_ref,
                   cq_ref, ckv_ref, kr_ref):
    h = jnp.dot(x_ref[...], w_ref[...], preferred_element_type=F32)
    cq_ref[...] = _rms(h[:, :MLA_Q_RANK], gq_ref[...]).astype(BF16)
    ckv_ref[...] = _rms(h[:, MLA_Q_RANK:MLA_Q_RANK + MLA_KV_RANK], gkv_ref[...]).astype(BF16)
    r = h[:, MLA_Q_RANK + MLA_KV_RANK:]
    kr_ref[...] = _rope_lanes(r, cos_ref[...], sa_ref[...], sb_ref[...]).astype(BF16)


def mla_in(xb, w_pad, gq, gkv, cos_t, sin_a, sin_b, *, tm=512):
    S, K = xb.shape
    N = w_pad.shape[1]
    blk = (tm * K * 2 + K * N * 2 + 3 * tm * LANES * 4
           + tm * (MLA_Q_RANK + MLA_KV_RANK + LANES) * 2)
    row = lambda i: (i, 0)
    fixed = lambda i: (0, 0)
    return pl.pallas_call(
        _mla_in_kernel,
        grid=(S // tm,),
        in_specs=[pl.BlockSpec((tm, K), row), pl.BlockSpec((K, N), fixed),
                  pl.BlockSpec((1, MLA_Q_RANK), fixed), pl.BlockSpec((1, MLA_KV_RANK), fixed),
                  pl.BlockSpec((tm, LANES), row), pl.BlockSpec((tm, LANES), row),
                  pl.BlockSpec((tm, LANES), row)],
        out_specs=[pl.BlockSpec((tm, MLA_Q_RANK), row), pl.BlockSpec((tm, MLA_KV_RANK), row),
                   pl.BlockSpec((tm, LANES), row)],
        out_shape=[jax.ShapeDtypeStruct((S, MLA_Q_RANK), BF16),
                   jax.ShapeDtypeStruct((S, MLA_KV_RANK), BF16),
                   jax.ShapeDtypeStruct((S, LANES), BF16)],
        compiler_params=_params(_vmem_limit(blk, temp_bytes=3 * tm * N * 4), 1),
        name="mla_in",
    )(xb, w_pad, gq, gkv, cos_t, sin_a, sin_b)


def _mla_q_kernel(cq_ref, w_ref, cos_ref, sa_ref, sb_ref, q_ref):
    acc = jnp.dot(cq_ref[...], w_ref[...], preferred_element_type=F32)
    cos_t, sin_a, sin_b = cos_ref[...], sa_ref[...], sb_ref[...]
    for hh in range(q_ref.shape[0]):
        base = hh * 2 * LANES
        q_ref[hh, :, 0:LANES] = acc[:, base:base + LANES].astype(BF16)
        r = acc[:, base + LANES:base + 2 * LANES]
        q_ref[hh, :, LANES:2 * LANES] = _rope_lanes(r, cos_t, sin_a, sin_b).astype(BF16)


def mla_q(cq, w_pad, cos_t, sin_a, sin_b, *, tm=512):
    S, K = cq.shape
    N = w_pad.shape[1]
    nh = N // (2 * LANES)
    blk = tm * K * 2 + K * N * 2 + 3 * tm * LANES * 4 + tm * N * 2
    row = lambda i: (i, 0)
    return pl.pallas_call(
        _mla_q_kernel,
        grid=(S // tm,),
        in_specs=[pl.BlockSpec((tm, K), row), pl.BlockSpec((K, N), lambda i: (0, 0)),
                  pl.BlockSpec((tm, LANES), row), pl.BlockSpec((tm, LANES), row),
                  pl.BlockSpec((tm, LANES), row)],
        out_specs=pl.BlockSpec((nh, tm, 2 * LANES), lambda i: (0, i, 0)),
        out_shape=jax.ShapeDtypeStruct((nh, S, 2 * LANES), BF16),
        compiler_params=_params(_vmem_limit(blk, temp_bytes=2 * tm * N * 4), 1),
        name="mla_q",
    )(cq, w_pad, cos_t, sin_a, sin_b)


def _mla_kv_kernel(ckv_ref, w_ref, kr_ref, k_ref, v_ref):
    acc = jnp.dot(ckv_ref[...], w_ref[...], preferred_element_type=F32)
    kr = kr_ref[...]
    for hh in range(k_ref.shape[0]):
        base = hh * (MLA_NOPE + MLA_V)
        k_ref[hh, :, 0:LANES] = acc[:, base:base + MLA_NOPE].astype(BF16)
        k_ref[hh, :, LANES:2 * LANES] = kr
        v_ref[hh] = acc[:, base + MLA_NOPE:base + MLA_NOPE + MLA_V].astype(BF16)


def mla_kv(ckv, w, kr, *, tm=512):
    S, K = ckv.shape
    N = w.shape[1]
    nh = N // (MLA_NOPE + MLA_V)
    blk = tm * K * 2 + K * N * 2 + tm * LANES * 2 + nh * tm * (2 * LANES + MLA_V) * 2
    row = lambda i: (i, 0)
    return pl.pallas_call(
        _mla_kv_kernel,
        grid=(S // tm,),
        in_specs=[pl.BlockSpec((tm, K), row), pl.BlockSpec((K, N), lambda i: (0, 0)),
                  pl.BlockSpec((tm, LANES), row)],
        out_specs=[pl.BlockSpec((nh, tm, 2 * LANES), lambda i: (0, i, 0)),
                   pl.BlockSpec((nh, tm, MLA_V), lambda i: (0, i, 0))],
        out_shape=[jax.ShapeDtypeStruct((nh, S, 2 * LANES), BF16),
                   jax.ShapeDtypeStruct((nh, S, MLA_V), BF16)],
        compiler_params=_params(_vmem_limit(blk, temp_bytes=2 * tm * N * 4), 1),
        name="mla_kv",
    )(ckv, w, kr)


def _rope_lane_tables(S):
    half = MLA_ROPE // 2
    inv = ROPE_THETA ** (-jnp.arange(0, MLA_ROPE, 2, dtype=F32) / MLA_ROPE)
    ang = jnp.arange(S, dtype=F32)[:, None] * inv[None, :]
    cos, sin = jnp.cos(ang), jnp.sin(ang)
    z = jnp.zeros((S, half), F32)
    z2 = jnp.zeros((S, LANES - MLA_ROPE), F32)
    cos_t = jnp.concatenate([cos, cos, z2], axis=1)
    sin_a = jnp.concatenate([-sin, z, z2], axis=1)
    sin_b = jnp.concatenate([z, sin, z2], axis=1)
    return cos_t, sin_a, sin_b


def _pad_cols(w, n):
    return jnp.pad(w, ((0, 0), (0, n - w.shape[1])))


def _ffn_block(x, xb, wg, wu, wd, g, b):
    hmid = ffn_gate_up(xb, wg.astype(BF16), wu.astype(BF16), tm=1024, tf=512)
    return mm_res_ln(hmid, wd.astype(BF16), x, g, b, tm=512, tk=512)


def kernel(x, ab_w_in, ab_forget_bias, ab_w_out, rel_bias, mla_w_in, mla_q_norm,
           mla_kv_norm, mla_w_uq, mla_w_ukv, mla_w_out, ffn_w_gate, ffn_w_up,
           ffn_w_down, ln_g, ln_b):
    S = x.shape[1]
    xf = x.reshape(S, D_MODEL)
    xb = xf.astype(BF16)
    cos_t, sin_a, sin_b = _rope_lane_tables(S)
    nf = 3 * DA + N_HEADS_A

    for layer in range(DEPTH):
        j = layer // 2
        g0, b0 = ln_g[layer, 0][None, :], ln_b[layer, 0][None, :]
        g1, b1 = ln_g[layer, 1][None, :], ln_b[layer, 1][None, :]
        if layer % 2 == 0:
            w_in = ab_w_in[j]
            w_qkv = jnp.concatenate([w_in[:, :3 * DA], w_in[:, nf:]], axis=1).astype(BF16)
            w_f = _pad_cols(w_in[:, 3 * DA:nf], LANES).astype(BF16)
            b_f = _pad_cols(ab_forget_bias[j][None, :], LANES)
            heads = mm_heads(xb, w_qkv, tm=1024, tn=512)
            ccol, crow = fox_gate(xb, w_f, b_f)
            ya = flash_attention(heads, heads, heads, n_heads=N_HEADS_A, q_off=0,
                                 k_off=N_HEADS_A, v_off=2 * N_HEADS_A,
                                 scale=HEAD_DIM ** -0.5, ccol=ccol,
                                 crow=crow.reshape(N_HEADS_A, 1, S))
            yb = moba_attention(heads, rel_bias, n_heads=N_HEADS_B, q_off=3 * N_HEADS_A,
                                k_off=3 * N_HEADS_A + N_HEADS_B,
                                v_off=3 * N_HEADS_A + 2 * N_HEADS_B)
            y = jnp.concatenate([ya, yb], axis=1)
            w_out = ab_w_out[j].astype(BF16)
        else:
            w_in = _pad_cols(mla_w_in[j], MLA_Q_RANK + MLA_KV_RANK + LANES).astype(BF16)
            cq, ckv, kr = mla_in(xb, w_in, mla_q_norm[j][None, :], mla_kv_norm[j][None, :],
                                 cos_t, sin_a, sin_b)
            w_uq = mla_w_uq[j].reshape(MLA_Q_RANK, MLA_HEADS, MLA_NOPE + MLA_ROPE)
            w_uq = jnp.pad(w_uq, ((0, 0), (0, 0), (0, 2 * LANES - MLA_NOPE - MLA_ROPE)))
            w_uq = w_uq.reshape(MLA_Q_RANK, MLA_HEADS * 2 * LANES).astype(BF16)
            q_full = mla_q(cq, w_uq, cos_t, sin_a, sin_b)
            k_full, v = mla_kv(ckv, mla_w_ukv[j].astype(BF16), kr)
            y = flash_attention(q_full, k_full, v, n_heads=MLA_HEADS, q_off=0, k_off=0,
                                v_off=0, scale=(MLA_NOPE + MLA_ROPE) ** -0.5)
            w_out = mla_w_out[j].astype(BF16)
        xf, xb = mm_res_ln(y, w_out, xf, g0, b0, tm=512, tk=1024)
        xf, xb = _ffn_block(xf, xb, ffn_w_gate[layer], ffn_w_up[layer], ffn_w_down[layer],
                            g1, b1)
    return xf.reshape(1, S, D_MODEL)
```

```python
import functools
import math

import numpy as np
import jax
import jax.numpy as jnp
from jax import lax
from jax.experimental import pallas as pl
from jax.experimental.pallas import tpu as pltpu

F32 = jnp.float32
BF16 = jnp.bfloat16

D_MODEL = 2048
DEPTH = 4
HEAD_DIM = 128
N_HEADS_A = 8
N_HEADS_B = 8
MOBA_BLOCK = 256
MOBA_TOPK = 3
N_BUCKETS = 32
MAX_DISTANCE = 128
MLA_HEADS = 16
MLA_Q_RANK = 512
MLA_KV_RANK = 512
MLA_NOPE = 128
MLA_ROPE = 64
MLA_V = 128
ROPE_THETA = 10000.0
DEEPNORM_ALPHA = (2 * DEPTH) ** 0.25
DA = N_HEADS_A * HEAD_DIM
DB = N_HEADS_B * HEAD_DIM
LN_EPS = 1e-5
RMS_EPS = 1e-6
LOG2E = math.log2(math.e)

LANES = 128
VMEM_BUDGET_BYTES = 56 * 2**20
MASK_VALUE = -1e30
ATTN_TILE = 2 * MOBA_BLOCK


def _vmem_limit(block_bytes, scratch_bytes=0, temp_bytes=0):
    est = 2 * block_bytes + scratch_bytes + temp_bytes + (4 << 20)
    return int(min(max(est, 16 << 20), VMEM_BUDGET_BYTES))


def _params(vmem_bytes, ngrid):
    return pltpu.CompilerParams(
        dimension_semantics=("arbitrary",) * ngrid, vmem_limit_bytes=vmem_bytes)


def _split3(x):
    hi = x.astype(BF16).astype(F32)
    r1 = x - hi
    lo = r1.astype(BF16).astype(F32)
    return hi, lo, r1 - lo


def _mm_heads_kernel(x_ref, w_ref, cs_ref, o_ref, *, width):
    acc = jnp.dot(x_ref[...], w_ref[...], preferred_element_type=F32) * cs_ref[...]
    for hh in range(o_ref.shape[0]):
        o_ref[hh] = acc[:, hh * width:(hh + 1) * width].astype(o_ref.dtype)


def mm_heads(x, w, colscale, *, tm, tn, width=HEAD_DIM):
    M, K = x.shape
    N = w.shape[1]
    nh = tn // width
    blk = tm * K * 2 + K * tn * 2 + tm * tn * 2 + tn * 4
    return pl.pallas_call(
        functools.partial(_mm_heads_kernel, width=width),
        grid=(M // tm, N // tn),
        in_specs=[pl.BlockSpec((tm, K), lambda i, j: (i, 0)),
                  pl.BlockSpec((K, tn), lambda i, j: (0, j)),
                  pl.BlockSpec((1, tn), lambda i, j: (0, j))],
        out_specs=pl.BlockSpec((nh, tm, width), lambda i, j: (j, i, 0)),
        out_shape=jax.ShapeDtypeStruct((N // width, M, width), BF16),
        compiler_params=_params(_vmem_limit(blk, temp_bytes=2 * tm * tn * 4), 2),
        name="mm_heads",
    )(x, w, colscale)


def _fox_gate_kernel(x_ref, wf_ref, bf_ref, qaug_ref, kaug_ref, carry_ref, *, tb, nh):
    @pl.when(pl.program_id(0) == 0)
    def _():
        carry_ref[...] = jnp.zeros_like(carry_ref)

    z = jnp.dot(x_ref[...], wf_ref[...], preferred_element_type=F32) + bf_ref[...]
    lf = jnp.minimum(z, 0.0) - jnp.log1p(jnp.exp(-jnp.abs(z)))
    row = lax.broadcasted_iota(jnp.int32, (tb, tb), 0)
    col = lax.broadcasted_iota(jnp.int32, (tb, tb), 1)
    tri = jnp.where(row >= col, 1.0, 0.0).astype(F32)
    cs = jnp.dot(tri, lf, preferred_element_type=F32,
                 precision=lax.Precision.HIGHEST) + carry_ref[...]
    carry_ref[...] = cs[tb - 1:tb, :]
    c2 = cs * LOG2E
    lane = lax.broadcasted_iota(jnp.int32, (tb, LANES), 1)
    for h in range(nh):
        hi, lo, lo2 = _split3(jnp.broadcast_to(c2[:, h:h + 1], (tb, LANES)))
        qa = jnp.where(lane == 0, hi, jnp.where(lane == 1, lo, jnp.where(
            lane == 2, lo2, jnp.where(lane < 6, 1.0, 0.0))))
        ka = jnp.where(lane < 3, 1.0, jnp.where(lane == 3, -hi, jnp.where(
            lane == 4, -lo, jnp.where(lane == 5, -lo2, 0.0))))
        qaug_ref[h] = qa.astype(BF16)
        kaug_ref[h] = ka.astype(BF16)


def fox_gate(xb, wf_pad, bf_pad, *, tb=512, nh=N_HEADS_A):
    S, K = xb.shape
    blk = tb * K * 2 + K * LANES * 2 + 2 * nh * tb * LANES * 2
    aug = jax.ShapeDtypeStruct((nh, S, LANES), BF16)
    return pl.pallas_call(
        functools.partial(_fox_gate_kernel, tb=tb, nh=nh),
        grid=(S // tb,),
        in_specs=[pl.BlockSpec((tb, K), lambda i: (i, 0)),
                  pl.BlockSpec((K, LANES), lambda i: (0, 0)),
                  pl.BlockSpec((1, LANES), lambda i: (0, 0))],
        out_specs=[pl.BlockSpec((nh, tb, LANES), lambda i: (0, i, 0)),
                   pl.BlockSpec((nh, tb, LANES), lambda i: (0, i, 0))],
        out_shape=[aug, aug],
        scratch_shapes=[pltpu.VMEM((1, LANES), F32)],
        compiler_params=_params(_vmem_limit(blk, temp_bytes=4 * tb * tb * 4), 1),
        name="fox_gate",
    )(xb, wf_pad, bf_pad)


def _moba_gate_kernel(rb_ref, q_ref, k_ref, qaug_ref, kmean_ref, *, nb, tr):
    B = MOBA_BLOCK
    h = pl.program_id(0)
    i = pl.program_id(1)
    far_bias = rb_ref[N_BUCKETS - 1, h] * LOG2E

    @pl.when(i == 0)
    def _():
        kf = k_ref[0].astype(F32).reshape(nb, B, HEAD_DIM)
        kmean_ref[...] = jnp.zeros_like(kmean_ref)
        kmean_ref[0:nb, :] = jnp.sum(kf, axis=1) * (1.0 / B)

    lane = lax.broadcasted_iota(jnp.int32, (tr, LANES), 1)
    lane_f = lane.astype(F32)
    own = i * (tr // B) + lax.broadcasted_iota(jnp.int32, (tr, LANES), 0) // B
    g = lax.dot_general(q_ref[0].astype(F32), kmean_ref[...], (((1,), (1,)), ((), ())),
                        preferred_element_type=F32, precision=lax.Precision.HIGHEST)
    neg_inf = -jnp.inf
    g = jnp.where(lane < own, g, neg_inf)
    mb = jnp.where(lane == own, 0.0, jnp.where(lane < nb, MASK_VALUE, 0.0)).astype(F32)
    for _ in range(MOBA_TOPK):
        mx = jnp.max(g, axis=1, keepdims=True)
        hit = jnp.logical_and(g == mx, mx > neg_inf)
        idx = jnp.min(jnp.where(hit, lane_f, float(LANES)), axis=1, keepdims=True)
        pick = lane_f == idx
        mb = jnp.where(pick, far_bias, mb)
        g = jnp.where(pick, neg_inf, g)
    mb_hi = mb.astype(BF16).astype(F32)
    mb_lo = jnp.where(mb > 0.5 * MASK_VALUE, mb - mb_hi, 0.0)
    qaug_ref[0] = (mb_hi + pltpu.roll(mb_lo, nb, 1)).astype(BF16)


def moba_gate(heads, rel_bias, *, n_heads, q_off, k_off, tr=1024):
    S = heads.shape[1]
    nb = S // MOBA_BLOCK
    assert 2 * nb <= LANES
    blk = tr * HEAD_DIM * 2 + S * HEAD_DIM * 2 + tr * LANES * 2
    return pl.pallas_call(
        functools.partial(_moba_gate_kernel, nb=nb, tr=tr),
        grid=(n_heads, S // tr),
        in_specs=[pl.BlockSpec(memory_space=pltpu.SMEM),
                  pl.BlockSpec((1, tr, HEAD_DIM), lambda h, i: (h + q_off, i, 0)),
                  pl.BlockSpec((1, S, HEAD_DIM), lambda h, i: (h + k_off, 0, 0))],
        out_specs=pl.BlockSpec((1, tr, LANES), lambda h, i: (h, i, 0)),
        out_shape=jax.ShapeDtypeStruct((n_heads, S, LANES), BF16),
        scratch_shapes=[pltpu.VMEM((LANES, HEAD_DIM), F32)],
        compiler_params=_params(
            _vmem_limit(blk, LANES * HEAD_DIM * 4,
                        temp_bytes=S * HEAD_DIM * 4 + 12 * tr * LANES * 4), 2),
        name="moba_gate",
    )(rel_bias, heads, heads)


_KIND_ZERO, _KIND_MASK, _KIND_BIAS, _KIND_BIAS_MINUS_FAR = 0, 1, 2, 3


def _moba_bias_kernel(rb_ref, bucket_ref, kind_ref, o_ref):
    h = pl.program_id(0)
    far = rb_ref[N_BUCKETS - 1, h]
    bucket = bucket_ref[...]
    kind = kind_ref[...]
    val = jnp.zeros(bucket.shape, F32)
    for b in range(N_BUCKETS):
        val = jnp.where(bucket == b, rb_ref[b, h], val)
    val = jnp.where(kind == _KIND_BIAS_MINUS_FAR, val - far, val) * LOG2E
    o_ref[0] = jnp.where(kind == _KIND_ZERO, 0.0, jnp.where(kind == _KIND_MASK, MASK_VALUE, val))


def _t5_bucket_table(n):
    rel = np.arange(n)
    max_exact = N_BUCKETS // 2
    nf = np.maximum(rel, 1).astype(np.float32)
    large = max_exact + (np.log(nf / np.float32(max_exact))
                         / np.float32(math.log(MAX_DISTANCE / max_exact))
                         * np.float32(N_BUCKETS - max_exact)).astype(np.int32)
    large = np.minimum(large, N_BUCKETS - 1)
    return np.where(rel < max_exact, rel, large).astype(np.int32)


def _moba_bias_codes(t):
    B = MOBA_BLOCK
    table = _t5_bucket_table(t + 2 * B)
    rows = np.arange(t)[:, None]
    cols = np.arange(3 * t)[None, :] - t
    rel = rows - cols
    qblk = rows // B
    kblk = np.floor_divide(cols, B)
    bucket = table[np.clip(rel, 0, table.size - 1)] + np.zeros((t, 3 * t), np.int32)
    kind = np.full((t, 3 * t), _KIND_ZERO, np.int32)
    kind = np.where(kblk == qblk - 1, _KIND_BIAS_MINUS_FAR, kind)
    kind = np.where(kblk == qblk, np.where(rel >= 0, _KIND_BIAS, _KIND_MASK), kind)
    kind = np.where(cols >= t, _KIND_MASK, kind)
    return bucket.astype(np.int32), kind.astype(np.int32)


def moba_bias(rel_bias, *, n_heads, t):
    bucket, kind = _moba_bias_codes(t)
    blk = 3 * t * 3 * t * 4
    return pl.pallas_call(
        _moba_bias_kernel,
        grid=(n_heads,),
        in_specs=[pl.BlockSpec(memory_space=pltpu.SMEM),
                  pl.BlockSpec((t, 3 * t), lambda h: (0, 0)),
                  pl.BlockSpec((t, 3 * t), lambda h: (0, 0))],
        out_specs=pl.BlockSpec((1, t, 3 * t), lambda h: (h, 0, 0)),
        out_shape=jax.ShapeDtypeStruct((n_heads, t, 3 * t), F32),
        compiler_params=_params(_vmem_limit(blk, temp_bytes=4 * t * 3 * t * 4), 1),
        name="moba_bias",
    )(rel_bias, jnp.asarray(bucket), jnp.asarray(kind))


def _causal_bias(t):
    rows = np.arange(t)[:, None]
    cols = np.arange(3 * t)[None, :] - t
    return jnp.asarray(np.where(cols <= rows, 0.0, MASK_VALUE).astype(np.float32)[None])


def _ones_column(rows):
    lane = lax.broadcasted_iota(jnp.int32, (rows, LANES), 1)
    return jnp.where(lane == 0, 1.0, 0.0).astype(BF16)


def _qk(q, k):
    return lax.dot_general(q, k, (((1,), (1,)), ((), ())), preferred_element_type=F32)


def _softmax_step(s, v1, m, acc):
    m_new = jnp.maximum(m, jnp.max(s, axis=1, keepdims=True))
    a = jnp.exp2(m - m_new)
    p = jnp.exp2(s - m_new)
    acc = a * acc + jnp.dot(p.astype(BF16), v1, preferred_element_type=F32)
    return m_new, acc


def _flash_kernel(*refs, t, nparts, nsplit):
    q_refs = refs[:nparts]
    k_refs = refs[nparts:2 * nparts]
    v_ref, b3_ref, o_ref = refs[2 * nparts:]
    qi = pl.program_id(1)
    dv = v_ref.shape[-1]
    r = t // nsplit
    q = jnp.concatenate([ref[0] for ref in q_refs], axis=1)
    qs = [q[c * r:(c + 1) * r] for c in range(nsplit)]

    def keys(start, width):
        ks = pl.multiple_of(start, t)
        return jnp.concatenate([ref[0, pl.ds(ks, width), :] for ref in k_refs], axis=1)

    def values(start, width):
        ks = pl.multiple_of(start, t)
        return jnp.concatenate([v_ref[0, pl.ds(ks, width), :], _ones_column(width)], axis=1)

    def logits(k):
        return [_qk(qc, k) for qc in qs]

    def process(ss, v1, state, bias=None):
        out = []
        for c in range(nsplit):
            s = ss[c] if bias is None else ss[c] + bias[c * r:(c + 1) * r]
            out.append(_softmax_step(s, v1, *state[c]))
        return out

    n_plain = jnp.maximum(qi - 1, 0)
    state = [(jnp.full((r, 1), MASK_VALUE, F32), jnp.zeros((r, dv + LANES), F32))
             for _ in range(nsplit)]
    ss = logits(keys(0, t))

    def body(kb, carry):
        ss, state = carry
        ss_next = logits(keys((kb + 1) * t, t))
        return ss_next, process(ss, values(kb * t, t), state)

    ss, state = lax.fori_loop(0, n_plain - 1, body, (ss, state))
    state = lax.cond(n_plain >= 1,
                     lambda: process(ss, values((n_plain - 1) * t, t), state),
                     lambda: state)

    off = pl.multiple_of(jnp.where(qi == 0, t, 0), t)
    bias = b3_ref[0, :, pl.ds(off, 2 * t)]
    state = process(logits(keys(n_plain * t, 2 * t)), values(n_plain * t, 2 * t), state, bias)
    out = [acc[:, :dv] * (1.0 / acc[:, dv:dv + 1]) for _, acc in state]
    o_ref[...] = jnp.concatenate(out, axis=0).astype(o_ref.dtype)


def flash_attention(q_parts, k_parts, v_part, b3, *, n_heads, name, t=ATTN_TILE, nsplit=2):
    v, v_off = v_part
    S, dv = v.shape[1], v.shape[2]

    def head_map(off, rows):
        if off is None:
            return lambda h, i: (0, i if rows else 0, 0)
        return lambda h, i: (h + off, i if rows else 0, 0)

    in_specs, args, blk = [], [], S * dv * 2 + t * dv * 2 + t * 3 * t * 4
    for arr, off in q_parts:
        in_specs.append(pl.BlockSpec((1, t, arr.shape[2]), head_map(off, True)))
        args.append(arr)
        blk += t * arr.shape[2] * 2
    for arr, off in k_parts:
        in_specs.append(pl.BlockSpec((1, S, arr.shape[2]), head_map(off, False)))
        args.append(arr)
        blk += S * arr.shape[2] * 2
    in_specs.append(pl.BlockSpec((1, S, dv), head_map(v_off, False)))
    args.append(v)
    in_specs.append(pl.BlockSpec((1, t, 3 * t), head_map(0 if b3.shape[0] > 1 else None, False)))
    args.append(b3)
    return pl.pallas_call(
        functools.partial(_flash_kernel, t=t, nparts=len(q_parts), nsplit=nsplit),
        grid=(n_heads, S // t),
        in_specs=in_specs,
        out_specs=pl.BlockSpec((t, dv), lambda h, i: (i, h)),
        out_shape=jax.ShapeDtypeStruct((S, n_heads * dv), BF16),
        compiler_params=_params(_vmem_limit(blk, temp_bytes=10 * t * t * 4), 2),
        name=name,
    )(*args)


def _moba_key_onehot(S):
    nb = S // MOBA_BLOCK
    lanes = np.arange(LANES)[None, :]
    blk_of = (np.arange(S) // MOBA_BLOCK)[:, None]
    onehot = (lanes % nb == blk_of) & (lanes < 2 * nb)
    return jnp.asarray(onehot.astype(np.float32)[None], dtype=BF16)


def _mm_res_ln_kernel(a_ref, w_ref, x_ref, g_ref, b_ref, o_ref, obf_ref, *scratch, nk):
    part = jnp.dot(a_ref[...], w_ref[...], preferred_element_type=F32)

    def finish(y):
        z = DEEPNORM_ALPHA * x_ref[...] + y
        mu = jnp.mean(z, axis=-1, keepdims=True)
        zc = z - mu
        var = jnp.mean(zc * zc, axis=-1, keepdims=True)
        out = zc * lax.rsqrt(var + LN_EPS) * g_ref[...] + b_ref[...]
        o_ref[...] = out
        obf_ref[...] = out.astype(BF16)

    if nk == 1:
        finish(part)
        return
    acc_ref, = scratch
    k = pl.program_id(1)

    @pl.when(k == 0)
    def _():
        acc_ref[...] = part

    @pl.when(k > 0)
    def _():
        acc_ref[...] += part

    @pl.when(k == nk - 1)
    def _():
        finish(acc_ref[...])


def mm_res_ln(a, w, x, g, b, *, tm, tk):
    M, K = a.shape
    N = w.shape[1]
    nk = K // tk
    blk = tm * tk * 2 + tk * N * 2 + tm * N * 4 + 2 * N * 4 + tm * N * 4 + tm * N * 2
    scratch = [pltpu.VMEM((tm, N), F32)] if nk > 1 else []
    return pl.pallas_call(
        functools.partial(_mm_res_ln_kernel, nk=nk),
        grid=(M // tm, nk),
        in_specs=[pl.BlockSpec((tm, tk), lambda i, k: (i, k)),
                  pl.BlockSpec((tk, N), lambda i, k: (k, 0)),
                  pl.BlockSpec((tm, N), lambda i, k: (i, 0)),
                  pl.BlockSpec((1, N), lambda i, k: (0, 0)),
                  pl.BlockSpec((1, N), lambda i, k: (0, 0))],
        out_specs=[pl.BlockSpec((tm, N), lambda i, k: (i, 0)),
                   pl.BlockSpec((tm, N), lambda i, k: (i, 0))],
        out_shape=[jax.ShapeDtypeStruct((M, N), F32),
                   jax.ShapeDtypeStruct((M, N), BF16)],
        scratch_shapes=scratch,
        compiler_params=_params(
            _vmem_limit(blk, tm * N * 4 if nk > 1 else 0, temp_bytes=3 * tm * N * 4), 2),
        name="mm_res_ln",
    )(a, w, x, g, b)


def _ffn_gu_kernel(x_ref, wg_ref, wu_ref, o_ref):
    x = x_ref[...]
    g = jnp.dot(x, wg_ref[...], preferred_element_type=F32)
    u = jnp.dot(x, wu_ref[...], preferred_element_type=F32)
    o_ref[...] = (g * jax.nn.sigmoid(g) * u).astype(o_ref.dtype)


def ffn_gate_up(xb, wg, wu, *, tm, tf):
    M, K = xb.shape
    F = wg.shape[1]
    blk = tm * K * 2 + 2 * K * tf * 2 + tm * tf * 2
    return pl.pallas_call(
        _ffn_gu_kernel,
        grid=(M // tm, F // tf),
        in_specs=[pl.BlockSpec((tm, K), lambda i, j: (i, 0)),
                  pl.BlockSpec((K, tf), lambda i, j: (0, j)),
                  pl.BlockSpec((K, tf), lambda i, j: (0, j))],
        out_specs=pl.BlockSpec((tm, tf), lambda i, j: (i, j)),
        out_shape=jax.ShapeDtypeStruct((M, F), BF16),
        compiler_params=_params(_vmem_limit(blk, temp_bytes=4 * tm * tf * 4), 2),
        name="ffn_gate_up",
    )(xb, wg, wu)


def _rope_lanes(r, cos_t, sin_a, sin_b):
    return (r * cos_t + pltpu.roll(r, LANES - MLA_ROPE // 2, 1) * sin_a
            + pltpu.roll(r, MLA_ROPE // 2, 1) * sin_b)


def _rms(x, g):
    return x * lax.rsqrt(jnp.mean(x * x, axis=-1, keepdims=True) + RMS_EPS) * g


def _mla_in_kernel(x_ref, w_ref, gq_ref, gkv_ref, cos_ref, sa_ref, sb_ref,
                   cq_ref, ckv_ref, kr_ref):
    h = jnp.dot(x_ref[...], w_ref[...], preferred_element_type=F32)
    cq_ref[...] = _rms(h[:, :MLA_Q_RANK], gq_ref[...]).astype(BF16)
    ckv_ref[...] = _rms(h[:, MLA_Q_RANK:MLA_Q_RANK + MLA_KV_RANK], gkv_ref[...]).astype(BF16)
    r = h[:, MLA_Q_RANK + MLA_KV_RANK:]
    kr_ref[...] = _rope_lanes(r, cos_ref[...], sa_ref[...], sb_ref[...]).astype(BF16)


def mla_in(xb, w_pad, gq, gkv, cos_t, sin_a, sin_b, *, tm=512):
    S, K = xb.shape
    N = w_pad.shape[1]
    blk = (tm * K * 2 + K * N * 2 + 3 * tm * LANES * 4
           + tm * (MLA_Q_RANK + MLA_KV_RANK + LANES) * 2)
    row = lambda i: (i, 0)
    fixed = lambda i: (0, 0)
    return pl.pallas_call(
        _mla_in_kernel,
        grid=(S // tm,),
        in_specs=[pl.BlockSpec((tm, K), row), pl.BlockSpec((K, N), fixed),
                  pl.BlockSpec((1, MLA_Q_RANK), fixed), pl.BlockSpec((1, MLA_KV_RANK), fixed),
                  pl.BlockSpec((tm, LANES), row), pl.BlockSpec((tm, LANES), row),
                  pl.BlockSpec((tm, LANES), row)],
        out_specs=[pl.BlockSpec((tm, MLA_Q_RANK), row), pl.BlockSpec((tm, MLA_KV_RANK), row),
                   pl.BlockSpec((tm, LANES), row)],
        out_shape=[jax.ShapeDtypeStruct((S, MLA_Q_RANK), BF16),
                   jax.ShapeDtypeStruct((S, MLA_KV_RANK), BF16),
                   jax.ShapeDtypeStruct((S, LANES), BF16)],
        compiler_params=_params(_vmem_limit(blk, temp_bytes=3 * tm * N * 4), 1),
        name="mla_in",
    )(xb, w_pad, gq, gkv, cos_t, sin_a, sin_b)


def _mla_q_kernel(cq_ref, w_ref, cos_ref, sa_ref, sb_ref, q_ref, *, qscale):
    acc = jnp.dot(cq_ref[...], w_ref[...], preferred_element_type=F32) * qscale
    cos_t, sin_a, sin_b = cos_ref[...], sa_ref[...], sb_ref[...]
    for hh in range(q_ref.shape[0]):
        base = hh * 2 * LANES
        q_ref[hh, :, 0:LANES] = acc[:, base:base + LANES].astype(BF16)
        r = acc[:, base + LANES:base + 2 * LANES]
        q_ref[hh, :, LANES:2 * LANES] = _rope_lanes(r, cos_t, sin_a, sin_b).astype(BF16)


def mla_q(cq, w_pad, cos_t, sin_a, sin_b, *, qscale, tm=512):
    S, K = cq.shape
    N = w_pad.shape[1]
    nh = N // (2 * LANES)
    blk = tm * K * 2 + K * N * 2 + 3 * tm * LANES * 4 + tm * N * 2
    row = lambda i: (i, 0)
    return pl.pallas_call(
        functools.partial(_mla_q_kernel, qscale=qscale),
        grid=(S // tm,),
        in_specs=[pl.BlockSpec((tm, K), row), pl.BlockSpec((K, N), lambda i: (0, 0)),
                  pl.BlockSpec((tm, LANES), row), pl.BlockSpec((tm, LANES), row),
                  pl.BlockSpec((tm, LANES), row)],
        out_specs=pl.BlockSpec((nh, tm, 2 * LANES), lambda i: (0, i, 0)),
        out_shape=jax.ShapeDtypeStruct((nh, S, 2 * LANES), BF16),
        compiler_params=_params(_vmem_limit(blk, temp_bytes=2 * tm * N * 4), 1),
        name="mla_q",
    )(cq, w_pad, cos_t, sin_a, sin_b)


def _mla_kv_kernel(ckv_ref, w_ref, kr_ref, k_ref, v_ref):
    acc = jnp.dot(ckv_ref[...], w_ref[...], preferred_element_type=F32)
    kr = kr_ref[...]
    for hh in range(k_ref.shape[0]):
        base = hh * (MLA_NOPE + MLA_V)
        k_ref[hh, :, 0:LANES] = acc[:, base:base + MLA_NOPE].astype(BF16)
        k_ref[hh, :, LANES:2 * LANES] = kr
        v_ref[hh] = acc[:, base + MLA_NOPE:base + MLA_NOPE + MLA_V].astype(BF16)


def mla_kv(ckv, w, kr, *, tm=512):
    S, K = ckv.shape
    N = w.shape[1]
    nh = N // (MLA_NOPE + MLA_V)
    blk = tm * K * 2 + K * N * 2 + tm * LANES * 2 + nh * tm * (2 * LANES + MLA_V) * 2
    row = lambda i: (i, 0)
    return pl.pallas_call(
        _mla_kv_kernel,
        grid=(S // tm,),
        in_specs=[pl.BlockSpec((tm, K), row), pl.BlockSpec((K, N), lambda i: (0, 0)),
                  pl.BlockSpec((tm, LANES), row)],
        out_specs=[pl.BlockSpec((nh, tm, 2 * LANES), lambda i: (0, i, 0)),
                   pl.BlockSpec((nh, tm, MLA_V), lambda i: (0, i, 0))],
        out_shape=[jax.ShapeDtypeStruct((nh, S, 2 * LANES), BF16),
                   jax.ShapeDtypeStruct((nh, S, MLA_V), BF16)],
        compiler_params=_params(_vmem_limit(blk, temp_bytes=2 * tm * N * 4), 1),
        name="mla_kv",
    )(ckv, w, kr)


def _rope_lane_tables(S):
    half = MLA_ROPE // 2
    inv = ROPE_THETA ** (-jnp.arange(0, MLA_ROPE, 2, dtype=F32) / MLA_ROPE)
    ang = jnp.arange(S, dtype=F32)[:, None] * inv[None, :]
    cos, sin = jnp.cos(ang), jnp.sin(ang)
    z = jnp.zeros((S, half), F32)
    z2 = jnp.zeros((S, LANES - MLA_ROPE), F32)
    cos_t = jnp.concatenate([cos, cos, z2], axis=1)
    sin_a = jnp.concatenate([-sin, z, z2], axis=1)
    sin_b = jnp.concatenate([z, sin, z2], axis=1)
    return cos_t, sin_a, sin_b


def _pad_cols(w, n):
    return jnp.pad(w, ((0, 0), (0, n - w.shape[1])))


def _ffn_block(x, xb, wg, wu, wd, g, b):
    hmid = ffn_gate_up(xb, wg.astype(BF16), wu.astype(BF16), tm=1024, tf=512)
    return mm_res_ln(hmid, wd.astype(BF16), x, g, b, tm=512, tk=512)


def kernel(x, ab_w_in, ab_forget_bias, ab_w_out, rel_bias, mla_w_in, mla_q_norm,
           mla_kv_norm, mla_w_uq, mla_w_ukv, mla_w_out, ffn_w_gate, ffn_w_up,
           ffn_w_down, ln_g, ln_b):
    S = x.shape[1]
    xf = x.reshape(S, D_MODEL)
    xb = xf.astype(BF16)
    cos_t, sin_a, sin_b = _rope_lane_tables(S)
    nf = 3 * DA + N_HEADS_A
    qs = HEAD_DIM ** -0.5 * LOG2E
    colscale = np.ones((1, 3 * DA + 3 * DB), np.float32)
    colscale[:, :DA] = qs
    colscale[:, 3 * DA:3 * DA + DB] = qs
    colscale = jnp.asarray(colscale)
    causal_b3 = _causal_bias(ATTN_TILE)
    moba_b3 = moba_bias(rel_bias, n_heads=N_HEADS_B, t=ATTN_TILE)
    moba_kaug = _moba_key_onehot(S)
    hb = 3 * N_HEADS_A

    for layer in range(DEPTH):
        j = layer // 2
        g0, b0 = ln_g[layer, 0][None, :], ln_b[layer, 0][None, :]
        g1, b1 = ln_g[layer, 1][None, :], ln_b[layer, 1][None, :]
        if layer % 2 == 0:
            w_in = ab_w_in[j]
            w_qkv = jnp.concatenate([w_in[:, :3 * DA], w_in[:, nf:]], axis=1).astype(BF16)
            w_f = _pad_cols(w_in[:, 3 * DA:nf], LANES).astype(BF16)
            b_f = _pad_cols(ab_forget_bias[j][None, :], LANES)
            heads = mm_heads(xb, w_qkv, colscale, tm=1024, tn=512)
            qaug, kaug = fox_gate(xb, w_f, b_f)
            ya = flash_attention([(heads, 0), (qaug, 0)], [(heads, N_HEADS_A), (kaug, 0)],
                                 (heads, 2 * N_HEADS_A), causal_b3,
                                 n_heads=N_HEADS_A, name="flash_fox")
            qaug_b = moba_gate(heads, rel_bias, n_heads=N_HEADS_B, q_off=hb,
                               k_off=hb + N_HEADS_B)
            yb = flash_attention([(heads, hb), (qaug_b, 0)],
                                 [(heads, hb + N_HEADS_B), (moba_kaug, None)],
                                 (heads, hb + 2 * N_HEADS_B), moba_b3,
                                 n_heads=N_HEADS_B, name="flash_moba")
            y = jnp.concatenate([ya, yb], axis=1)
            w_out = ab_w_out[j].astype(BF16)
        else:
            w_in = _pad_cols(mla_w_in[j], MLA_Q_RANK + MLA_KV_RANK + LANES).astype(BF16)
            cq, ckv, kr = mla_in(xb, w_in, mla_q_norm[j][None, :], mla_kv_norm[j][None, :],
                                 cos_t, sin_a, sin_b)
            w_uq = mla_w_uq[j].reshape(MLA_Q_RANK, MLA_HEADS, MLA_NOPE + MLA_ROPE)
            w_uq = jnp.pad(w_uq, ((0, 0), (0, 0), (0, 2 * LANES - MLA_NOPE - MLA_ROPE)))
            w_uq = w_uq.reshape(MLA_Q_RANK, MLA_HEADS * 2 * LANES).astype(BF16)
            q_full = mla_q(cq, w_uq, cos_t, sin_a, sin_b,
                           qscale=(MLA_NOPE + MLA_ROPE) ** -0.5 * LOG2E)
            k_full, v = mla_kv(ckv, mla_w_ukv[j].astype(BF16), kr)
            y = flash_attention([(q_full, 0)], [(k_full, 0)], (v, 0), causal_b3,
                                n_heads=MLA_HEADS, name="flash_mla")
            w_out = mla_w_out[j].astype(BF16)
        xf, xb = mm_res_ln(y, w_out, xf, g0, b0, tm=512, tk=1024)
        xf, xb = _ffn_block(xf, xb, ffn_w_gate[layer], ffn_w_up[layer], ffn_w_down[layer],
                            g1, b1)
    return xf.reshape(1, S, D_MODEL)
```

```python
import functools
import math

import numpy as np
import jax
import jax.numpy as jnp
from jax import lax
from jax.experimental import pallas as pl
from jax.experimental.pallas import tpu as pltpu

F32 = jnp.float32
BF16 = jnp.bfloat16

D_MODEL = 2048
DEPTH = 4
HEAD_DIM = 128
N_HEADS_A = 8
N_HEADS_B = 8
MOBA_BLOCK = 256
MOBA_TOPK = 3
N_BUCKETS = 32
MAX_DISTANCE = 128
MLA_HEADS = 16
MLA_Q_RANK = 512
MLA_KV_RANK = 512
MLA_NOPE = 128
MLA_ROPE = 64
MLA_V = 128
ROPE_THETA = 10000.0
DEEPNORM_ALPHA = (2 * DEPTH) ** 0.25
DA = N_HEADS_A * HEAD_DIM
DB = N_HEADS_B * HEAD_DIM
LN_EPS = 1e-5
RMS_EPS = 1e-6
LOG2E = math.log2(math.e)

LANES = 128
VMEM_BUDGET_BYTES = 56 * 2**20
MASK_VALUE = -1e30
ATTN_TILE = 2 * MOBA_BLOCK


def _vmem_limit(block_bytes, scratch_bytes=0, temp_bytes=0):
    est = 2 * block_bytes + scratch_bytes + temp_bytes + (4 << 20)
    return int(min(max(est, 16 << 20), VMEM_BUDGET_BYTES))


def _params(vmem_bytes, ngrid, flags=None):
    return pltpu.CompilerParams(
        dimension_semantics=("arbitrary",) * ngrid, vmem_limit_bytes=vmem_bytes, flags=flags)


def _split3(x):
    hi = x.astype(BF16).astype(F32)
    r1 = x - hi
    lo = r1.astype(BF16).astype(F32)
    return hi, lo, r1 - lo


def _mm_heads_kernel(x_ref, w_ref, cs_ref, o_ref, *, width):
    acc = jnp.dot(x_ref[...], w_ref[...], preferred_element_type=F32) * cs_ref[...]
    for hh in range(o_ref.shape[0]):
        o_ref[hh] = acc[:, hh * width:(hh + 1) * width].astype(o_ref.dtype)


def mm_heads(x, w, colscale, *, tm, tn, width=HEAD_DIM):
    M, K = x.shape
    N = w.shape[1]
    nh = tn // width
    blk = tm * K * 2 + K * tn * 2 + tm * tn * 2 + tn * 4
    return pl.pallas_call(
        functools.partial(_mm_heads_kernel, width=width),
        grid=(M // tm, N // tn),
        in_specs=[pl.BlockSpec((tm, K), lambda i, j: (i, 0)),
                  pl.BlockSpec((K, tn), lambda i, j: (0, j)),
                  pl.BlockSpec((1, tn), lambda i, j: (0, j))],
        out_specs=pl.BlockSpec((nh, tm, width), lambda i, j: (j, i, 0)),
        out_shape=jax.ShapeDtypeStruct((N // width, M, width), BF16),
        compiler_params=_params(_vmem_limit(blk, temp_bytes=2 * tm * tn * 4), 2),
        name="mm_heads",
    )(x, w, colscale)


def _fox_gate_kernel(x_ref, wf_ref, bf_ref, qaug_ref, kaug_ref, carry_ref, *, tb, nh):
    @pl.when(pl.program_id(0) == 0)
    def _():
        carry_ref[...] = jnp.zeros_like(carry_ref)

    z = jnp.dot(x_ref[...], wf_ref[...], preferred_element_type=F32) + bf_ref[...]
    lf = jnp.minimum(z, 0.0) - jnp.log1p(jnp.exp(-jnp.abs(z)))
    row = lax.broadcasted_iota(jnp.int32, (tb, tb), 0)
    col = lax.broadcasted_iota(jnp.int32, (tb, tb), 1)
    tri = jnp.where(row >= col, 1.0, 0.0).astype(F32)
    cs = jnp.dot(tri, lf, preferred_element_type=F32,
                 precision=lax.Precision.HIGHEST) + carry_ref[...]
    carry_ref[...] = cs[tb - 1:tb, :]
    c2 = cs * LOG2E
    lane = lax.broadcasted_iota(jnp.int32, (tb, LANES), 1)
    for h in range(nh):
        hi, lo, lo2 = _split3(jnp.broadcast_to(c2[:, h:h + 1], (tb, LANES)))
        qa = jnp.where(lane == 0, hi, jnp.where(lane == 1, lo, jnp.where(
            lane == 2, lo2, jnp.where(lane < 6, 1.0, 0.0))))
        ka = jnp.where(lane < 3, 1.0, jnp.where(lane == 3, -hi, jnp.where(
            lane == 4, -lo, jnp.where(lane == 5, -lo2, 0.0))))
        qaug_ref[h] = qa.astype(BF16)
        kaug_ref[h] = ka.astype(BF16)


def fox_gate(xb, wf_pad, bf_pad, *, tb=512, nh=N_HEADS_A):
    S, K = xb.shape
    blk = tb * K * 2 + K * LANES * 2 + 2 * nh * tb * LANES * 2
    aug = jax.ShapeDtypeStruct((nh, S, LANES), BF16)
    return pl.pallas_call(
        functools.partial(_fox_gate_kernel, tb=tb, nh=nh),
        grid=(S // tb,),
        in_specs=[pl.BlockSpec((tb, K), lambda i: (i, 0)),
                  pl.BlockSpec((K, LANES), lambda i: (0, 0)),
                  pl.BlockSpec((1, LANES), lambda i: (0, 0))],
        out_specs=[pl.BlockSpec((nh, tb, LANES), lambda i: (0, i, 0)),
                   pl.BlockSpec((nh, tb, LANES), lambda i: (0, i, 0))],
        out_shape=[aug, aug],
        scratch_shapes=[pltpu.VMEM((1, LANES), F32)],
        compiler_params=_params(_vmem_limit(blk, temp_bytes=4 * tb * tb * 4), 1),
        name="fox_gate",
    )(xb, wf_pad, bf_pad)


def _moba_gate_kernel(rb_ref, q_ref, k_ref, qaug_ref, kmean_ref, *, nb, tr):
    B = MOBA_BLOCK
    h = pl.program_id(0)
    i = pl.program_id(1)
    far_bias = rb_ref[N_BUCKETS - 1, h] * LOG2E

    @pl.when(i == 0)
    def _():
        kf = k_ref[0].astype(F32).reshape(nb, B, HEAD_DIM)
        kmean_ref[...] = jnp.zeros_like(kmean_ref)
        kmean_ref[0:nb, :] = jnp.sum(kf, axis=1) * (1.0 / B)

    lane = lax.broadcasted_iota(jnp.int32, (tr, LANES), 1)
    lane_f = lane.astype(F32)
    own = i * (tr // B) + lax.broadcasted_iota(jnp.int32, (tr, LANES), 0) // B
    g = lax.dot_general(q_ref[0].astype(F32), kmean_ref[...], (((1,), (1,)), ((), ())),
                        preferred_element_type=F32, precision=lax.Precision.HIGHEST)
    neg_inf = -jnp.inf
    g = jnp.where(lane < own, g, neg_inf)
    mb = jnp.where(lane == own, 0.0, jnp.where(lane < nb, MASK_VALUE, 0.0)).astype(F32)
    for _ in range(MOBA_TOPK):
        mx = jnp.max(g, axis=1, keepdims=True)
        hit = jnp.logical_and(g == mx, mx > neg_inf)
        idx = jnp.min(jnp.where(hit, lane_f, float(LANES)), axis=1, keepdims=True)
        pick = lane_f == idx
        mb = jnp.where(pick, far_bias, mb)
        g = jnp.where(pick, neg_inf, g)
    mb_hi = mb.astype(BF16).astype(F32)
    mb_lo = jnp.where(mb > 0.5 * MASK_VALUE, mb - mb_hi, 0.0)
    qaug_ref[0] = (mb_hi + pltpu.roll(mb_lo, nb, 1)).astype(BF16)


def moba_gate(heads, rel_bias, *, n_heads, q_off, k_off, tr=1024):
    S = heads.shape[1]
    nb = S // MOBA_BLOCK
    assert 2 * nb <= LANES
    blk = tr * HEAD_DIM * 2 + S * HEAD_DIM * 2 + tr * LANES * 2
    return pl.pallas_call(
        functools.partial(_moba_gate_kernel, nb=nb, tr=tr),
        grid=(n_heads, S // tr),
        in_specs=[pl.BlockSpec(memory_space=pltpu.SMEM),
                  pl.BlockSpec((1, tr, HEAD_DIM), lambda h, i: (h + q_off, i, 0)),
                  pl.BlockSpec((1, S, HEAD_DIM), lambda h, i: (h + k_off, 0, 0))],
        out_specs=pl.BlockSpec((1, tr, LANES), lambda h, i: (h, i, 0)),
        out_shape=jax.ShapeDtypeStruct((n_heads, S, LANES), BF16),
        scratch_shapes=[pltpu.VMEM((LANES, HEAD_DIM), F32)],
        compiler_params=_params(
            _vmem_limit(blk, LANES * HEAD_DIM * 4,
                        temp_bytes=S * HEAD_DIM * 4 + 12 * tr * LANES * 4), 2),
        name="moba_gate",
    )(rel_bias, heads, heads)


_KIND_ZERO, _KIND_MASK, _KIND_BIAS, _KIND_BIAS_MINUS_FAR = 0, 1, 2, 3


def _moba_bias_kernel(rb_ref, bucket_ref, kind_ref, o_ref):
    h = pl.program_id(0)
    far = rb_ref[N_BUCKETS - 1, h]
    bucket = bucket_ref[...]
    kind = kind_ref[...]
    val = jnp.zeros(bucket.shape, F32)
    for b in range(N_BUCKETS):
        val = jnp.where(bucket == b, rb_ref[b, h], val)
    val = jnp.where(kind == _KIND_BIAS_MINUS_FAR, val - far, val) * LOG2E
    o_ref[0] = jnp.where(kind == _KIND_ZERO, 0.0, jnp.where(kind == _KIND_MASK, MASK_VALUE, val))


def _t5_bucket_table(n):
    rel = np.arange(n)
    max_exact = N_BUCKETS // 2
    nf = np.maximum(rel, 1).astype(np.float32)
    large = max_exact + (np.log(nf / np.float32(max_exact))
                         / np.float32(math.log(MAX_DISTANCE / max_exact))
                         * np.float32(N_BUCKETS - max_exact)).astype(np.int32)
    large = np.minimum(large, N_BUCKETS - 1)
    return np.where(rel < max_exact, rel, large).astype(np.int32)


def _moba_bias_codes(t):
    B = MOBA_BLOCK
    table = _t5_bucket_table(t + 2 * B)
    rows = np.arange(t)[:, None]
    cols = np.arange(3 * t)[None, :] - t
    rel = rows - cols
    qblk = rows // B
    kblk = np.floor_divide(cols, B)
    bucket = table[np.clip(rel, 0, table.size - 1)] + np.zeros((t, 3 * t), np.int32)
    kind = np.full((t, 3 * t), _KIND_ZERO, np.int32)
    kind = np.where(kblk == qblk - 1, _KIND_BIAS_MINUS_FAR, kind)
    kind = np.where(kblk == qblk, np.where(rel >= 0, _KIND_BIAS, _KIND_MASK), kind)
    kind = np.where(cols >= t, _KIND_MASK, kind)
    return bucket.astype(np.int32), kind.astype(np.int32)


def moba_bias(rel_bias, *, n_heads, t):
    bucket, kind = _moba_bias_codes(t)
    blk = 3 * t * 3 * t * 4
    return pl.pallas_call(
        _moba_bias_kernel,
        grid=(n_heads,),
        in_specs=[pl.BlockSpec(memory_space=pltpu.SMEM),
                  pl.BlockSpec((t, 3 * t), lambda h: (0, 0)),
                  pl.BlockSpec((t, 3 * t), lambda h: (0, 0))],
        out_specs=pl.BlockSpec((1, t, 3 * t), lambda h: (h, 0, 0)),
        out_shape=jax.ShapeDtypeStruct((n_heads, t, 3 * t), F32),
        compiler_params=_params(_vmem_limit(blk, temp_bytes=4 * t * 3 * t * 4), 1),
        name="moba_bias",
    )(rel_bias, jnp.asarray(bucket), jnp.asarray(kind))


def _causal_bias(t):
    rows = np.arange(t)[:, None]
    cols = np.arange(3 * t)[None, :] - t
    return jnp.asarray(np.where(cols <= rows, 0.0, MASK_VALUE).astype(np.float32)[None])


def _ones_column(rows):
    lane = lax.broadcasted_iota(jnp.int32, (rows, LANES), 1)
    return jnp.where(lane == 0, 1.0, 0.0).astype(BF16)


def _qk(q, k):
    return lax.dot_general(q, k, (((1,), (1,)), ((), ())), preferred_element_type=F32)


def _flash_kernel(*refs, t, nparts, nsplit, group):
    q_refs = refs[:nparts]
    k_refs = refs[nparts:2 * nparts]
    v_ref, b3_ref, o_ref, m_ref, acc_ref = refs[2 * nparts:]
    qi = pl.program_id(1)
    dv = v_ref.shape[-1]
    r = t // nsplit
    chains = [(g, c) for g in range(group) for c in range(nsplit)]

    def head(ref, g):
        return g if ref.shape[0] == group else 0

    qs = [jnp.concatenate([ref[head(ref, g), c * r:(c + 1) * r, :] for ref in q_refs], axis=1)
          for g, c in chains]
    m_ref[...] = jnp.full_like(m_ref, MASK_VALUE)
    acc_ref[...] = jnp.zeros_like(acc_ref)

    def step(tile, bias_off=None):
        ks = pl.multiple_of(tile * t, t)
        ones = _ones_column(t)
        ss = []
        for n, (g, c) in enumerate(chains):
            k = jnp.concatenate([ref[head(ref, g), pl.ds(ks, t), :] for ref in k_refs], axis=1)
            s = _qk(qs[n], k)
            if bias_off is not None:
                s = s + b3_ref[head(b3_ref, g), c * r:(c + 1) * r, pl.ds(bias_off, t)]
            ss.append(s)
        for n, (g, c) in enumerate(chains):
            v1 = jnp.concatenate([v_ref[g, pl.ds(ks, t), :], ones], axis=1)
            m = m_ref[n]
            m_new = jnp.maximum(m, jnp.max(ss[n], axis=1, keepdims=True))
            p = jnp.exp2(ss[n] - jnp.concatenate([m_new] * (t // LANES), axis=1))
            a = jnp.exp2(m - m_new)
            acc_ref[n] = (jnp.concatenate([a] * ((dv + LANES) // LANES), axis=1) * acc_ref[n]
                          + jnp.dot(p.astype(BF16), v1, preferred_element_type=F32))
            m_ref[n] = m_new

    n_plain = jnp.maximum(qi - 1, 0)

    def body(kb, carry):
        step(kb)
        return carry

    lax.fori_loop(0, n_plain, body, 0)
    step(n_plain, bias_off=pl.multiple_of(jnp.where(qi == 0, 2 * t, 0), t))
    step(qi, bias_off=t)
    for g in range(group):
        out = []
        for c in range(nsplit):
            acc = acc_ref[g * nsplit + c]
            out.append(acc[:, :dv] * (1.0 / acc[:, dv:dv + 1]))
        o_ref[:, g * dv:(g + 1) * dv] = jnp.concatenate(out, axis=0).astype(o_ref.dtype)


def flash_attention(q_parts, k_parts, v_part, b3, *, n_heads, name, t=ATTN_TILE, nsplit=2,
                    group=2):
    v, v_off = v_part
    S, dv = v.shape[1], v.shape[2]

    def spec(arr, off, rows):
        g = group if off is not None and arr.shape[0] > 1 else 1
        assert off is None or off % group == 0
        blk_shape = (g, t if rows else arr.shape[1], arr.shape[2])
        if g == 1:
            index = lambda h, i: (0, i if rows else 0, 0)
        else:
            index = lambda h, i: (h + off // group, i if rows else 0, 0)
        return pl.BlockSpec(blk_shape, index), math.prod(blk_shape) * arr.dtype.itemsize

    operands = ([(a, o, True) for a, o in q_parts] + [(a, o, False) for a, o in k_parts]
                + [(v, v_off, False), (b3, 0 if b3.shape[0] > 1 else None, False)])
    in_specs, blk = [], t * group * dv * 2
    for arr, off, rows in operands:
        s, nbytes = spec(arr, off, rows)
        in_specs.append(s)
        blk += nbytes
    nchain = group * nsplit
    r = t // nsplit
    return pl.pallas_call(
        functools.partial(_flash_kernel, t=t, nparts=len(q_parts), nsplit=nsplit, group=group),
        grid=(n_heads // group, S // t),
        in_specs=in_specs,
        out_specs=pl.BlockSpec((t, group * dv), lambda h, i: (i, h)),
        out_shape=jax.ShapeDtypeStruct((S, n_heads * dv), BF16),
        scratch_shapes=[pltpu.VMEM((nchain, r, LANES), F32),
                        pltpu.VMEM((nchain, r, dv + LANES), F32)],
        compiler_params=_params(
            _vmem_limit(blk, nchain * r * (dv + 2 * LANES) * 4,
                        temp_bytes=6 * group * t * t * 4), 2),
        name=name,
    )(*[arr for arr, _, _ in operands])


def _moba_key_onehot(S):
    nb = S // MOBA_BLOCK
    lanes = np.arange(LANES)[None, :]
    blk_of = (np.arange(S) // MOBA_BLOCK)[:, None]
    onehot = (lanes % nb == blk_of) & (lanes < 2 * nb)
    return jnp.asarray(onehot.astype(np.float32)[None], dtype=BF16)


def _mm_res_ln_kernel(*refs, n_a, nk):
    a_refs = refs[:n_a]
    w_ref, x_ref, g_ref, b_ref, o_ref, obf_ref = refs[n_a:n_a + 6]
    scratch = refs[n_a + 6:]
    part, row = None, 0
    for a_ref in a_refs:
        ka = a_ref.shape[1]
        d = jnp.dot(a_ref[...], w_ref[row:row + ka, :], preferred_element_type=F32)
        part = d if part is None else part + d
        row += ka

    def finish(y):
        z = DEEPNORM_ALPHA * x_ref[...] + y
        mu = jnp.mean(z, axis=-1, keepdims=True)
        zc = z - mu
        var = jnp.mean(zc * zc, axis=-1, keepdims=True)
        out = zc * lax.rsqrt(var + LN_EPS) * g_ref[...] + b_ref[...]
        o_ref[...] = out
        obf_ref[...] = out.astype(BF16)

    if nk == 1:
        finish(part)
        return
    acc_ref, = scratch
    k = pl.program_id(1)

    @pl.when(k == 0)
    def _():
        acc_ref[...] = part

    @pl.when(k > 0)
    def _():
        acc_ref[...] += part

    @pl.when(k == nk - 1)
    def _():
        finish(acc_ref[...])


def mm_res_ln(a_list, w_stack, layer, x, g, b, *, tm, tk=None):
    M = a_list[0].shape[0]
    K = sum(a.shape[1] for a in a_list)
    N = w_stack.shape[2]
    tk = K if tk is None else tk
    nk = K // tk
    assert len(a_list) == 1 or nk == 1
    blk = tm * tk * 2 + tk * N * 2 + tm * N * 4 + 2 * N * 4 + tm * N * 4 + tm * N * 2
    scratch = [pltpu.VMEM((tm, N), F32)] if nk > 1 else []
    a_specs = [pl.BlockSpec((tm, tk if nk > 1 else a.shape[1]), lambda i, k: (i, k))
               for a in a_list]
    return pl.pallas_call(
        functools.partial(_mm_res_ln_kernel, n_a=len(a_list), nk=nk),
        grid=(M // tm, nk),
        in_specs=a_specs + [
            pl.BlockSpec((None, tk, N), lambda i, k: (layer, k, 0)),
            pl.BlockSpec((tm, N), lambda i, k: (i, 0)),
            pl.BlockSpec((1, N), lambda i, k: (0, 0)),
            pl.BlockSpec((1, N), lambda i, k: (0, 0))],
        out_specs=[pl.BlockSpec((tm, N), lambda i, k: (i, 0)),
                   pl.BlockSpec((tm, N), lambda i, k: (i, 0))],
        out_shape=[jax.ShapeDtypeStruct((M, N), F32),
                   jax.ShapeDtypeStruct((M, N), BF16)],
        scratch_shapes=scratch,
        compiler_params=_params(
            _vmem_limit(blk, tm * N * 4 if nk > 1 else 0, temp_bytes=3 * tm * N * 4), 2),
        name="mm_res_ln",
    )(*a_list, w_stack, x, g, b)


def _ffn_gu_kernel(x_ref, wg_ref, wu_ref, o_ref, wgb_ref, wub_ref):
    @pl.when(pl.program_id(1) == 0)
    def _():
        wgb_ref[...] = wg_ref[...].astype(BF16)
        wub_ref[...] = wu_ref[...].astype(BF16)

    x = x_ref[...]
    g = jnp.dot(x, wgb_ref[...], preferred_element_type=F32)
    u = jnp.dot(x, wub_ref[...], preferred_element_type=F32)
    o_ref[...] = (g * jax.nn.sigmoid(g) * u).astype(o_ref.dtype)


def ffn_gate_up(xb, wg_stack, wu_stack, layer, *, tm, tf):
    M, K = xb.shape
    F = wg_stack.shape[2]
    blk = tm * K * 2 + 2 * K * tf * 4 + tm * tf * 2
    w_spec = pl.BlockSpec((None, K, tf), lambda j, i: (layer, 0, j))
    return pl.pallas_call(
        _ffn_gu_kernel,
        grid=(F // tf, M // tm),
        in_specs=[pl.BlockSpec((tm, K), lambda j, i: (i, 0)), w_spec, w_spec],
        out_specs=pl.BlockSpec((tm, tf), lambda j, i: (i, j)),
        out_shape=jax.ShapeDtypeStruct((M, F), BF16),
        scratch_shapes=[pltpu.VMEM((K, tf), BF16), pltpu.VMEM((K, tf), BF16)],
        compiler_params=_params(
            _vmem_limit(blk, 2 * K * tf * 2, temp_bytes=4 * tm * tf * 4), 2),
        name="ffn_gate_up",
    )(xb, wg_stack, wu_stack)


def _rope_lanes(r, cos_t, sin_a, sin_b):
    return (r * cos_t + pltpu.roll(r, LANES - MLA_ROPE // 2, 1) * sin_a
            + pltpu.roll(r, MLA_ROPE // 2, 1) * sin_b)


def _rms(x, g):
    return x * lax.rsqrt(jnp.mean(x * x, axis=-1, keepdims=True) + RMS_EPS) * g


def _mla_in_kernel(x_ref, w_ref, gq_ref, gkv_ref, cos_ref, sa_ref, sb_ref,
                   cq_ref, ckv_ref, kr_ref):
    h = jnp.dot(x_ref[...], w_ref[...], preferred_element_type=F32)
    cq_ref[...] = _rms(h[:, :MLA_Q_RANK], gq_ref[...]).astype(BF16)
    ckv_ref[...] = _rms(h[:, MLA_Q_RANK:MLA_Q_RANK + MLA_KV_RANK], gkv_ref[...]).astype(BF16)
    r = h[:, MLA_Q_RANK + MLA_KV_RANK:]
    kr_ref[...] = _rope_lanes(r, cos_ref[...], sa_ref[...], sb_ref[...]).astype(BF16)


def mla_in(xb, w_pad, gq, gkv, cos_t, sin_a, sin_b, *, tm=512):
    S, K = xb.shape
    N = w_pad.shape[1]
    blk = (tm * K * 2 + K * N * 2 + 3 * tm * LANES * 4
           + tm * (MLA_Q_RANK + MLA_KV_RANK + LANES) * 2)
    row = lambda i: (i, 0)
    fixed = lambda i: (0, 0)
    return pl.pallas_call(
        _mla_in_kernel,
        grid=(S // tm,),
        in_specs=[pl.BlockSpec((tm, K), row), pl.BlockSpec((K, N), fixed),
                  pl.BlockSpec((1, MLA_Q_RANK), fixed), pl.BlockSpec((1, MLA_KV_RANK), fixed),
                  pl.BlockSpec((tm, LANES), row), pl.BlockSpec((tm, LANES), row),
                  pl.BlockSpec((tm, LANES), row)],
        out_specs=[pl.BlockSpec((tm, MLA_Q_RANK), row), pl.BlockSpec((tm, MLA_KV_RANK), row),
                   pl.BlockSpec((tm, LANES), row)],
        out_shape=[jax.ShapeDtypeStruct((S, MLA_Q_RANK), BF16),
                   jax.ShapeDtypeStruct((S, MLA_KV_RANK), BF16),
                   jax.ShapeDtypeStruct((S, LANES), BF16)],
        compiler_params=_params(_vmem_limit(blk, temp_bytes=3 * tm * N * 4), 1),
        name="mla_in",
    )(xb, w_pad, gq, gkv, cos_t, sin_a, sin_b)


def _mla_q_kernel(cq_ref, w_ref, cos_ref, sa_ref, sb_ref, q_ref, *, qscale):
    acc = jnp.dot(cq_ref[...], w_ref[...], preferred_element_type=F32) * qscale
    cos_t, sin_a, sin_b = cos_ref[...], sa_ref[...], sb_ref[...]
    for hh in range(q_ref.shape[0]):
        base = hh * 2 * LANES
        q_ref[hh, :, 0:LANES] = acc[:, base:base + LANES].astype(BF16)
        r = acc[:, base + LANES:base + 2 * LANES]
        q_ref[hh, :, LANES:2 * LANES] = _rope_lanes(r, cos_t, sin_a, sin_b).astype(BF16)


def mla_q(cq, w_pad, cos_t, sin_a, sin_b, *, qscale, tm=512):
    S, K = cq.shape
    N = w_pad.shape[1]
    nh = N // (2 * LANES)
    blk = tm * K * 2 + K * N * 2 + 3 * tm * LANES * 4 + tm * N * 2
    row = lambda i: (i, 0)
    return pl.pallas_call(
        functools.partial(_mla_q_kernel, qscale=qscale),
        grid=(S // tm,),
        in_specs=[pl.BlockSpec((tm, K), row), pl.BlockSpec((K, N), lambda i: (0, 0)),
                  pl.BlockSpec((tm, LANES), row), pl.BlockSpec((tm, LANES), row),
                  pl.BlockSpec((tm, LANES), row)],
        out_specs=pl.BlockSpec((nh, tm, 2 * LANES), lambda i: (0, i, 0)),
        out_shape=jax.ShapeDtypeStruct((nh, S, 2 * LANES), BF16),
        compiler_params=_params(_vmem_limit(blk, temp_bytes=2 * tm * N * 4), 1),
        name="mla_q",
    )(cq, w_pad, cos_t, sin_a, sin_b)


def _mla_kv_kernel(ckv_ref, w_ref, kr_ref, k_ref, v_ref):
    acc = jnp.dot(ckv_ref[...], w_ref[...], preferred_element_type=F32)
    kr = kr_ref[...]
    for hh in range(k_ref.shape[0]):
        base = hh * (MLA_NOPE + MLA_V)
        k_ref[hh, :, 0:LANES] = acc[:, base:base + MLA_NOPE].astype(BF16)
        k_ref[hh, :, LANES:2 * LANES] = kr
        v_ref[hh] = acc[:, base + MLA_NOPE:base + MLA_NOPE + MLA_V].astype(BF16)


def mla_kv(ckv, w, kr, *, tm=512):
    S, K = ckv.shape
    N = w.shape[1]
    nh = N // (MLA_NOPE + MLA_V)
    blk = tm * K * 2 + K * N * 2 + tm * LANES * 2 + nh * tm * (2 * LANES + MLA_V) * 2
    row = lambda i: (i, 0)
    return pl.pallas_call(
        _mla_kv_kernel,
        grid=(S // tm,),
        in_specs=[pl.BlockSpec((tm, K), row), pl.BlockSpec((K, N), lambda i: (0, 0)),
                  pl.BlockSpec((tm, LANES), row)],
        out_specs=[pl.BlockSpec((nh, tm, 2 * LANES), lambda i: (0, i, 0)),
                   pl.BlockSpec((nh, tm, MLA_V), lambda i: (0, i, 0))],
        out_shape=[jax.ShapeDtypeStruct((nh, S, 2 * LANES), BF16),
                   jax.ShapeDtypeStruct((nh, S, MLA_V), BF16)],
        compiler_params=_params(_vmem_limit(blk, temp_bytes=2 * tm * N * 4), 1),
        name="mla_kv",
    )(ckv, w, kr)


def _rope_lane_tables(S):
    half = MLA_ROPE // 2
    inv = ROPE_THETA ** (-jnp.arange(0, MLA_ROPE, 2, dtype=F32) / MLA_ROPE)
    ang = jnp.arange(S, dtype=F32)[:, None] * inv[None, :]
    cos, sin = jnp.cos(ang), jnp.sin(ang)
    z = jnp.zeros((S, half), F32)
    z2 = jnp.zeros((S, LANES - MLA_ROPE), F32)
    cos_t = jnp.concatenate([cos, cos, z2], axis=1)
    sin_a = jnp.concatenate([-sin, z, z2], axis=1)
    sin_b = jnp.concatenate([z, sin, z2], axis=1)
    return cos_t, sin_a, sin_b


def _pad_cols(w, n):
    return jnp.pad(w, ((0, 0), (0, n - w.shape[1])))


def kernel(x, ab_w_in, ab_forget_bias, ab_w_out, rel_bias, mla_w_in, mla_q_norm,
           mla_kv_norm, mla_w_uq, mla_w_ukv, mla_w_out, ffn_w_gate, ffn_w_up,
           ffn_w_down, ln_g, ln_b):
    S = x.shape[1]
    xf = x.reshape(S, D_MODEL)
    xb = xf.astype(BF16)
    cos_t, sin_a, sin_b = _rope_lane_tables(S)
    nf = 3 * DA + N_HEADS_A
    qs = HEAD_DIM ** -0.5 * LOG2E
    colscale = np.ones((1, 3 * DA + 3 * DB), np.float32)
    colscale[:, :DA] = qs
    colscale[:, 3 * DA:3 * DA + DB] = qs
    colscale = jnp.asarray(colscale)
    causal_b3 = _causal_bias(ATTN_TILE)
    moba_b3 = moba_bias(rel_bias, n_heads=N_HEADS_B, t=ATTN_TILE)
    moba_kaug = _moba_key_onehot(S)
    hb = 3 * N_HEADS_A
    ab_w_out_b = ab_w_out.astype(BF16)
    mla_w_out_b = mla_w_out.astype(BF16)
    ffn_w_down_b = ffn_w_down.astype(BF16)

    for layer in range(DEPTH):
        j = layer // 2
        g0, b0 = ln_g[layer, 0][None, :], ln_b[layer, 0][None, :]
        g1, b1 = ln_g[layer, 1][None, :], ln_b[layer, 1][None, :]
        if layer % 2 == 0:
            w_in = ab_w_in[j]
            w_qkv = jnp.concatenate([w_in[:, :3 * DA], w_in[:, nf:]], axis=1).astype(BF16)
            w_f = _pad_cols(w_in[:, 3 * DA:nf], LANES).astype(BF16)
            b_f = _pad_cols(ab_forget_bias[j][None, :], LANES)
            heads = mm_heads(xb, w_qkv, colscale, tm=1024, tn=512)
            qaug, kaug = fox_gate(xb, w_f, b_f)
            ya = flash_attention([(heads, 0), (qaug, 0)], [(heads, N_HEADS_A), (kaug, 0)],
                                 (heads, 2 * N_HEADS_A), causal_b3,
                                 n_heads=N_HEADS_A, name="flash_fox")
            qaug_b = moba_gate(heads, rel_bias, n_heads=N_HEADS_B, q_off=hb,
                               k_off=hb + N_HEADS_B)
            yb = flash_attention([(heads, hb), (qaug_b, 0)],
                                 [(heads, hb + N_HEADS_B), (moba_kaug, None)],
                                 (heads, hb + 2 * N_HEADS_B), moba_b3,
                                 n_heads=N_HEADS_B, name="flash_moba")
            y, w_out = [ya, yb], ab_w_out_b
        else:
            w_in = _pad_cols(mla_w_in[j], MLA_Q_RANK + MLA_KV_RANK + LANES).astype(BF16)
            cq, ckv, kr = mla_in(xb, w_in, mla_q_norm[j][None, :], mla_kv_norm[j][None, :],
                                 cos_t, sin_a, sin_b)
            w_uq = mla_w_uq[j].reshape(MLA_Q_RANK, MLA_HEADS, MLA_NOPE + MLA_ROPE)
            w_uq = jnp.pad(w_uq, ((0, 0), (0, 0), (0, 2 * LANES - MLA_NOPE - MLA_ROPE)))
            w_uq = w_uq.reshape(MLA_Q_RANK, MLA_HEADS * 2 * LANES).astype(BF16)
            q_full = mla_q(cq, w_uq, cos_t, sin_a, sin_b,
                           qscale=(MLA_NOPE + MLA_ROPE) ** -0.5 * LOG2E)
            k_full, v = mla_kv(ckv, mla_w_ukv[j].astype(BF16), kr)
            y = [flash_attention([(q_full, 0)], [(k_full, 0)], (v, 0), causal_b3,
                                 n_heads=MLA_HEADS, name="flash_mla")]
            w_out = mla_w_out_b
        xf, xb = mm_res_ln(y, w_out, j, xf, g0, b0, tm=512)
        hmid = ffn_gate_up(xb, ffn_w_gate, ffn_w_up, layer, tm=1024, tf=512)
        xf, xb = mm_res_ln([hmid], ffn_w_down_b, layer, xf, g1, b1, tm=512, tk=1408)
    return xf.reshape(1, S, D_MODEL)
```

```python
import functools
import math

import numpy as np
import jax
import jax.numpy as jnp
from jax import lax
from jax.experimental import pallas as pl
from jax.experimental.pallas import tpu as pltpu

F32 = jnp.float32
BF16 = jnp.bfloat16

D_MODEL = 2048
DEPTH = 4
HEAD_DIM = 128
N_HEADS_A = 8
N_HEADS_B = 8
MOBA_BLOCK = 256
MOBA_TOPK = 3
N_BUCKETS = 32
MAX_DISTANCE = 128
MLA_HEADS = 16
MLA_Q_RANK = 512
MLA_KV_RANK = 512
MLA_NOPE = 128
MLA_ROPE = 64
MLA_V = 128
ROPE_THETA = 10000.0
DEEPNORM_ALPHA = (2 * DEPTH) ** 0.25
DA = N_HEADS_A * HEAD_DIM
DB = N_HEADS_B * HEAD_DIM
LN_EPS = 1e-5
RMS_EPS = 1e-6
LOG2E = math.log2(math.e)

LANES = 128
VMEM_BUDGET_BYTES = 56 * 2**20
MASK_VALUE = -1e30
ATTN_TILE = 2 * MOBA_BLOCK


def _vmem_limit(block_bytes, scratch_bytes=0, temp_bytes=0):
    est = 2 * block_bytes + scratch_bytes + temp_bytes + (4 << 20)
    return int(min(max(est, 16 << 20), VMEM_BUDGET_BYTES))


def _params(vmem_bytes, ngrid, flags=None):
    return pltpu.CompilerParams(
        dimension_semantics=("arbitrary",) * ngrid, vmem_limit_bytes=vmem_bytes, flags=flags)


def _split3(x):
    hi = x.astype(BF16).astype(F32)
    r1 = x - hi
    lo = r1.astype(BF16).astype(F32)
    return hi, lo, r1 - lo


def _mm_heads_kernel(x_ref, w_ref, cs_ref, o_ref, *, width):
    acc = jnp.dot(x_ref[...], w_ref[...], preferred_element_type=F32) * cs_ref[...]
    for hh in range(o_ref.shape[0]):
        o_ref[hh] = acc[:, hh * width:(hh + 1) * width].astype(o_ref.dtype)


def mm_heads(x, w, colscale, *, tm, tn, width=HEAD_DIM):
    M, K = x.shape
    N = w.shape[1]
    nh = tn // width
    blk = tm * K * 2 + K * tn * 2 + tm * tn * 2 + tn * 4
    return pl.pallas_call(
        functools.partial(_mm_heads_kernel, width=width),
        grid=(M // tm, N // tn),
        in_specs=[pl.BlockSpec((tm, K), lambda i, j: (i, 0)),
                  pl.BlockSpec((K, tn), lambda i, j: (0, j)),
                  pl.BlockSpec((1, tn), lambda i, j: (0, j))],
        out_specs=pl.BlockSpec((nh, tm, width), lambda i, j: (j, i, 0)),
        out_shape=jax.ShapeDtypeStruct((N // width, M, width), BF16),
        compiler_params=_params(_vmem_limit(blk, temp_bytes=2 * tm * tn * 4), 2),
        name="mm_heads",
    )(x, w, colscale)


def _fox_gate_kernel(x_ref, wf_ref, bf_ref, qaug_ref, kaug_ref, carry_ref, *, tb, nh):
    @pl.when(pl.program_id(0) == 0)
    def _():
        carry_ref[...] = jnp.zeros_like(carry_ref)

    z = jnp.dot(x_ref[...], wf_ref[...], preferred_element_type=F32) + bf_ref[...]
    lf = jnp.minimum(z, 0.0) - jnp.log1p(jnp.exp(-jnp.abs(z)))
    row = lax.broadcasted_iota(jnp.int32, (tb, tb), 0)
    col = lax.broadcasted_iota(jnp.int32, (tb, tb), 1)
    tri = jnp.where(row >= col, 1.0, 0.0).astype(F32)
    cs = jnp.dot(tri, lf, preferred_element_type=F32,
                 precision=lax.Precision.HIGHEST) + carry_ref[...]
    carry_ref[...] = cs[tb - 1:tb, :]
    c2 = cs * LOG2E
    lane = lax.broadcasted_iota(jnp.int32, (tb, LANES), 1)
    for h in range(nh):
        hi, lo, lo2 = _split3(jnp.broadcast_to(c2[:, h:h + 1], (tb, LANES)))
        qa = jnp.where(lane == 0, hi, jnp.where(lane == 1, lo, jnp.where(
            lane == 2, lo2, jnp.where(lane < 6, 1.0, 0.0))))
        ka = jnp.where(lane < 3, 1.0, jnp.where(lane == 3, -hi, jnp.where(
            lane == 4, -lo, jnp.where(lane == 5, -lo2, 0.0))))
        qaug_ref[h] = qa.astype(BF16)
        kaug_ref[h] = ka.astype(BF16)


def fox_gate(xb, wf_pad, bf_pad, *, tb=512, nh=N_HEADS_A):
    S, K = xb.shape
    blk = tb * K * 2 + K * LANES * 2 + 2 * nh * tb * LANES * 2
    aug = jax.ShapeDtypeStruct((nh, S, LANES), BF16)
    return pl.pallas_call(
        functools.partial(_fox_gate_kernel, tb=tb, nh=nh),
        grid=(S // tb,),
        in_specs=[pl.BlockSpec((tb, K), lambda i: (i, 0)),
                  pl.BlockSpec((K, LANES), lambda i: (0, 0)),
                  pl.BlockSpec((1, LANES), lambda i: (0, 0))],
        out_specs=[pl.BlockSpec((nh, tb, LANES), lambda i: (0, i, 0)),
                   pl.BlockSpec((nh, tb, LANES), lambda i: (0, i, 0))],
        out_shape=[aug, aug],
        scratch_shapes=[pltpu.VMEM((1, LANES), F32)],
        compiler_params=_params(_vmem_limit(blk, temp_bytes=4 * tb * tb * 4), 1),
        name="fox_gate",
    )(xb, wf_pad, bf_pad)


def _moba_gate_kernel(rb_ref, q_ref, k_ref, qaug_ref, kmean_ref, *, nb, tr):
    B = MOBA_BLOCK
    h = pl.program_id(0)
    i = pl.program_id(1)
    far_bias = rb_ref[N_BUCKETS - 1, h] * LOG2E

    @pl.when(i == 0)
    def _():
        kf = k_ref[0].astype(F32).reshape(nb, B, HEAD_DIM)
        kmean_ref[...] = jnp.zeros_like(kmean_ref)
        kmean_ref[0:nb, :] = jnp.sum(kf, axis=1) * (1.0 / B)

    lane = lax.broadcasted_iota(jnp.int32, (tr, LANES), 1)
    lane_f = lane.astype(F32)
    own = i * (tr // B) + lax.broadcasted_iota(jnp.int32, (tr, LANES), 0) // B
    g = lax.dot_general(q_ref[0].astype(F32), kmean_ref[...], (((1,), (1,)), ((), ())),
                        preferred_element_type=F32, precision=lax.Precision.HIGHEST)
    neg_inf = -jnp.inf
    g = jnp.where(lane < own, g, neg_inf)
    mb = jnp.where(lane == own, 0.0, jnp.where(lane < nb, MASK_VALUE, 0.0)).astype(F32)
    for _ in range(MOBA_TOPK):
        mx = jnp.max(g, axis=1, keepdims=True)
        hit = jnp.logical_and(g == mx, mx > neg_inf)
        idx = jnp.min(jnp.where(hit, lane_f, float(LANES)), axis=1, keepdims=True)
        pick = lane_f == idx
        mb = jnp.where(pick, far_bias, mb)
        g = jnp.where(pick, neg_inf, g)
    mb_hi = mb.astype(BF16).astype(F32)
    mb_lo = jnp.where(mb > 0.5 * MASK_VALUE, mb - mb_hi, 0.0)
    qaug_ref[0] = (mb_hi + pltpu.roll(mb_lo, nb, 1)).astype(BF16)


def moba_gate(heads, rel_bias, *, n_heads, q_off, k_off, tr=1024):
    S = heads.shape[1]
    nb = S // MOBA_BLOCK
    assert 2 * nb <= LANES
    blk = tr * HEAD_DIM * 2 + S * HEAD_DIM * 2 + tr * LANES * 2
    return pl.pallas_call(
        functools.partial(_moba_gate_kernel, nb=nb, tr=tr),
        grid=(n_heads, S // tr),
        in_specs=[pl.BlockSpec(memory_space=pltpu.SMEM),
                  pl.BlockSpec((1, tr, HEAD_DIM), lambda h, i: (h + q_off, i, 0)),
                  pl.BlockSpec((1, S, HEAD_DIM), lambda h, i: (h + k_off, 0, 0))],
        out_specs=pl.BlockSpec((1, tr, LANES), lambda h, i: (h, i, 0)),
        out_shape=jax.ShapeDtypeStruct((n_heads, S, LANES), BF16),
        scratch_shapes=[pltpu.VMEM((LANES, HEAD_DIM), F32)],
        compiler_params=_params(
            _vmem_limit(blk, LANES * HEAD_DIM * 4,
                        temp_bytes=S * HEAD_DIM * 4 + 12 * tr * LANES * 4), 2),
        name="moba_gate",
    )(rel_bias, heads, heads)


_KIND_ZERO, _KIND_MASK, _KIND_BIAS, _KIND_BIAS_MINUS_FAR = 0, 1, 2, 3


def _moba_bias_kernel(rb_ref, bucket_ref, kind_ref, o_ref):
    h = pl.program_id(0)
    far = rb_ref[N_BUCKETS - 1, h]
    bucket = bucket_ref[...]
    kind = kind_ref[...]
    val = jnp.zeros(bucket.shape, F32)
    for b in range(N_BUCKETS):
        val = jnp.where(bucket == b, rb_ref[b, h], val)
    val = jnp.where(kind == _KIND_BIAS_MINUS_FAR, val - far, val) * LOG2E
    o_ref[0] = jnp.where(kind == _KIND_ZERO, 0.0, jnp.where(kind == _KIND_MASK, MASK_VALUE, val))


def _t5_bucket_table(n):
    rel = np.arange(n)
    max_exact = N_BUCKETS // 2
    nf = np.maximum(rel, 1).astype(np.float32)
    large = max_exact + (np.log(nf / np.float32(max_exact))
                         / np.float32(math.log(MAX_DISTANCE / max_exact))
                         * np.float32(N_BUCKETS - max_exact)).astype(np.int32)
    large = np.minimum(large, N_BUCKETS - 1)
    return np.where(rel < max_exact, rel, large).astype(np.int32)


def _moba_bias_codes(t):
    B = MOBA_BLOCK
    table = _t5_bucket_table(t + 2 * B)
    rows = np.arange(t)[:, None]
    cols = np.arange(3 * t)[None, :] - t
    rel = rows - cols
    qblk = rows // B
    kblk = np.floor_divide(cols, B)
    bucket = table[np.clip(rel, 0, table.size - 1)] + np.zeros((t, 3 * t), np.int32)
    kind = np.full((t, 3 * t), _KIND_ZERO, np.int32)
    kind = np.where(kblk == qblk - 1, _KIND_BIAS_MINUS_FAR, kind)
    kind = np.where(kblk == qblk, np.where(rel >= 0, _KIND_BIAS, _KIND_MASK), kind)
    kind = np.where(cols >= t, _KIND_MASK, kind)
    return bucket.astype(np.int32), kind.astype(np.int32)


def moba_bias(rel_bias, *, n_heads, t):
    bucket, kind = _moba_bias_codes(t)
    blk = 3 * t * 3 * t * 4
    return pl.pallas_call(
        _moba_bias_kernel,
        grid=(n_heads,),
        in_specs=[pl.BlockSpec(memory_space=pltpu.SMEM),
                  pl.BlockSpec((t, 3 * t), lambda h: (0, 0)),
                  pl.BlockSpec((t, 3 * t), lambda h: (0, 0))],
        out_specs=pl.BlockSpec((1, t, 3 * t), lambda h: (h, 0, 0)),
        out_shape=jax.ShapeDtypeStruct((n_heads, t, 3 * t), F32),
        compiler_params=_params(_vmem_limit(blk, temp_bytes=4 * t * 3 * t * 4), 1),
        name="moba_bias",
    )(rel_bias, jnp.asarray(bucket), jnp.asarray(kind))


def _causal_bias(t):
    rows = np.arange(t)[:, None]
    cols = np.arange(3 * t)[None, :] - t
    return jnp.asarray(np.where(cols <= rows, 0.0, MASK_VALUE).astype(np.float32)[None])


def _ones_column(rows):
    lane = lax.broadcasted_iota(jnp.int32, (rows, LANES), 1)
    return jnp.where(lane == 0, 1.0, 0.0).astype(BF16)


def _qk(q, k):
    return lax.dot_general(q, k, (((1,), (1,)), ((), ())), preferred_element_type=F32)


def _flash_kernel(*refs, t, nparts, nsplit, group):
    q_refs = refs[:nparts]
    k_refs = refs[nparts:2 * nparts]
    v_ref, b3_ref, o_ref, m_ref, acc_ref = refs[2 * nparts:]
    qi = pl.program_id(1)
    dv = v_ref.shape[-1]
    r = t // nsplit
    chains = [(g, c) for g in range(group) for c in range(nsplit)]

    def head(ref, g):
        return g if ref.shape[0] == group else 0

    qs = [jnp.concatenate([ref[head(ref, g), c * r:(c + 1) * r, :] for ref in q_refs], axis=1)
          for g, c in chains]
    m_ref[...] = jnp.full_like(m_ref, MASK_VALUE)
    acc_ref[...] = jnp.zeros_like(acc_ref)

    def step(tile, width, bias_off=None):
        ks = pl.multiple_of(tile * t, t)
        ones = _ones_column(width)
        ss = []
        for n, (g, c) in enumerate(chains):
            k = jnp.concatenate([ref[head(ref, g), pl.ds(ks, width), :] for ref in k_refs],
                                axis=1)
            s = _qk(qs[n], k)
            if bias_off is not None:
                s = s + b3_ref[head(b3_ref, g), c * r:(c + 1) * r, pl.ds(bias_off, width)]
            ss.append(s)
        for n, (g, c) in enumerate(chains):
            v1 = jnp.concatenate([v_ref[g, pl.ds(ks, width), :], ones], axis=1)
            m = m_ref[n]
            m_new = jnp.maximum(m, jnp.max(ss[n], axis=1, keepdims=True))
            p = jnp.exp2(ss[n] - jnp.concatenate([m_new] * (width // LANES), axis=1))
            a = jnp.exp2(m - m_new)
            acc_ref[n] = (jnp.concatenate([a] * ((dv + LANES) // LANES), axis=1) * acc_ref[n]
                          + jnp.dot(p.astype(BF16), v1, preferred_element_type=F32))
            m_ref[n] = m_new

    n_plain = jnp.maximum(qi - 1, 0)

    def body(kb, carry):
        step(2 * kb, 2 * t)
        return carry

    lax.fori_loop(0, n_plain // 2, body, 0)

    @pl.when(n_plain % 2 == 1)
    def _():
        step(n_plain - 1, t)

    step(n_plain, 2 * t, bias_off=pl.multiple_of(jnp.where(qi == 0, t, 0), t))
    for g in range(group):
        out = []
        for c in range(nsplit):
            acc = acc_ref[g * nsplit + c]
            out.append(acc[:, :dv] * (1.0 / acc[:, dv:dv + 1]))
        o_ref[:, g * dv:(g + 1) * dv] = jnp.concatenate(out, axis=0).astype(o_ref.dtype)


def flash_attention(q_parts, k_parts, v_part, b3, *, n_heads, name, t=ATTN_TILE, nsplit=1,
                    group=2):
    v, v_off = v_part
    S, dv = v.shape[1], v.shape[2]

    def spec(arr, off, rows):
        g = group if off is not None and arr.shape[0] > 1 else 1
        assert off is None or off % group == 0
        blk_shape = (g, t if rows else arr.shape[1], arr.shape[2])
        if g == 1:
            index = lambda h, i: (0, i if rows else 0, 0)
        else:
            index = lambda h, i: (h + off // group, i if rows else 0, 0)
        return pl.BlockSpec(blk_shape, index), math.prod(blk_shape) * arr.dtype.itemsize

    operands = ([(a, o, True) for a, o in q_parts] + [(a, o, False) for a, o in k_parts]
                + [(v, v_off, False), (b3, 0 if b3.shape[0] > 1 else None, False)])
    in_specs, blk = [], t * group * dv * 2
    for arr, off, rows in operands:
        s, nbytes = spec(arr, off, rows)
        in_specs.append(s)
        blk += nbytes
    nchain = group * nsplit
    r = t // nsplit
    return pl.pallas_call(
        functools.partial(_flash_kernel, t=t, nparts=len(q_parts), nsplit=nsplit, group=group),
        grid=(n_heads // group, S // t),
        in_specs=in_specs,
        out_specs=pl.BlockSpec((t, group * dv), lambda h, i: (i, h)),
        out_shape=jax.ShapeDtypeStruct((S, n_heads * dv), BF16),
        scratch_shapes=[pltpu.VMEM((nchain, r, LANES), F32),
                        pltpu.VMEM((nchain, r, dv + LANES), F32)],
        compiler_params=_params(
            _vmem_limit(blk, nchain * r * (dv + 2 * LANES) * 4,
                        temp_bytes=6 * group * t * t * 4), 2),
        name=name,
    )(*[arr for arr, _, _ in operands])


def _moba_key_onehot(S):
    nb = S // MOBA_BLOCK
    lanes = np.arange(LANES)[None, :]
    blk_of = (np.arange(S) // MOBA_BLOCK)[:, None]
    onehot = (lanes % nb == blk_of) & (lanes < 2 * nb)
    return jnp.asarray(onehot.astype(np.float32)[None], dtype=BF16)


def _mm_res_ln_kernel(*refs, n_a, nk):
    a_refs = refs[:n_a]
    w_ref, x_ref, g_ref, b_ref, o_ref, obf_ref = refs[n_a:n_a + 6]
    scratch = refs[n_a + 6:]
    part, row = None, 0
    for a_ref in a_refs:
        ka = a_ref.shape[1]
        d = jnp.dot(a_ref[...], w_ref[row:row + ka, :], preferred_element_type=F32)
        part = d if part is None else part + d
        row += ka

    def finish(y):
        z = DEEPNORM_ALPHA * x_ref[...] + y
        mu = jnp.mean(z, axis=-1, keepdims=True)
        zc = z - mu
        var = jnp.mean(zc * zc, axis=-1, keepdims=True)
        out = zc * lax.rsqrt(var + LN_EPS) * g_ref[...] + b_ref[...]
        o_ref[...] = out
        obf_ref[...] = out.astype(BF16)

    if nk == 1:
        finish(part)
        return
    acc_ref, = scratch
    k = pl.program_id(1)

    @pl.when(k == 0)
    def _():
        acc_ref[...] = part

    @pl.when(k > 0)
    def _():
        acc_ref[...] += part

    @pl.when(k == nk - 1)
    def _():
        finish(acc_ref[...])


def mm_res_ln(a_list, w_stack, layer, x, g, b, *, tm, tk=None):
    M = a_list[0].shape[0]
    K = sum(a.shape[1] for a in a_list)
    N = w_stack.shape[2]
    tk = K if tk is None else tk
    nk = K // tk
    assert len(a_list) == 1 or nk == 1
    blk = tm * tk * 2 + tm * N * 4 + 2 * N * 4 + tm * N * 4 + tm * N * 2
    scratch = [pltpu.VMEM((tm, N), F32)] if nk > 1 else []
    a_specs = [pl.BlockSpec((tm, tk if nk > 1 else a.shape[1]), lambda i, k: (i, k))
               for a in a_list]
    w_bufs = 1 if nk == 1 else 2
    return pl.pallas_call(
        functools.partial(_mm_res_ln_kernel, n_a=len(a_list), nk=nk),
        grid=(M // tm, nk),
        in_specs=a_specs + [
            pl.BlockSpec((None, tk, N), lambda i, k: (layer, k, 0),
                         pipeline_mode=pl.Buffered(w_bufs)),
            pl.BlockSpec((tm, N), lambda i, k: (i, 0)),
            pl.BlockSpec((1, N), lambda i, k: (0, 0)),
            pl.BlockSpec((1, N), lambda i, k: (0, 0))],
        out_specs=[pl.BlockSpec((tm, N), lambda i, k: (i, 0)),
                   pl.BlockSpec((tm, N), lambda i, k: (i, 0))],
        out_shape=[jax.ShapeDtypeStruct((M, N), F32),
                   jax.ShapeDtypeStruct((M, N), BF16)],
        scratch_shapes=scratch,
        compiler_params=_params(
            _vmem_limit(blk, (tm * N * 4 if nk > 1 else 0) + w_bufs * tk * N * 2,
                        temp_bytes=3 * tm * N * 4), 2),
        name="mm_res_ln",
    )(*a_list, w_stack, x, g, b)


def _ffn_gu_kernel(x_ref, wg_ref, wu_ref, o_ref, wgb_ref, wub_ref):
    @pl.when(pl.program_id(1) == 0)
    def _():
        wgb_ref[...] = wg_ref[...].astype(BF16)
        wub_ref[...] = wu_ref[...].astype(BF16)

    x = x_ref[...]
    g = jnp.dot(x, wgb_ref[...], preferred_element_type=F32)
    u = jnp.dot(x, wub_ref[...], preferred_element_type=F32)
    o_ref[...] = (g * jax.nn.sigmoid(g) * u).astype(o_ref.dtype)


def ffn_gate_up(xb, wg_stack, wu_stack, layer, *, tm, tf):
    M, K = xb.shape
    F = wg_stack.shape[2]
    blk = tm * K * 2 + 2 * K * tf * 4 + tm * tf * 2
    w_spec = pl.BlockSpec((None, K, tf), lambda j, i: (layer, 0, j))
    return pl.pallas_call(
        _ffn_gu_kernel,
        grid=(F // tf, M // tm),
        in_specs=[pl.BlockSpec((tm, K), lambda j, i: (i, 0)), w_spec, w_spec],
        out_specs=pl.BlockSpec((tm, tf), lambda j, i: (i, j)),
        out_shape=jax.ShapeDtypeStruct((M, F), BF16),
        scratch_shapes=[pltpu.VMEM((K, tf), BF16), pltpu.VMEM((K, tf), BF16)],
        compiler_params=_params(
            _vmem_limit(blk, 2 * K * tf * 2, temp_bytes=4 * tm * tf * 4), 2),
        name="ffn_gate_up",
    )(xb, wg_stack, wu_stack)


def _rope_lanes(r, cos_t, sin_a, sin_b):
    return (r * cos_t + pltpu.roll(r, LANES - MLA_ROPE // 2, 1) * sin_a
            + pltpu.roll(r, MLA_ROPE // 2, 1) * sin_b)


def _rms(x, g):
    return x * lax.rsqrt(jnp.mean(x * x, axis=-1, keepdims=True) + RMS_EPS) * g


def _mla_in_kernel(x_ref, w_ref, gq_ref, gkv_ref, cos_ref, sa_ref, sb_ref,
                   cq_ref, ckv_ref, kr_ref):
    h = jnp.dot(x_ref[...], w_ref[...], preferred_element_type=F32)
    cq_ref[...] = _rms(h[:, :MLA_Q_RANK], gq_ref[...]).astype(BF16)
    ckv_ref[...] = _rms(h[:, MLA_Q_RANK:MLA_Q_RANK + MLA_KV_RANK], gkv_ref[...]).astype(BF16)
    r = h[:, MLA_Q_RANK + MLA_KV_RANK:]
    kr_ref[...] = _rope_lanes(r, cos_ref[...], sa_ref[...], sb_ref[...]).astype(BF16)


def mla_in(xb, w_pad, gq, gkv, cos_t, sin_a, sin_b, *, tm=512):
    S, K = xb.shape
    N = w_pad.shape[1]
    blk = (tm * K * 2 + K * N * 2 + 3 * tm * LANES * 4
           + tm * (MLA_Q_RANK + MLA_KV_RANK + LANES) * 2)
    row = lambda i: (i, 0)
    fixed = lambda i: (0, 0)
    return pl.pallas_call(
        _mla_in_kernel,
        grid=(S // tm,),
        in_specs=[pl.BlockSpec((tm, K), row), pl.BlockSpec((K, N), fixed),
                  pl.BlockSpec((1, MLA_Q_RANK), fixed), pl.BlockSpec((1, MLA_KV_RANK), fixed),
                  pl.BlockSpec((tm, LANES), row), pl.BlockSpec((tm, LANES), row),
                  pl.BlockSpec((tm, LANES), row)],
        out_specs=[pl.BlockSpec((tm, MLA_Q_RANK), row), pl.BlockSpec((tm, MLA_KV_RANK), row),
                   pl.BlockSpec((tm, LANES), row)],
        out_shape=[jax.ShapeDtypeStruct((S, MLA_Q_RANK), BF16),
                   jax.ShapeDtypeStruct((S, MLA_KV_RANK), BF16),
                   jax.ShapeDtypeStruct((S, LANES), BF16)],
        compiler_params=_params(_vmem_limit(blk, temp_bytes=3 * tm * N * 4), 1),
        name="mla_in",
    )(xb, w_pad, gq, gkv, cos_t, sin_a, sin_b)


def _mla_q_kernel(cq_ref, w_ref, cos_ref, sa_ref, sb_ref, q_ref, *, qscale):
    acc = jnp.dot(cq_ref[...], w_ref[...], preferred_element_type=F32) * qscale
    cos_t, sin_a, sin_b = cos_ref[...], sa_ref[...], sb_ref[...]
    for hh in range(q_ref.shape[0]):
        base = hh * 2 * LANES
        q_ref[hh, :, 0:LANES] = acc[:, base:base + LANES].astype(BF16)
        r = acc[:, base + LANES:base + 2 * LANES]
        q_ref[hh, :, LANES:2 * LANES] = _rope_lanes(r, cos_t, sin_a, sin_b).astype(BF16)


def mla_q(cq, w_pad, cos_t, sin_a, sin_b, *, qscale, tm=512):
    S, K = cq.shape
    N = w_pad.shape[1]
    nh = N // (2 * LANES)
    blk = tm * K * 2 + K * N * 2 + 3 * tm * LANES * 4 + tm * N * 2
    row = lambda i: (i, 0)
    return pl.pallas_call(
        functools.partial(_mla_q_kernel, qscale=qscale),
        grid=(S // tm,),
        in_specs=[pl.BlockSpec((tm, K), row), pl.BlockSpec((K, N), lambda i: (0, 0)),
                  pl.BlockSpec((tm, LANES), row), pl.BlockSpec((tm, LANES), row),
                  pl.BlockSpec((tm, LANES), row)],
        out_specs=pl.BlockSpec((nh, tm, 2 * LANES), lambda i: (0, i, 0)),
        out_shape=jax.ShapeDtypeStruct((nh, S, 2 * LANES), BF16),
        compiler_params=_params(_vmem_limit(blk, temp_bytes=2 * tm * N * 4), 1),
        name="mla_q",
    )(cq, w_pad, cos_t, sin_a, sin_b)


def _mla_kv_kernel(ckv_ref, w_ref, kr_ref, k_ref, v_ref):
    acc = jnp.dot(ckv_ref[...], w_ref[...], preferred_element_type=F32)
    kr = kr_ref[...]
    for hh in range(k_ref.shape[0]):
        base = hh * (MLA_NOPE + MLA_V)
        k_ref[hh, :, 0:LANES] = acc[:, base:base + MLA_NOPE].astype(BF16)
        k_ref[hh, :, LANES:2 * LANES] = kr
        v_ref[hh] = acc[:, base + MLA_NOPE:base + MLA_NOPE + MLA_V].astype(BF16)


def mla_kv(ckv, w, kr, *, tm=512):
    S, K = ckv.shape
    N = w.shape[1]
    nh = N // (MLA_NOPE + MLA_V)
    blk = tm * K * 2 + K * N * 2 + tm * LANES * 2 + nh * tm * (2 * LANES + MLA_V) * 2
    row = lambda i: (i, 0)
    return pl.pallas_call(
        _mla_kv_kernel,
        grid=(S // tm,),
        in_specs=[pl.BlockSpec((tm, K), row), pl.BlockSpec((K, N), lambda i: (0, 0)),
                  pl.BlockSpec((tm, LANES), row)],
        out_specs=[pl.BlockSpec((nh, tm, 2 * LANES), lambda i: (0, i, 0)),
                   pl.BlockSpec((nh, tm, MLA_V), lambda i: (0, i, 0))],
        out_shape=[jax.ShapeDtypeStruct((nh, S, 2 * LANES), BF16),
                   jax.ShapeDtypeStruct((nh, S, MLA_V), BF16)],
        compiler_params=_params(_vmem_limit(blk, temp_bytes=2 * tm * N * 4), 1),
        name="mla_kv",
    )(ckv, w, kr)


def _rope_lane_tables(S):
    half = MLA_ROPE // 2
    inv = ROPE_THETA ** (-np.arange(0, MLA_ROPE, 2, dtype=np.float64) / MLA_ROPE)
    ang = np.arange(S, dtype=np.float64)[:, None] * inv[None, :]
    cos, sin = np.cos(ang), np.sin(ang)
    z = np.zeros((S, half))
    z2 = np.zeros((S, LANES - MLA_ROPE))
    cos_t = np.concatenate([cos, cos, z2], axis=1)
    sin_a = np.concatenate([-sin, z, z2], axis=1)
    sin_b = np.concatenate([z, sin, z2], axis=1)
    return tuple(jnp.asarray(a.astype(np.float32)) for a in (cos_t, sin_a, sin_b))


def _pad_cols(w, n):
    return jnp.pad(w, ((0, 0), (0, n - w.shape[1])))


def kernel(x, ab_w_in, ab_forget_bias, ab_w_out, rel_bias, mla_w_in, mla_q_norm,
           mla_kv_norm, mla_w_uq, mla_w_ukv, mla_w_out, ffn_w_gate, ffn_w_up,
           ffn_w_down, ln_g, ln_b):
    S = x.shape[1]
    xf = x.reshape(S, D_MODEL)
    xb = xf.astype(BF16)
    cos_t, sin_a, sin_b = _rope_lane_tables(S)
    nf = 3 * DA + N_HEADS_A
    qs = HEAD_DIM ** -0.5 * LOG2E
    colscale = np.ones((1, 3 * DA + 3 * DB), np.float32)
    colscale[:, :DA] = qs
    colscale[:, 3 * DA:3 * DA + DB] = qs
    colscale = jnp.asarray(colscale)
    causal_b3 = _causal_bias(ATTN_TILE)
    moba_b3 = moba_bias(rel_bias, n_heads=N_HEADS_B, t=ATTN_TILE)
    moba_kaug = _moba_key_onehot(S)
    hb = 3 * N_HEADS_A
    ab_w_out_b = ab_w_out.astype(BF16)
    mla_w_out_b = mla_w_out.astype(BF16)
    ffn_w_down_b = ffn_w_down.astype(BF16)

    for layer in range(DEPTH):
        j = layer // 2
        g0, b0 = ln_g[layer, 0][None, :], ln_b[layer, 0][None, :]
        g1, b1 = ln_g[layer, 1][None, :], ln_b[layer, 1][None, :]
        if layer % 2 == 0:
            w_in = ab_w_in[j]
            w_qkv = jnp.concatenate([w_in[:, :3 * DA], w_in[:, nf:]], axis=1).astype(BF16)
            w_f = _pad_cols(w_in[:, 3 * DA:nf], LANES).astype(BF16)
            b_f = _pad_cols(ab_forget_bias[j][None, :], LANES)
            heads = mm_heads(xb, w_qkv, colscale, tm=1024, tn=512)
            qaug, kaug = fox_gate(xb, w_f, b_f)
            ya = flash_attention([(heads, 0), (qaug, 0)], [(heads, N_HEADS_A), (kaug, 0)],
                                 (heads, 2 * N_HEADS_A), causal_b3,
                                 n_heads=N_HEADS_A, name="flash_fox")
            qaug_b = moba_gate(heads, rel_bias, n_heads=N_HEADS_B, q_off=hb,
                               k_off=hb + N_HEADS_B)
            yb = flash_attention([(heads, hb), (qaug_b, 0)],
                                 [(heads, hb + N_HEADS_B), (moba_kaug, None)],
                                 (heads, hb + 2 * N_HEADS_B), moba_b3,
                                 n_heads=N_HEADS_B, name="flash_moba")
            y, w_out = [ya, yb], ab_w_out_b
        else:
            w_in = _pad_cols(mla_w_in[j], MLA_Q_RANK + MLA_KV_RANK + LANES).astype(BF16)
            cq, ckv, kr = mla_in(xb, w_in, mla_q_norm[j][None, :], mla_kv_norm[j][None, :],
                                 cos_t, sin_a, sin_b)
            w_uq = mla_w_uq[j].reshape(MLA_Q_RANK, MLA_HEADS, MLA_NOPE + MLA_ROPE)
            w_uq = jnp.pad(w_uq, ((0, 0), (0, 0), (0, 2 * LANES - MLA_NOPE - MLA_ROPE)))
            w_uq = w_uq.reshape(MLA_Q_RANK, MLA_HEADS * 2 * LANES).astype(BF16)
            q_full = mla_q(cq, w_uq, cos_t, sin_a, sin_b,
                           qscale=(MLA_NOPE + MLA_ROPE) ** -0.5 * LOG2E)
            k_full, v = mla_kv(ckv, mla_w_ukv[j].astype(BF16), kr)
            y = [flash_attention([(q_full, 0)], [(k_full, 0)], (v, 0), causal_b3,
                                 n_heads=MLA_HEADS, name="flash_mla")]
            w_out = mla_w_out_b
        xf, xb = mm_res_ln(y, w_out, j, xf, g0, b0, tm=512)
        hmid = ffn_gate_up(xb, ffn_w_gate, ffn_w_up, layer, tm=1024, tf=512)
        xf, xb = mm_res_ln([hmid], ffn_w_down_b, layer, xf, g1, b1, tm=256)
    return xf.reshape(1, S, D_MODEL)
```

```python
import functools
import math

import numpy as np
import jax
import jax.numpy as jnp
from jax import lax
from jax.experimental import pallas as pl
from jax.experimental.pallas import tpu as pltpu

F32 = jnp.float32
BF16 = jnp.bfloat16

D_MODEL = 2048
DEPTH = 4
HEAD_DIM = 128
N_HEADS_A = 8
N_HEADS_B = 8
MOBA_BLOCK = 256
MOBA_TOPK = 3
N_BUCKETS = 32
MAX_DISTANCE = 128
MLA_HEADS = 16
MLA_Q_RANK = 512
MLA_KV_RANK = 512
MLA_NOPE = 128
MLA_ROPE = 64
MLA_V = 128
ROPE_THETA = 10000.0
DEEPNORM_ALPHA = (2 * DEPTH) ** 0.25
DA = N_HEADS_A * HEAD_DIM
DB = N_HEADS_B * HEAD_DIM
LN_EPS = 1e-5
RMS_EPS = 1e-6
LOG2E = math.log2(math.e)

LANES = 128
VMEM_BUDGET_BYTES = 56 * 2**20
MASK_VALUE = -1e30
ATTN_TILE = 2 * MOBA_BLOCK


def _vmem_limit(block_bytes, scratch_bytes=0, temp_bytes=0):
    est = 2 * block_bytes + scratch_bytes + temp_bytes + (4 << 20)
    return int(min(max(est, 16 << 20), VMEM_BUDGET_BYTES))


def _params(vmem_bytes, ngrid, flags=None):
    return pltpu.CompilerParams(
        dimension_semantics=("arbitrary",) * ngrid, vmem_limit_bytes=vmem_bytes, flags=flags)


def _split3(x):
    hi = x.astype(BF16).astype(F32)
    r1 = x - hi
    lo = r1.astype(BF16).astype(F32)
    return hi, lo, r1 - lo


def _mm_heads_kernel(x_ref, w_ref, cs_ref, o_ref, *, width):
    acc = jnp.dot(x_ref[...], w_ref[...], preferred_element_type=F32) * cs_ref[...]
    for hh in range(o_ref.shape[0]):
        o_ref[hh] = acc[:, hh * width:(hh + 1) * width].astype(o_ref.dtype)


def mm_heads(x, w, colscale, *, tm, tn, width=HEAD_DIM):
    M, K = x.shape
    N = w.shape[1]
    nh = tn // width
    blk = tm * K * 2 + K * tn * 2 + tm * tn * 2 + tn * 4
    return pl.pallas_call(
        functools.partial(_mm_heads_kernel, width=width),
        grid=(M // tm, N // tn),
        in_specs=[pl.BlockSpec((tm, K), lambda i, j: (i, 0)),
                  pl.BlockSpec((K, tn), lambda i, j: (0, j)),
                  pl.BlockSpec((1, tn), lambda i, j: (0, j))],
        out_specs=pl.BlockSpec((nh, tm, width), lambda i, j: (j, i, 0)),
        out_shape=jax.ShapeDtypeStruct((N // width, M, width), BF16),
        compiler_params=_params(_vmem_limit(blk, temp_bytes=2 * tm * tn * 4), 2),
        name="mm_heads",
    )(x, w, colscale)


def _fox_gate_kernel(x_ref, wf_ref, bf_ref, selq_ref, selk_ref, cq_ref, ck_ref,
                     qaug_ref, kaug_ref, carry_ref, *, tb, nh):
    @pl.when(pl.program_id(0) == 0)
    def _():
        carry_ref[...] = jnp.zeros_like(carry_ref)

    z = jnp.dot(x_ref[...], wf_ref[...], preferred_element_type=F32) + bf_ref[...]
    lf = jnp.minimum(z, 0.0) - jnp.log1p(jnp.exp(-jnp.abs(z)))
    row = lax.broadcasted_iota(jnp.int32, (tb, tb), 0)
    col = lax.broadcasted_iota(jnp.int32, (tb, tb), 1)
    tri = jnp.where(row >= col, 1.0, 0.0).astype(BF16)
    lf_terms = jnp.concatenate(_split3(lf), axis=1).astype(BF16)
    part = jnp.dot(tri, lf_terms, preferred_element_type=F32)
    cs = (part[:, :LANES] + part[:, LANES:2 * LANES] + part[:, 2 * LANES:]) + carry_ref[...]
    carry_ref[...] = cs[tb - 1:tb, :]
    lane = lax.broadcasted_iota(jnp.int32, (tb, LANES), 1)
    hi, lo, lo2 = [jnp.where(lane < nh, term, 0.0) for term in _split3(cs * LOG2E)]
    terms = (hi + pltpu.roll(lo, nh, 1) + pltpu.roll(lo2, 2 * nh, 1)).astype(BF16)
    qa = jnp.dot(terms, selq_ref[...], preferred_element_type=F32) + cq_ref[...]
    ka = jnp.dot(terms, selk_ref[...], preferred_element_type=F32) + ck_ref[...]
    for h in range(nh):
        qaug_ref[h] = qa[:, h * LANES:(h + 1) * LANES].astype(BF16)
        kaug_ref[h] = ka[:, h * LANES:(h + 1) * LANES].astype(BF16)


def _fox_selectors(nh):
    selq = np.zeros((LANES, nh * LANES), np.float32)
    selk = np.zeros((LANES, nh * LANES), np.float32)
    cq = np.zeros((1, nh * LANES), np.float32)
    ck = np.zeros((1, nh * LANES), np.float32)
    for h in range(nh):
        for term in range(3):
            selq[term * nh + h, h * LANES + term] = 1.0
            selk[term * nh + h, h * LANES + 3 + term] = -1.0
        cq[0, h * LANES + 3:h * LANES + 6] = 1.0
        ck[0, h * LANES:h * LANES + 3] = 1.0
    return (jnp.asarray(selq, dtype=BF16), jnp.asarray(selk, dtype=BF16),
            jnp.asarray(cq), jnp.asarray(ck))


def fox_gate(xb, wf_pad, bf_pad, *, tb=512, nh=N_HEADS_A):
    S, K = xb.shape
    selq, selk, cq, ck = _fox_selectors(nh)
    blk = (tb * K * 2 + K * LANES * 2 + 2 * nh * tb * LANES * 2
           + 2 * LANES * nh * LANES * 2)
    aug = jax.ShapeDtypeStruct((nh, S, LANES), BF16)
    fixed = lambda i: (0, 0)
    return pl.pallas_call(
        functools.partial(_fox_gate_kernel, tb=tb, nh=nh),
        grid=(S // tb,),
        in_specs=[pl.BlockSpec((tb, K), lambda i: (i, 0)),
                  pl.BlockSpec((K, LANES), fixed),
                  pl.BlockSpec((1, LANES), fixed),
                  pl.BlockSpec(selq.shape, fixed), pl.BlockSpec(selk.shape, fixed),
                  pl.BlockSpec(cq.shape, fixed), pl.BlockSpec(ck.shape, fixed)],
        out_specs=[pl.BlockSpec((nh, tb, LANES), lambda i: (0, i, 0)),
                   pl.BlockSpec((nh, tb, LANES), lambda i: (0, i, 0))],
        out_shape=[aug, aug],
        scratch_shapes=[pltpu.VMEM((1, LANES), F32)],
        compiler_params=_params(
            _vmem_limit(blk, temp_bytes=4 * tb * tb * 4 + 4 * tb * nh * LANES * 4), 1),
        name="fox_gate",
    )(xb, wf_pad, bf_pad, selq, selk, cq, ck)


def _moba_gate_kernel(rb_ref, q_ref, k_ref, qaug_ref, kmean_ref, *, nb, tr):
    B = MOBA_BLOCK
    h = pl.program_id(0)
    i = pl.program_id(1)
    far_bias = rb_ref[N_BUCKETS - 1, h] * LOG2E

    @pl.when(i == 0)
    def _():
        kf = k_ref[0].astype(F32).reshape(nb, B, HEAD_DIM)
        kmean_ref[...] = jnp.sum(kf, axis=1) * (1.0 / B)

    blk = lax.broadcasted_iota(jnp.int32, (nb, tr), 0)
    blk_f = blk.astype(F32)
    own = i * (tr // B) + lax.broadcasted_iota(jnp.int32, (nb, tr), 1) // B
    g = lax.dot_general(kmean_ref[...], q_ref[0].astype(F32), (((1,), (1,)), ((), ())),
                        preferred_element_type=F32, precision=lax.Precision.HIGHEST)
    neg_inf = -jnp.inf
    g = jnp.where(blk < own, g, neg_inf)
    mb = jnp.where(blk == own, 0.0, MASK_VALUE).astype(F32)
    for _ in range(MOBA_TOPK):
        mx = jnp.max(g, axis=0, keepdims=True)
        hit = jnp.logical_and(g == mx, mx > neg_inf)
        idx = jnp.min(jnp.where(hit, blk_f, float(nb)), axis=0, keepdims=True)
        pick = blk_f == idx
        mb = jnp.where(pick, far_bias, mb)
        g = jnp.where(pick, neg_inf, g)
    mb_hi = mb.astype(BF16).astype(F32)
    mb_lo = jnp.where(mb > 0.5 * MASK_VALUE, mb - mb_hi, 0.0)
    aug_t = jnp.concatenate([mb_hi, mb_lo, jnp.zeros((LANES - 2 * nb, tr), F32)], axis=0)
    qaug_ref[0] = jnp.transpose(aug_t).astype(BF16)


def moba_gate(heads, rel_bias, *, n_heads, q_off, k_off, tr=1024):
    S = heads.shape[1]
    nb = S // MOBA_BLOCK
    assert 2 * nb <= LANES
    blk = tr * HEAD_DIM * 2 + S * HEAD_DIM * 2 + tr * LANES * 2
    return pl.pallas_call(
        functools.partial(_moba_gate_kernel, nb=nb, tr=tr),
        grid=(n_heads, S // tr),
        in_specs=[pl.BlockSpec(memory_space=pltpu.SMEM),
                  pl.BlockSpec((1, tr, HEAD_DIM), lambda h, i: (h + q_off, i, 0)),
                  pl.BlockSpec((1, S, HEAD_DIM), lambda h, i: (h + k_off, 0, 0))],
        out_specs=pl.BlockSpec((1, tr, LANES), lambda h, i: (h, i, 0)),
        out_shape=jax.ShapeDtypeStruct((n_heads, S, LANES), BF16),
        scratch_shapes=[pltpu.VMEM((nb, HEAD_DIM), F32)],
        compiler_params=_params(
            _vmem_limit(blk, nb * HEAD_DIM * 4,
                        temp_bytes=S * HEAD_DIM * 4 + 12 * tr * LANES * 4), 2),
        name="moba_gate",
    )(rel_bias, heads, heads)


_KIND_ZERO, _KIND_MASK, _KIND_BIAS, _KIND_BIAS_MINUS_FAR = 0, 1, 2, 3


def _moba_bias_kernel(rb_ref, bucket_ref, kind_ref, o_ref):
    h = pl.program_id(0)
    far = rb_ref[N_BUCKETS - 1, h]
    bucket = bucket_ref[...]
    kind = kind_ref[...]
    val = jnp.zeros(bucket.shape, F32)
    for b in range(N_BUCKETS):
        val = jnp.where(bucket == b, rb_ref[b, h], val)
    val = jnp.where(kind == _KIND_BIAS_MINUS_FAR, val - far, val) * LOG2E
    o_ref[0] = jnp.where(kind == _KIND_ZERO, 0.0, jnp.where(kind == _KIND_MASK, MASK_VALUE, val))


def _t5_bucket_table(n):
    rel = np.arange(n)
    max_exact = N_BUCKETS // 2
    nf = np.maximum(rel, 1).astype(np.float32)
    large = max_exact + (np.log(nf / np.float32(max_exact))
                         / np.float32(math.log(MAX_DISTANCE / max_exact))
                         * np.float32(N_BUCKETS - max_exact)).astype(np.int32)
    large = np.minimum(large, N_BUCKETS - 1)
    return np.where(rel < max_exact, rel, large).astype(np.int32)


def _moba_bias_codes(t):
    B = MOBA_BLOCK
    table = _t5_bucket_table(t + 2 * B)
    rows = np.arange(t)[:, None]
    cols = np.arange(3 * t)[None, :] - t
    rel = rows - cols
    qblk = rows // B
    kblk = np.floor_divide(cols, B)
    bucket = table[np.clip(rel, 0, table.size - 1)] + np.zeros((t, 3 * t), np.int32)
    kind = np.full((t, 3 * t), _KIND_ZERO, np.int32)
    kind = np.where(kblk == qblk - 1, _KIND_BIAS_MINUS_FAR, kind)
    kind = np.where(kblk == qblk, np.where(rel >= 0, _KIND_BIAS, _KIND_MASK), kind)
    kind = np.where(cols >= t, _KIND_MASK, kind)
    return bucket.astype(np.int32), kind.astype(np.int32)


def moba_bias(rel_bias, *, n_heads, t):
    bucket, kind = _moba_bias_codes(t)
    blk = 3 * t * 3 * t * 4
    return pl.pallas_call(
        _moba_bias_kernel,
        grid=(n_heads,),
        in_specs=[pl.BlockSpec(memory_space=pltpu.SMEM),
                  pl.BlockSpec((t, 3 * t), lambda h: (0, 0)),
                  pl.BlockSpec((t, 3 * t), lambda h: (0, 0))],
        out_specs=pl.BlockSpec((1, t, 3 * t), lambda h: (h, 0, 0)),
        out_shape=jax.ShapeDtypeStruct((n_heads, t, 3 * t), F32),
        compiler_params=_params(_vmem_limit(blk, temp_bytes=4 * t * 3 * t * 4), 1),
        name="moba_bias",
    )(rel_bias, jnp.asarray(bucket), jnp.asarray(kind))


def _causal_bias(t):
    rows = np.arange(t)[:, None]
    cols = np.arange(3 * t)[None, :] - t
    return jnp.asarray(np.where(cols <= rows, 0.0, MASK_VALUE).astype(np.float32)[None])


def _ones_column(rows):
    lane = lax.broadcasted_iota(jnp.int32, (rows, LANES), 1)
    return jnp.where(lane == 0, 1.0, 0.0).astype(BF16)


def _qk(q, k):
    return lax.dot_general(q, k, (((1,), (1,)), ((), ())), preferred_element_type=F32)


def _flash_kernel(*refs, t, nparts, nsplit, group, prev_biased):
    q_refs = refs[:nparts]
    k_refs = refs[nparts:2 * nparts]
    v_ref, b3_ref, o_ref, m_ref, acc_ref = refs[2 * nparts:]
    qi = pl.program_id(1)
    dv = v_ref.shape[-1]
    r = t // nsplit
    chains = [(g, c) for g in range(group) for c in range(nsplit)]

    def head(ref, g):
        return g if ref.shape[0] == group else 0

    qs = [jnp.concatenate([ref[head(ref, g), c * r:(c + 1) * r, :] for ref in q_refs], axis=1)
          for g, c in chains]
    m_ref[...] = jnp.full_like(m_ref, MASK_VALUE)
    acc_ref[...] = jnp.zeros_like(acc_ref)

    def step(tile, width, bias_off=None):
        ks = pl.multiple_of(tile * t, t)
        ones = _ones_column(width)
        ss = []
        for n, (g, c) in enumerate(chains):
            k = jnp.concatenate([ref[head(ref, g), pl.ds(ks, width), :] for ref in k_refs],
                                axis=1)
            s = _qk(qs[n], k)
            if bias_off is not None:
                s = s + b3_ref[head(b3_ref, g), c * r:(c + 1) * r, pl.ds(bias_off, width)]
            ss.append(s)
        for n, (g, c) in enumerate(chains):
            v1 = jnp.concatenate([v_ref[g, pl.ds(ks, width), :], ones], axis=1)
            m = m_ref[n]
            m_new = jnp.maximum(m, jnp.max(ss[n], axis=1, keepdims=True))
            p = jnp.exp2(ss[n] - jnp.concatenate([m_new] * (width // LANES), axis=1))
            a = jnp.exp2(m - m_new)
            acc_ref[n] = (jnp.concatenate([a] * ((dv + LANES) // LANES), axis=1) * acc_ref[n]
                          + jnp.dot(p.astype(BF16), v1, preferred_element_type=F32))
            m_ref[n] = m_new

    n_plain = jnp.maximum(qi - 1, 0) if prev_biased else qi

    def body(kb, carry):
        step(4 * kb, 4 * t)
        return carry

    lax.fori_loop(0, n_plain // 4, body, 0)

    @pl.when(n_plain % 4 >= 2)
    def _():
        step((n_plain // 4) * 4, 2 * t)

    @pl.when(n_plain % 2 == 1)
    def _():
        step(n_plain - 1, t)

    if prev_biased:
        step(n_plain, 2 * t, bias_off=pl.multiple_of(jnp.where(qi == 0, t, 0), t))
    else:
        step(qi, t, bias_off=t)
    for g in range(group):
        out = []
        for c in range(nsplit):
            acc = acc_ref[g * nsplit + c]
            out.append(acc[:, :dv] * (1.0 / acc[:, dv:dv + 1]))
        o_ref[:, g * dv:(g + 1) * dv] = jnp.concatenate(out, axis=0).astype(o_ref.dtype)


def flash_attention(q_parts, k_parts, v_part, b3, *, n_heads, name, t=ATTN_TILE, nsplit=1,
                    group=2, prev_biased=False):
    v, v_off = v_part
    S, dv = v.shape[1], v.shape[2]

    def spec(arr, off, rows):
        g = group if off is not None and arr.shape[0] > 1 else 1
        assert off is None or off % group == 0
        blk_shape = (g, t if rows else arr.shape[1], arr.shape[2])
        if g == 1:
            index = lambda h, i: (0, i if rows else 0, 0)
        else:
            index = lambda h, i: (h + off // group, i if rows else 0, 0)
        return pl.BlockSpec(blk_shape, index), math.prod(blk_shape) * arr.dtype.itemsize

    operands = ([(a, o, True) for a, o in q_parts] + [(a, o, False) for a, o in k_parts]
                + [(v, v_off, False), (b3, 0 if b3.shape[0] > 1 else None, False)])
    in_specs, blk = [], t * group * dv * 2
    for arr, off, rows in operands:
        s, nbytes = spec(arr, off, rows)
        in_specs.append(s)
        blk += nbytes
    nchain = group * nsplit
    r = t // nsplit
    return pl.pallas_call(
        functools.partial(_flash_kernel, t=t, nparts=len(q_parts), nsplit=nsplit, group=group,
                          prev_biased=prev_biased),
        grid=(n_heads // group, S // t),
        in_specs=in_specs,
        out_specs=pl.BlockSpec((t, group * dv), lambda h, i: (i, h)),
        out_shape=jax.ShapeDtypeStruct((S, n_heads * dv), BF16),
        scratch_shapes=[pltpu.VMEM((nchain, r, LANES), F32),
                        pltpu.VMEM((nchain, r, dv + LANES), F32)],
        compiler_params=_params(
            _vmem_limit(blk, nchain * r * (dv + 2 * LANES) * 4,
                        temp_bytes=6 * group * t * t * 4), 2),
        name=name,
    )(*[arr for arr, _, _ in operands])


def _moba_key_onehot(S):
    nb = S // MOBA_BLOCK
    lanes = np.arange(LANES)[None, :]
    blk_of = (np.arange(S) // MOBA_BLOCK)[:, None]
    onehot = (lanes % nb == blk_of) & (lanes < 2 * nb)
    return jnp.asarray(onehot.astype(np.float32)[None], dtype=BF16)


def _mm_res_ln_kernel(*refs, n_a, nk):
    a_refs = refs[:n_a]
    w_ref, x_ref, g_ref, b_ref, o_ref, obf_ref = refs[n_a:n_a + 6]
    scratch = refs[n_a + 6:]
    part, row = None, 0
    for a_ref in a_refs:
        ka = a_ref.shape[1]
        d = jnp.dot(a_ref[...], w_ref[row:row + ka, :], preferred_element_type=F32)
        part = d if part is None else part + d
        row += ka

    def finish(y):
        z = DEEPNORM_ALPHA * x_ref[...] + y
        mu = jnp.mean(z, axis=-1, keepdims=True)
        zc = z - mu
        var = jnp.mean(zc * zc, axis=-1, keepdims=True)
        out = zc * lax.rsqrt(var + LN_EPS) * g_ref[...] + b_ref[...]
        o_ref[...] = out
        obf_ref[...] = out.astype(BF16)

    if nk == 1:
        finish(part)
        return
    acc_ref, = scratch
    k = pl.program_id(1)

    @pl.when(k == 0)
    def _():
        acc_ref[...] = part

    @pl.when(k > 0)
    def _():
        acc_ref[...] += part

    @pl.when(k == nk - 1)
    def _():
        finish(acc_ref[...])


def mm_res_ln(a_list, w_stack, layer, x, g, b, *, tm, tk=None):
    M = a_list[0].shape[0]
    K = sum(a.shape[1] for a in a_list)
    N = w_stack.shape[2]
    tk = K if tk is None else tk
    nk = K // tk
    assert len(a_list) == 1 or nk == 1
    blk = tm * tk * 2 + tm * N * 4 + 2 * N * 4 + tm * N * 4 + tm * N * 2
    scratch = [pltpu.VMEM((tm, N), F32)] if nk > 1 else []
    a_specs = [pl.BlockSpec((tm, tk if nk > 1 else a.shape[1]), lambda i, k: (i, k))
               for a in a_list]
    w_bufs = 1 if nk == 1 else 2
    return pl.pallas_call(
        functools.partial(_mm_res_ln_kernel, n_a=len(a_list), nk=nk),
        grid=(M // tm, nk),
        in_specs=a_specs + [
            pl.BlockSpec((None, tk, N), lambda i, k: (layer, k, 0),
                         pipeline_mode=pl.Buffered(w_bufs)),
            pl.BlockSpec((tm, N), lambda i, k: (i, 0)),
            pl.BlockSpec((1, N), lambda i, k: (0, 0)),
            pl.BlockSpec((1, N), lambda i, k: (0, 0))],
        out_specs=[pl.BlockSpec((tm, N), lambda i, k: (i, 0)),
                   pl.BlockSpec((tm, N), lambda i, k: (i, 0))],
        out_shape=[jax.ShapeDtypeStruct((M, N), F32),
                   jax.ShapeDtypeStruct((M, N), BF16)],
        scratch_shapes=scratch,
        compiler_params=_params(
            _vmem_limit(blk, (tm * N * 4 if nk > 1 else 0) + w_bufs * tk * N * 2,
                        temp_bytes=3 * tm * N * 4), 2),
        name="mm_res_ln",
    )(*a_list, w_stack, x, g, b)


def _ffn_gu_kernel(x_ref, wg_ref, wu_ref, o_ref, wgb_ref, wub_ref):
    @pl.when(pl.program_id(1) == 0)
    def _():
        wgb_ref[...] = wg_ref[...].astype(BF16)
        wub_ref[...] = wu_ref[...].astype(BF16)

    x = x_ref[...]
    g = jnp.dot(x, wgb_ref[...], preferred_element_type=F32)
    u = jnp.dot(x, wub_ref[...], preferred_element_type=F32)
    o_ref[...] = (g * jax.nn.sigmoid(g) * u).astype(o_ref.dtype)


def ffn_gate_up(xb, wg_stack, wu_stack, layer, *, tm, tf):
    M, K = xb.shape
    F = wg_stack.shape[2]
    blk = tm * K * 2 + 2 * K * tf * 4 + tm * tf * 2
    w_spec = pl.BlockSpec((None, K, tf), lambda j, i: (layer, 0, j))
    return pl.pallas_call(
        _ffn_gu_kernel,
        grid=(F // tf, M // tm),
        in_specs=[pl.BlockSpec((tm, K), lambda j, i: (i, 0)), w_spec, w_spec],
        out_specs=pl.BlockSpec((tm, tf), lambda j, i: (i, j)),
        out_shape=jax.ShapeDtypeStruct((M, F), BF16),
        scratch_shapes=[pltpu.VMEM((K, tf), BF16), pltpu.VMEM((K, tf), BF16)],
        compiler_params=_params(
            _vmem_limit(blk, 2 * K * tf * 2, temp_bytes=4 * tm * tf * 4), 2),
        name="ffn_gate_up",
    )(xb, wg_stack, wu_stack)


def _rope_lanes(r, cos_t, sin_a, sin_b):
    return (r * cos_t + pltpu.roll(r, LANES - MLA_ROPE // 2, 1) * sin_a
            + pltpu.roll(r, MLA_ROPE // 2, 1) * sin_b)


def _rms(x, g):
    return x * lax.rsqrt(jnp.mean(x * x, axis=-1, keepdims=True) + RMS_EPS) * g


def _mla_in_kernel(x_ref, w_ref, gq_ref, gkv_ref, cos_ref, sa_ref, sb_ref,
                   cq_ref, ckv_ref, kr_ref):
    h = jnp.dot(x_ref[...], w_ref[...], preferred_element_type=F32)
    cq_ref[...] = _rms(h[:, :MLA_Q_RANK], gq_ref[...]).astype(BF16)
    ckv_ref[...] = _rms(h[:, MLA_Q_RANK:MLA_Q_RANK + MLA_KV_RANK], gkv_ref[...]).astype(BF16)
    r = h[:, MLA_Q_RANK + MLA_KV_RANK:]
    kr_ref[...] = _rope_lanes(r, cos_ref[...], sa_ref[...], sb_ref[...]).astype(BF16)


def mla_in(xb, w_pad, gq, gkv, cos_t, sin_a, sin_b, *, tm=512):
    S, K = xb.shape
    N = w_pad.shape[1]
    blk = (tm * K * 2 + K * N * 2 + 3 * tm * LANES * 4
           + tm * (MLA_Q_RANK + MLA_KV_RANK + LANES) * 2)
    row = lambda i: (i, 0)
    fixed = lambda i: (0, 0)
    return pl.pallas_call(
        _mla_in_kernel,
        grid=(S // tm,),
        in_specs=[pl.BlockSpec((tm, K), row), pl.BlockSpec((K, N), fixed),
                  pl.BlockSpec((1, MLA_Q_RANK), fixed), pl.BlockSpec((1, MLA_KV_RANK), fixed),
                  pl.BlockSpec((tm, LANES), row), pl.BlockSpec((tm, LANES), row),
                  pl.BlockSpec((tm, LANES), row)],
        out_specs=[pl.BlockSpec((tm, MLA_Q_RANK), row), pl.BlockSpec((tm, MLA_KV_RANK), row),
                   pl.BlockSpec((tm, LANES), row)],
        out_shape=[jax.ShapeDtypeStruct((S, MLA_Q_RANK), BF16),
                   jax.ShapeDtypeStruct((S, MLA_KV_RANK), BF16),
                   jax.ShapeDtypeStruct((S, LANES), BF16)],
        compiler_params=_params(_vmem_limit(blk, temp_bytes=3 * tm * N * 4), 1),
        name="mla_in",
    )(xb, w_pad, gq, gkv, cos_t, sin_a, sin_b)


def _mla_q_kernel(cq_ref, w_ref, cos_ref, sa_ref, sb_ref, q_ref, *, qscale):
    acc = jnp.dot(cq_ref[...], w_ref[...], preferred_element_type=F32) * qscale
    cos_t, sin_a, sin_b = cos_ref[...], sa_ref[...], sb_ref[...]
    for hh in range(q_ref.shape[0]):
        base = hh * 2 * LANES
        q_ref[hh, :, 0:LANES] = acc[:, base:base + LANES].astype(BF16)
        r = acc[:, base + LANES:base + 2 * LANES]
        q_ref[hh, :, LANES:2 * LANES] = _rope_lanes(r, cos_t, sin_a, sin_b).astype(BF16)


def mla_q(cq, w_pad, cos_t, sin_a, sin_b, *, qscale, tm=512):
    S, K = cq.shape
    N = w_pad.shape[1]
    nh = N // (2 * LANES)
    blk = tm * K * 2 + K * N * 2 + 3 * tm * LANES * 4 + tm * N * 2
    row = lambda i: (i, 0)
    return pl.pallas_call(
        functools.partial(_mla_q_kernel, qscale=qscale),
        grid=(S // tm,),
        in_specs=[pl.BlockSpec((tm, K), row), pl.BlockSpec((K, N), lambda i: (0, 0)),
                  pl.BlockSpec((tm, LANES), row), pl.BlockSpec((tm, LANES), row),
                  pl.BlockSpec((tm, LANES), row)],
        out_specs=pl.BlockSpec((nh, tm, 2 * LANES), lambda i: (0, i, 0)),
        out_shape=jax.ShapeDtypeStruct((nh, S, 2 * LANES), BF16),
        compiler_params=_params(_vmem_limit(blk, temp_bytes=2 * tm * N * 4), 1),
        name="mla_q",
    )(cq, w_pad, cos_t, sin_a, sin_b)


def _mla_kv_kernel(ckv_ref, w_ref, kr_ref, k_ref, v_ref):
    acc = jnp.dot(ckv_ref[...], w_ref[...], preferred_element_type=F32)
    kr = kr_ref[...]
    for hh in range(k_ref.shape[0]):
        base = hh * (MLA_NOPE + MLA_V)
        k_ref[hh, :, 0:LANES] = acc[:, base:base + MLA_NOPE].astype(BF16)
        k_ref[hh, :, LANES:2 * LANES] = kr
        v_ref[hh] = acc[:, base + MLA_NOPE:base + MLA_NOPE + MLA_V].astype(BF16)


def mla_kv(ckv, w, kr, *, tm=512):
    S, K = ckv.shape
    N = w.shape[1]
    nh = N // (MLA_NOPE + MLA_V)
    blk = tm * K * 2 + K * N * 2 + tm * LANES * 2 + nh * tm * (2 * LANES + MLA_V) * 2
    row = lambda i: (i, 0)
    return pl.pallas_call(
        _mla_kv_kernel,
        grid=(S // tm,),
        in_specs=[pl.BlockSpec((tm, K), row), pl.BlockSpec((K, N), lambda i: (0, 0)),
                  pl.BlockSpec((tm, LANES), row)],
        out_specs=[pl.BlockSpec((nh, tm, 2 * LANES), lambda i: (0, i, 0)),
                   pl.BlockSpec((nh, tm, MLA_V), lambda i: (0, i, 0))],
        out_shape=[jax.ShapeDtypeStruct((nh, S, 2 * LANES), BF16),
                   jax.ShapeDtypeStruct((nh, S, MLA_V), BF16)],
        compiler_params=_params(_vmem_limit(blk, temp_bytes=2 * tm * N * 4), 1),
        name="mla_kv",
    )(ckv, w, kr)


def _rope_lane_tables(S):
    half = MLA_ROPE // 2
    inv = ROPE_THETA ** (-np.arange(0, MLA_ROPE, 2, dtype=np.float64) / MLA_ROPE)
    ang = np.arange(S, dtype=np.float64)[:, None] * inv[None, :]
    cos, sin = np.cos(ang), np.sin(ang)
    z = np.zeros((S, half))
    z2 = np.zeros((S, LANES - MLA_ROPE))
    cos_t = np.concatenate([cos, cos, z2], axis=1)
    sin_a = np.concatenate([-sin, z, z2], axis=1)
    sin_b = np.concatenate([z, sin, z2], axis=1)
    return tuple(jnp.asarray(a.astype(np.float32)) for a in (cos_t, sin_a, sin_b))


def _pad_cols(w, n):
    return jnp.pad(w, ((0, 0), (0, n - w.shape[1])))


def kernel(x, ab_w_in, ab_forget_bias, ab_w_out, rel_bias, mla_w_in, mla_q_norm,
           mla_kv_norm, mla_w_uq, mla_w_ukv, mla_w_out, ffn_w_gate, ffn_w_up,
           ffn_w_down, ln_g, ln_b):
    S = x.shape[1]
    xf = x.reshape(S, D_MODEL)
    xb = xf.astype(BF16)
    cos_t, sin_a, sin_b = _rope_lane_tables(S)
    nf = 3 * DA + N_HEADS_A
    qs = HEAD_DIM ** -0.5 * LOG2E
    colscale = np.ones((1, 3 * DA + 3 * DB), np.float32)
    colscale[:, :DA] = qs
    colscale[:, 3 * DA:3 * DA + DB] = qs
    colscale = jnp.asarray(colscale)
    causal_b3 = _causal_bias(ATTN_TILE)
    moba_b3 = moba_bias(rel_bias, n_heads=N_HEADS_B, t=ATTN_TILE)
    moba_kaug = _moba_key_onehot(S)
    hb = 3 * N_HEADS_A
    ab_w_out_b = ab_w_out.astype(BF16)
    mla_w_out_b = mla_w_out.astype(BF16)
    ffn_w_down_b = ffn_w_down.astype(BF16)

    for layer in range(DEPTH):
        j = layer // 2
        g0, b0 = ln_g[layer, 0][None, :], ln_b[layer, 0][None, :]
        g1, b1 = ln_g[layer, 1][None, :], ln_b[layer, 1][None, :]
        if layer % 2 == 0:
            w_in = ab_w_in[j]
            w_qkv = jnp.concatenate([w_in[:, :3 * DA], w_in[:, nf:]], axis=1).astype(BF16)
            w_f = _pad_cols(w_in[:, 3 * DA:nf], LANES).astype(BF16)
            b_f = _pad_cols(ab_forget_bias[j][None, :], LANES)
            heads = mm_heads(xb, w_qkv, colscale, tm=1024, tn=512)
            qaug, kaug = fox_gate(xb, w_f, b_f)
            ya = flash_attention([(heads, 0), (qaug, 0)], [(heads, N_HEADS_A), (kaug, 0)],
                                 (heads, 2 * N_HEADS_A), causal_b3,
                                 n_heads=N_HEADS_A, name="flash_fox")
            qaug_b = moba_gate(heads, rel_bias, n_heads=N_HEADS_B, q_off=hb,
                               k_off=hb + N_HEADS_B)
            yb = flash_attention([(heads, hb), (qaug_b, 0)],
                                 [(heads, hb + N_HEADS_B), (moba_kaug, None)],
                                 (heads, hb + 2 * N_HEADS_B), moba_b3,
                                 n_heads=N_HEADS_B, name="flash_moba", prev_biased=True)
            y, w_out = [ya, yb], ab_w_out_b
        else:
            w_in = _pad_cols(mla_w_in[j], MLA_Q_RANK + MLA_KV_RANK + LANES).astype(BF16)
            cq, ckv, kr = mla_in(xb, w_in, mla_q_norm[j][None, :], mla_kv_norm[j][None, :],
                                 cos_t, sin_a, sin_b)
            w_uq = mla_w_uq[j].reshape(MLA_Q_RANK, MLA_HEADS, MLA_NOPE + MLA_ROPE)
            w_uq = jnp.pad(w_uq, ((0, 0), (0, 0), (0, 2 * LANES - MLA_NOPE - MLA_ROPE)))
            w_uq = w_uq.reshape(MLA_Q_RANK, MLA_HEADS * 2 * LANES).astype(BF16)
            q_full = mla_q(cq, w_uq, cos_t, sin_a, sin_b,
                           qscale=(MLA_NOPE + MLA_ROPE) ** -0.5 * LOG2E)
            k_full, v = mla_kv(ckv, mla_w_ukv[j].astype(BF16), kr)
            y = [flash_attention([(q_full, 0)], [(k_full, 0)], (v, 0), causal_b3,
                                 n_heads=MLA_HEADS, name="flash_mla")]
            w_out = mla_w_out_b
        xf, xb = mm_res_ln(y, w_out, j, xf, g0, b0, tm=512)
        hmid = ffn_gate_up(xb, ffn_w_gate, ffn_w_up, layer, tm=1024, tf=512)
        xf, xb = mm_res_ln([hmid], ffn_w_down_b, layer, xf, g1, b1, tm=256)
    return xf.reshape(1, S, D_MODEL)
```

```python
import functools
import math

import numpy as np
import jax
import jax.numpy as jnp
from jax import lax
from jax.experimental import pallas as pl
from jax.experimental.pallas import tpu as pltpu

F32 = jnp.float32
BF16 = jnp.bfloat16

D_MODEL = 2048
DEPTH = 4
HEAD_DIM = 128
N_HEADS_A = 8
N_HEADS_B = 8
MOBA_BLOCK = 256
MOBA_TOPK = 3
N_BUCKETS = 32
MAX_DISTANCE = 128
MLA_HEADS = 16
MLA_Q_RANK = 512
MLA_KV_RANK = 512
MLA_NOPE = 128
MLA_ROPE = 64
MLA_V = 128
ROPE_THETA = 10000.0
DEEPNORM_ALPHA = (2 * DEPTH) ** 0.25
DA = N_HEADS_A * HEAD_DIM
DB = N_HEADS_B * HEAD_DIM
LN_EPS = 1e-5
RMS_EPS = 1e-6
LOG2E = math.log2(math.e)

LANES = 128
VMEM_BUDGET_BYTES = 56 * 2**20
MASK_VALUE = -1e30
ATTN_TILE = 2 * MOBA_BLOCK


def _vmem_limit(block_bytes, scratch_bytes=0, temp_bytes=0):
    est = 2 * block_bytes + scratch_bytes + temp_bytes + (4 << 20)
    return int(min(max(est, 16 << 20), VMEM_BUDGET_BYTES))


def _params(vmem_bytes, ngrid, flags=None):
    return pltpu.CompilerParams(
        dimension_semantics=("arbitrary",) * ngrid, vmem_limit_bytes=vmem_bytes, flags=flags)


def _split3(x):
    hi = x.astype(BF16).astype(F32)
    r1 = x - hi
    lo = r1.astype(BF16).astype(F32)
    return hi, lo, r1 - lo


def _mm_heads_kernel(x_ref, w_ref, cs_ref, o_ref, *, width):
    acc = jnp.dot(x_ref[...], w_ref[...], preferred_element_type=F32) * cs_ref[...]
    for hh in range(o_ref.shape[0]):
        o_ref[hh] = acc[:, hh * width:(hh + 1) * width].astype(o_ref.dtype)


def mm_heads(x, w, colscale, *, tm, tn, width=HEAD_DIM):
    M, K = x.shape
    N = w.shape[1]
    nh = tn // width
    blk = tm * K * 2 + K * tn * 2 + tm * tn * 2 + tn * 4
    return pl.pallas_call(
        functools.partial(_mm_heads_kernel, width=width),
        grid=(M // tm, N // tn),
        in_specs=[pl.BlockSpec((tm, K), lambda i, j: (i, 0)),
                  pl.BlockSpec((K, tn), lambda i, j: (0, j)),
                  pl.BlockSpec((1, tn), lambda i, j: (0, j))],
        out_specs=pl.BlockSpec((nh, tm, width), lambda i, j: (j, i, 0)),
        out_shape=jax.ShapeDtypeStruct((N // width, M, width), BF16),
        compiler_params=_params(_vmem_limit(blk, temp_bytes=2 * tm * tn * 4), 2),
        name="mm_heads",
    )(x, w, colscale)


def _fox_gate_kernel(x_ref, wf_ref, bf_ref, selq_ref, selk_ref, cq_ref, ck_ref,
                     qaug_ref, kaug_ref, carry_ref, *, tb, nh):
    @pl.when(pl.program_id(0) == 0)
    def _():
        carry_ref[...] = jnp.zeros_like(carry_ref)

    z = jnp.dot(x_ref[...], wf_ref[...], preferred_element_type=F32) + bf_ref[...]
    lf = jnp.minimum(z, 0.0) - jnp.log1p(jnp.exp(-jnp.abs(z)))
    row = lax.broadcasted_iota(jnp.int32, (tb, tb), 0)
    col = lax.broadcasted_iota(jnp.int32, (tb, tb), 1)
    tri = jnp.where(row >= col, 1.0, 0.0).astype(BF16)
    lf_terms = jnp.concatenate(_split3(lf), axis=1).astype(BF16)
    part = jnp.dot(tri, lf_terms, preferred_element_type=F32)
    cs = (part[:, :LANES] + part[:, LANES:2 * LANES] + part[:, 2 * LANES:]) + carry_ref[...]
    carry_ref[...] = cs[tb - 1:tb, :]
    lane = lax.broadcasted_iota(jnp.int32, (tb, LANES), 1)
    hi, lo, lo2 = [jnp.where(lane < nh, term, 0.0) for term in _split3(cs * LOG2E)]
    terms = (hi + pltpu.roll(lo, nh, 1) + pltpu.roll(lo2, 2 * nh, 1)).astype(BF16)
    qa = jnp.dot(terms, selq_ref[...], preferred_element_type=F32) + cq_ref[...]
    ka = jnp.dot(terms, selk_ref[...], preferred_element_type=F32) + ck_ref[...]
    for h in range(nh):
        qaug_ref[h] = qa[:, h * LANES:(h + 1) * LANES].astype(BF16)
        kaug_ref[h] = ka[:, h * LANES:(h + 1) * LANES].astype(BF16)


def _fox_selectors(nh):
    selq = np.zeros((LANES, nh * LANES), np.float32)
    selk = np.zeros((LANES, nh * LANES), np.float32)
    cq = np.zeros((1, nh * LANES), np.float32)
    ck = np.zeros((1, nh * LANES), np.float32)
    for h in range(nh):
        for term in range(3):
            selq[term * nh + h, h * LANES + term] = 1.0
            selk[term * nh + h, h * LANES + 3 + term] = -1.0
        cq[0, h * LANES + 3:h * LANES + 6] = 1.0
        ck[0, h * LANES:h * LANES + 3] = 1.0
    return (jnp.asarray(selq, dtype=BF16), jnp.asarray(selk, dtype=BF16),
            jnp.asarray(cq), jnp.asarray(ck))


def fox_gate(xb, wf_pad, bf_pad, *, tb=512, nh=N_HEADS_A):
    S, K = xb.shape
    selq, selk, cq, ck = _fox_selectors(nh)
    blk = (tb * K * 2 + K * LANES * 2 + 2 * nh * tb * LANES * 2
           + 2 * LANES * nh * LANES * 2)
    aug = jax.ShapeDtypeStruct((nh, S, LANES), BF16)
    fixed = lambda i: (0, 0)
    return pl.pallas_call(
        functools.partial(_fox_gate_kernel, tb=tb, nh=nh),
        grid=(S // tb,),
        in_specs=[pl.BlockSpec((tb, K), lambda i: (i, 0)),
                  pl.BlockSpec((K, LANES), fixed),
                  pl.BlockSpec((1, LANES), fixed),
                  pl.BlockSpec(selq.shape, fixed), pl.BlockSpec(selk.shape, fixed),
                  pl.BlockSpec(cq.shape, fixed), pl.BlockSpec(ck.shape, fixed)],
        out_specs=[pl.BlockSpec((nh, tb, LANES), lambda i: (0, i, 0)),
                   pl.BlockSpec((nh, tb, LANES), lambda i: (0, i, 0))],
        out_shape=[aug, aug],
        scratch_shapes=[pltpu.VMEM((1, LANES), F32)],
        compiler_params=_params(
            _vmem_limit(blk, temp_bytes=4 * tb * tb * 4 + 4 * tb * nh * LANES * 4), 1),
        name="fox_gate",
    )(xb, wf_pad, bf_pad, selq, selk, cq, ck)


def _moba_gate_kernel(rb_ref, q_ref, k_ref, qaug_ref, kmean_ref, *, nb, tr):
    B = MOBA_BLOCK
    h = pl.program_id(0)
    i = pl.program_id(1)
    far_bias = rb_ref[N_BUCKETS - 1, h] * LOG2E

    @pl.when(i == 0)
    def _():
        kf = k_ref[0].astype(F32).reshape(nb, B, HEAD_DIM)
        kmean_ref[...] = jnp.sum(kf, axis=1) * (1.0 / B)

    blk = lax.broadcasted_iota(jnp.int32, (nb, tr), 0)
    blk_f = blk.astype(F32)
    own = i * (tr // B) + lax.broadcasted_iota(jnp.int32, (nb, tr), 1) // B
    g = lax.dot_general(kmean_ref[...], q_ref[0].astype(F32), (((1,), (1,)), ((), ())),
                        preferred_element_type=F32, precision=lax.Precision.HIGHEST)
    neg_inf = -jnp.inf
    g = jnp.where(blk < own, g, neg_inf)
    mb = jnp.where(blk == own, 0.0, MASK_VALUE).astype(F32)
    for _ in range(MOBA_TOPK):
        mx = jnp.max(g, axis=0, keepdims=True)
        hit = jnp.logical_and(g == mx, mx > neg_inf)
        idx = jnp.min(jnp.where(hit, blk_f, float(nb)), axis=0, keepdims=True)
        pick = blk_f == idx
        mb = jnp.where(pick, far_bias, mb)
        g = jnp.where(pick, neg_inf, g)
    mb_hi = mb.astype(BF16).astype(F32)
    mb_lo = jnp.where(mb > 0.5 * MASK_VALUE, mb - mb_hi, 0.0)
    aug_t = jnp.concatenate([mb_hi, mb_lo, jnp.zeros((LANES - 2 * nb, tr), F32)], axis=0)
    qaug_ref[0] = jnp.transpose(aug_t).astype(BF16)


def moba_gate(heads, rel_bias, *, n_heads, q_off, k_off, tr=1024):
    S = heads.shape[1]
    nb = S // MOBA_BLOCK
    assert 2 * nb <= LANES
    blk = tr * HEAD_DIM * 2 + S * HEAD_DIM * 2 + tr * LANES * 2
    return pl.pallas_call(
        functools.partial(_moba_gate_kernel, nb=nb, tr=tr),
        grid=(n_heads, S // tr),
        in_specs=[pl.BlockSpec(memory_space=pltpu.SMEM),
                  pl.BlockSpec((1, tr, HEAD_DIM), lambda h, i: (h + q_off, i, 0)),
                  pl.BlockSpec((1, S, HEAD_DIM), lambda h, i: (h + k_off, 0, 0))],
        out_specs=pl.BlockSpec((1, tr, LANES), lambda h, i: (h, i, 0)),
        out_shape=jax.ShapeDtypeStruct((n_heads, S, LANES), BF16),
        scratch_shapes=[pltpu.VMEM((nb, HEAD_DIM), F32)],
        compiler_params=_params(
            _vmem_limit(blk, nb * HEAD_DIM * 4,
                        temp_bytes=S * HEAD_DIM * 4 + 12 * tr * LANES * 4), 2),
        name="moba_gate",
    )(rel_bias, heads, heads)


def _moba_bias_kernel(rb_ref, own_ref, prev_ref, o_ref):
    B = MOBA_BLOCK
    h = pl.program_id(0)
    far = rb_ref[N_BUCKETS - 1, h]
    own_bucket = own_ref[...]
    prev_bucket = prev_ref[...]
    own = jnp.zeros((B, B), F32)
    prev = jnp.zeros((B, B), F32)
    for b in range(N_BUCKETS):
        val = rb_ref[b, h]
        own = jnp.where(own_bucket == b, val, own)
        prev = jnp.where(prev_bucket == b, val, prev)
    own = jnp.where(own_bucket < 0, MASK_VALUE, own * LOG2E)
    prev = (prev - far) * LOG2E
    o_ref[0] = jnp.zeros(o_ref.shape[1:], F32)
    o_ref[0, 0:B, B:2 * B] = prev
    o_ref[0, 0:B, 2 * B:3 * B] = own
    o_ref[0, B:2 * B, 2 * B:3 * B] = prev
    o_ref[0, B:2 * B, 3 * B:4 * B] = own
    o_ref[0, :, 4 * B:6 * B] = jnp.full((2 * B, 2 * B), MASK_VALUE, F32)


def _t5_bucket_table(n):
    rel = np.arange(n)
    max_exact = N_BUCKETS // 2
    nf = np.maximum(rel, 1).astype(np.float32)
    large = max_exact + (np.log(nf / np.float32(max_exact))
                         / np.float32(math.log(MAX_DISTANCE / max_exact))
                         * np.float32(N_BUCKETS - max_exact)).astype(np.int32)
    large = np.minimum(large, N_BUCKETS - 1)
    return np.where(rel < max_exact, rel, large).astype(np.int32)


def moba_bias(rel_bias, *, n_heads, t):
    B = MOBA_BLOCK
    assert t == 2 * B
    table = _t5_bucket_table(2 * B)
    rel = np.arange(B)[:, None] - np.arange(B)[None, :]
    own_bucket = np.where(rel >= 0, table[np.maximum(rel, 0)], -1).astype(np.int32)
    prev_bucket = table[rel + B].astype(np.int32)
    blk = t * 3 * t * 4 + 2 * B * B * 4
    return pl.pallas_call(
        _moba_bias_kernel,
        grid=(n_heads,),
        in_specs=[pl.BlockSpec(memory_space=pltpu.SMEM),
                  pl.BlockSpec((B, B), lambda h: (0, 0)),
                  pl.BlockSpec((B, B), lambda h: (0, 0))],
        out_specs=pl.BlockSpec((1, t, 3 * t), lambda h: (h, 0, 0)),
        out_shape=jax.ShapeDtypeStruct((n_heads, t, 3 * t), F32),
        compiler_params=_params(_vmem_limit(blk, temp_bytes=8 * B * B * 4), 1),
        name="moba_bias",
    )(rel_bias, jnp.asarray(own_bucket), jnp.asarray(prev_bucket))


def _causal_bias(t):
    rows = np.arange(t)[:, None]
    cols = np.arange(3 * t)[None, :] - t
    return jnp.asarray(np.where(cols <= rows, 0.0, MASK_VALUE).astype(np.float32)[None])


def _ones_column(rows):
    lane = lax.broadcasted_iota(jnp.int32, (rows, LANES), 1)
    return jnp.where(lane == 0, 1.0, 0.0).astype(BF16)


def _qk(q, k):
    return lax.dot_general(q, k, (((1,), (1,)), ((), ())), preferred_element_type=F32)


def _flash_kernel(*refs, t, nparts, nsplit, group, prev_biased):
    q_refs = refs[:nparts]
    k_refs = refs[nparts:2 * nparts]
    v_ref, b3_ref, o_ref, m_ref, acc_ref = refs[2 * nparts:]
    qi = pl.program_id(1)
    dv = v_ref.shape[-1]
    r = t // nsplit
    chains = [(g, c) for g in range(group) for c in range(nsplit)]

    def head(ref, g):
        return g if ref.shape[0] == group else 0

    qs = [jnp.concatenate([ref[head(ref, g), c * r:(c + 1) * r, :] for ref in q_refs], axis=1)
          for g, c in chains]
    m_ref[...] = jnp.full_like(m_ref, MASK_VALUE)
    acc_ref[...] = jnp.zeros_like(acc_ref)

    def step(tile, width, bias_off=None):
        ks = pl.multiple_of(tile * t, t)
        ones = _ones_column(width)
        ss = []
        for n, (g, c) in enumerate(chains):
            k = jnp.concatenate([ref[head(ref, g), pl.ds(ks, width), :] for ref in k_refs],
                                axis=1)
            s = _qk(qs[n], k)
            if bias_off is not None:
                s = s + b3_ref[head(b3_ref, g), c * r:(c + 1) * r, pl.ds(bias_off, width)]
            ss.append(s)
        for n, (g, c) in enumerate(chains):
            v1 = jnp.concatenate([v_ref[g, pl.ds(ks, width), :], ones], axis=1)
            m = m_ref[n]
            m_new = jnp.maximum(m, jnp.max(ss[n], axis=1, keepdims=True))
            p = jnp.exp2(ss[n] - jnp.concatenate([m_new] * (width // LANES), axis=1))
            a = jnp.exp2(m - m_new)
            acc_ref[n] = (jnp.concatenate([a] * ((dv + LANES) // LANES), axis=1) * acc_ref[n]
                          + jnp.dot(p.astype(BF16), v1, preferred_element_type=F32))
            m_ref[n] = m_new

    n_plain = jnp.maximum(qi - 1, 0) if prev_biased else qi

    def body(kb, carry):
        step(4 * kb, 4 * t)
        return carry

    lax.fori_loop(0, n_plain // 4, body, 0)

    @pl.when(n_plain % 4 >= 2)
    def _():
        step((n_plain // 4) * 4, 2 * t)

    @pl.when(n_plain % 2 == 1)
    def _():
        step(n_plain - 1, t)

    if prev_biased:
        step(n_plain, 2 * t, bias_off=pl.multiple_of(jnp.where(qi == 0, t, 0), t))
    else:
        step(qi, t, bias_off=t)
    for g in range(group):
        out = []
        for c in range(nsplit):
            acc = acc_ref[g * nsplit + c]
            out.append(acc[:, :dv] * (1.0 / acc[:, dv:dv + 1]))
        o_ref[:, g * dv:(g + 1) * dv] = jnp.concatenate(out, axis=0).astype(o_ref.dtype)


def flash_attention(q_parts, k_parts, v_part, b3, *, n_heads, name, t=ATTN_TILE, nsplit=1,
                    group=2, prev_biased=False):
    v, v_off = v_part
    S, dv = v.shape[1], v.shape[2]

    def spec(arr, off, rows):
        g = group if off is not None and arr.shape[0] > 1 else 1
        assert off is None or off % group == 0
        blk_shape = (g, t if rows else arr.shape[1], arr.shape[2])
        if g == 1:
            index = lambda h, i: (0, i if rows else 0, 0)
        else:
            index = lambda h, i: (h + off // group, i if rows else 0, 0)
        return pl.BlockSpec(blk_shape, index), math.prod(blk_shape) * arr.dtype.itemsize

    operands = ([(a, o, True) for a, o in q_parts] + [(a, o, False) for a, o in k_parts]
                + [(v, v_off, False), (b3, 0 if b3.shape[0] > 1 else None, False)])
    in_specs, blk = [], t * group * dv * 2
    for arr, off, rows in operands:
        s, nbytes = spec(arr, off, rows)
        in_specs.append(s)
        blk += nbytes
    nchain = group * nsplit
    r = t // nsplit
    return pl.pallas_call(
        functools.partial(_flash_kernel, t=t, nparts=len(q_parts), nsplit=nsplit, group=group,
                          prev_biased=prev_biased),
        grid=(n_heads // group, S // t),
        in_specs=in_specs,
        out_specs=pl.BlockSpec((t, group * dv), lambda h, i: (i, h)),
        out_shape=jax.ShapeDtypeStruct((S, n_heads * dv), BF16),
        scratch_shapes=[pltpu.VMEM((nchain, r, LANES), F32),
                        pltpu.VMEM((nchain, r, dv + LANES), F32)],
        compiler_params=_params(
            _vmem_limit(blk, nchain * r * (dv + 2 * LANES) * 4,
                        temp_bytes=6 * group * t * t * 4), 2),
        name=name,
    )(*[arr for arr, _, _ in operands])


def _moba_key_onehot(S):
    nb = S // MOBA_BLOCK
    lanes = np.arange(LANES)[None, :]
    blk_of = (np.arange(S) // MOBA_BLOCK)[:, None]
    onehot = (lanes % nb == blk_of) & (lanes < 2 * nb)
    return jnp.asarray(onehot.astype(np.float32)[None], dtype=BF16)


def _mm_res_ln_kernel(*refs, n_a, nk):
    a_refs = refs[:n_a]
    w_ref, x_ref, g_ref, b_ref, o_ref, obf_ref = refs[n_a:n_a + 6]
    scratch = refs[n_a + 6:]
    part, row = None, 0
    for a_ref in a_refs:
        ka = a_ref.shape[1]
        d = jnp.dot(a_ref[...], w_ref[row:row + ka, :], preferred_element_type=F32)
        part = d if part is None else part + d
        row += ka

    def finish(y):
        z = DEEPNORM_ALPHA * x_ref[...] + y
        mu = jnp.mean(z, axis=-1, keepdims=True)
        zc = z - mu
        var = jnp.mean(zc * zc, axis=-1, keepdims=True)
        out = zc * lax.rsqrt(var + LN_EPS) * g_ref[...] + b_ref[...]
        o_ref[...] = out
        obf_ref[...] = out.astype(BF16)

    if nk == 1:
        finish(part)
        return
    acc_ref, = scratch
    k = pl.program_id(1)

    @pl.when(k == 0)
    def _():
        acc_ref[...] = part

    @pl.when(k > 0)
    def _():
        acc_ref[...] += part

    @pl.when(k == nk - 1)
    def _():
        finish(acc_ref[...])


def mm_res_ln(a_list, w_stack, layer, x, g, b, *, tm, tk=None):
    M = a_list[0].shape[0]
    K = sum(a.shape[1] for a in a_list)
    N = w_stack.shape[2]
    tk = K if tk is None else tk
    nk = K // tk
    assert len(a_list) == 1 or nk == 1
    blk = tm * tk * 2 + tm * N * 4 + 2 * N * 4 + tm * N * 4 + tm * N * 2
    scratch = [pltpu.VMEM((tm, N), F32)] if nk > 1 else []
    a_specs = [pl.BlockSpec((tm, tk if nk > 1 else a.shape[1]), lambda i, k: (i, k))
               for a in a_list]
    w_bufs = 1 if nk == 1 else 2
    return pl.pallas_call(
        functools.partial(_mm_res_ln_kernel, n_a=len(a_list), nk=nk),
        grid=(M // tm, nk),
        in_specs=a_specs + [
            pl.BlockSpec((None, tk, N), lambda i, k: (layer, k, 0),
                         pipeline_mode=pl.Buffered(w_bufs)),
            pl.BlockSpec((tm, N), lambda i, k: (i, 0)),
            pl.BlockSpec((1, N), lambda i, k: (0, 0)),
            pl.BlockSpec((1, N), lambda i, k: (0, 0))],
        out_specs=[pl.BlockSpec((tm, N), lambda i, k: (i, 0)),
                   pl.BlockSpec((tm, N), lambda i, k: (i, 0))],
        out_shape=[jax.ShapeDtypeStruct((M, N), F32),
                   jax.ShapeDtypeStruct((M, N), BF16)],
        scratch_shapes=scratch,
        compiler_params=_params(
            _vmem_limit(blk, (tm * N * 4 if nk > 1 else 0) + w_bufs * tk * N * 2,
                        temp_bytes=3 * tm * N * 4), 2),
        name="mm_res_ln",
    )(*a_list, w_stack, x, g, b)


def _ffn_gu_kernel(x_ref, wg_ref, wu_ref, wd_ref, o_ref, wdb_ref, wgb_ref, wub_ref):
    @pl.when(pl.program_id(1) == 0)
    def _():
        wgb_ref[...] = wg_ref[...].astype(BF16)
        wub_ref[...] = wu_ref[...].astype(BF16)
        wdb_ref[...] = wd_ref[...].astype(BF16)

    x = x_ref[...]
    g = jnp.dot(x, wgb_ref[...], preferred_element_type=F32)
    u = jnp.dot(x, wub_ref[...], preferred_element_type=F32)
    o_ref[...] = (g * jax.nn.sigmoid(g) * u).astype(o_ref.dtype)


def ffn_gate_up(xb, wg_stack, wu_stack, wd_stack, layer, *, tm, tf):
    M, K = xb.shape
    F = wg_stack.shape[2]
    N = wd_stack.shape[2]
    blk = tm * K * 2 + 2 * K * tf * 4 + tm * tf * 2 + tf * N * 4 + tf * N * 2
    w_spec = pl.BlockSpec((None, K, tf), lambda j, i: (layer, 0, j))
    return pl.pallas_call(
        _ffn_gu_kernel,
        grid=(F // tf, M // tm),
        in_specs=[pl.BlockSpec((tm, K), lambda j, i: (i, 0)), w_spec, w_spec,
                  pl.BlockSpec((None, tf, N), lambda j, i: (layer, j, 0))],
        out_specs=[pl.BlockSpec((tm, tf), lambda j, i: (i, j)),
                   pl.BlockSpec((tf, N), lambda j, i: (j, 0))],
        out_shape=[jax.ShapeDtypeStruct((M, F), BF16),
                   jax.ShapeDtypeStruct((F, N), BF16)],
        scratch_shapes=[pltpu.VMEM((K, tf), BF16), pltpu.VMEM((K, tf), BF16)],
        compiler_params=_params(
            _vmem_limit(blk, 2 * K * tf * 2, temp_bytes=4 * tm * tf * 4), 2),
        name="ffn_gate_up",
    )(xb, wg_stack, wu_stack, wd_stack)


def _rope_lanes(r, cos_t, sin_a, sin_b):
    return (r * cos_t + pltpu.roll(r, LANES - MLA_ROPE // 2, 1) * sin_a
            + pltpu.roll(r, MLA_ROPE // 2, 1) * sin_b)


def _rms(x, g):
    return x * lax.rsqrt(jnp.mean(x * x, axis=-1, keepdims=True) + RMS_EPS) * g


def _mla_in_kernel(x_ref, w_ref, gq_ref, gkv_ref, cos_ref, sa_ref, sb_ref,
                   cq_ref, ckv_ref, kr_ref):
    h = jnp.dot(x_ref[...], w_ref[...], preferred_element_type=F32)
    cq_ref[...] = _rms(h[:, :MLA_Q_RANK], gq_ref[...]).astype(BF16)
    ckv_ref[...] = _rms(h[:, MLA_Q_RANK:MLA_Q_RANK + MLA_KV_RANK], gkv_ref[...]).astype(BF16)
    r = h[:, MLA_Q_RANK + MLA_KV_RANK:]
    kr_ref[...] = _rope_lanes(r, cos_ref[...], sa_ref[...], sb_ref[...]).astype(BF16)


def mla_in(xb, w_pad, gq, gkv, cos_t, sin_a, sin_b, *, tm=512):
    S, K = xb.shape
    N = w_pad.shape[1]
    blk = (tm * K * 2 + K * N * 2 + 3 * tm * LANES * 4
           + tm * (MLA_Q_RANK + MLA_KV_RANK + LANES) * 2)
    row = lambda i: (i, 0)
    fixed = lambda i: (0, 0)
    return pl.pallas_call(
        _mla_in_kernel,
        grid=(S // tm,),
        in_specs=[pl.BlockSpec((tm, K), row), pl.BlockSpec((K, N), fixed),
                  pl.BlockSpec((1, MLA_Q_RANK), fixed), pl.BlockSpec((1, MLA_KV_RANK), fixed),
                  pl.BlockSpec((tm, LANES), row), pl.BlockSpec((tm, LANES), row),
                  pl.BlockSpec((tm, LANES), row)],
        out_specs=[pl.BlockSpec((tm, MLA_Q_RANK), row), pl.BlockSpec((tm, MLA_KV_RANK), row),
                   pl.BlockSpec((tm, LANES), row)],
        out_shape=[jax.ShapeDtypeStruct((S, MLA_Q_RANK), BF16),
                   jax.ShapeDtypeStruct((S, MLA_KV_RANK), BF16),
                   jax.ShapeDtypeStruct((S, LANES), BF16)],
        compiler_params=_params(_vmem_limit(blk, temp_bytes=3 * tm * N * 4), 1),
        name="mla_in",
    )(xb, w_pad, gq, gkv, cos_t, sin_a, sin_b)


def _mla_q_kernel(cq_ref, w_ref, cos_ref, sa_ref, sb_ref, q_ref, *, qscale):
    acc = jnp.dot(cq_ref[...], w_ref[...], preferred_element_type=F32) * qscale
    cos_t, sin_a, sin_b = cos_ref[...], sa_ref[...], sb_ref[...]
    for hh in range(q_ref.shape[0]):
        base = hh * 2 * LANES
        q_ref[hh, :, 0:LANES] = acc[:, base:base + LANES].astype(BF16)
        r = acc[:, base + LANES:base + 2 * LANES]
        q_ref[hh, :, LANES:2 * LANES] = _rope_lanes(r, cos_t, sin_a, sin_b).astype(BF16)


def mla_q(cq, w_pad, cos_t, sin_a, sin_b, *, qscale, tm=512):
    S, K = cq.shape
    N = w_pad.shape[1]
    nh = N // (2 * LANES)
    blk = tm * K * 2 + K * N * 2 + 3 * tm * LANES * 4 + tm * N * 2
    row = lambda i: (i, 0)
    return pl.pallas_call(
        functools.partial(_mla_q_kernel, qscale=qscale),
        grid=(S // tm,),
        in_specs=[pl.BlockSpec((tm, K), row), pl.BlockSpec((K, N), lambda i: (0, 0)),
                  pl.BlockSpec((tm, LANES), row), pl.BlockSpec((tm, LANES), row),
                  pl.BlockSpec((tm, LANES), row)],
        out_specs=pl.BlockSpec((nh, tm, 2 * LANES), lambda i: (0, i, 0)),
        out_shape=jax.ShapeDtypeStruct((nh, S, 2 * LANES), BF16),
        compiler_params=_params(_vmem_limit(blk, temp_bytes=2 * tm * N * 4), 1),
        name="mla_q",
    )(cq, w_pad, cos_t, sin_a, sin_b)


def _mla_kv_kernel(ckv_ref, w_ref, kr_ref, k_ref, v_ref):
    acc = jnp.dot(ckv_ref[...], w_ref[...], preferred_element_type=F32)
    kr = kr_ref[...]
    for hh in range(k_ref.shape[0]):
        base = hh * (MLA_NOPE + MLA_V)
        k_ref[hh, :, 0:LANES] = acc[:, base:base + MLA_NOPE].astype(BF16)
        k_ref[hh, :, LANES:2 * LANES] = kr
        v_ref[hh] = acc[:, base + MLA_NOPE:base + MLA_NOPE + MLA_V].astype(BF16)


def mla_kv(ckv, w, kr, *, tm=512):
    S, K = ckv.shape
    N = w.shape[1]
    nh = N // (MLA_NOPE + MLA_V)
    blk = tm * K * 2 + K * N * 2 + tm * LANES * 2 + nh * tm * (2 * LANES + MLA_V) * 2
    row = lambda i: (i, 0)
    return pl.pallas_call(
        _mla_kv_kernel,
        grid=(S // tm,),
        in_specs=[pl.BlockSpec((tm, K), row), pl.BlockSpec((K, N), lambda i: (0, 0)),
                  pl.BlockSpec((tm, LANES), row)],
        out_specs=[pl.BlockSpec((nh, tm, 2 * LANES), lambda i: (0, i, 0)),
                   pl.BlockSpec((nh, tm, MLA_V), lambda i: (0, i, 0))],
        out_shape=[jax.ShapeDtypeStruct((nh, S, 2 * LANES), BF16),
                   jax.ShapeDtypeStruct((nh, S, MLA_V), BF16)],
        compiler_params=_params(_vmem_limit(blk, temp_bytes=2 * tm * N * 4), 1),
        name="mla_kv",
    )(ckv, w, kr)


def _rope_lane_tables(S):
    half = MLA_ROPE // 2
    inv = ROPE_THETA ** (-np.arange(0, MLA_ROPE, 2, dtype=np.float64) / MLA_ROPE)
    ang = np.arange(S, dtype=np.float64)[:, None] * inv[None, :]
    cos, sin = np.cos(ang), np.sin(ang)
    z = np.zeros((S, half))
    z2 = np.zeros((S, LANES - MLA_ROPE))
    cos_t = np.concatenate([cos, cos, z2], axis=1)
    sin_a = np.concatenate([-sin, z, z2], axis=1)
    sin_b = np.concatenate([z, sin, z2], axis=1)
    return tuple(jnp.asarray(a.astype(np.float32)) for a in (cos_t, sin_a, sin_b))


def _pad_cols(w, n):
    return jnp.pad(w, ((0, 0), (0, n - w.shape[1])))


def kernel(x, ab_w_in, ab_forget_bias, ab_w_out, rel_bias, mla_w_in, mla_q_norm,
           mla_kv_norm, mla_w_uq, mla_w_ukv, mla_w_out, ffn_w_gate, ffn_w_up,
           ffn_w_down, ln_g, ln_b):
    S = x.shape[1]
    xf = x.reshape(S, D_MODEL)
    xb = xf.astype(BF16)
    cos_t, sin_a, sin_b = _rope_lane_tables(S)
    nf = 3 * DA + N_HEADS_A
    qs = HEAD_DIM ** -0.5 * LOG2E
    colscale = np.ones((1, 3 * DA + 3 * DB), np.float32)
    colscale[:, :DA] = qs
    colscale[:, 3 * DA:3 * DA + DB] = qs
    colscale = jnp.asarray(colscale)
    causal_b3 = _causal_bias(ATTN_TILE)
    moba_b3 = moba_bias(rel_bias, n_heads=N_HEADS_B, t=ATTN_TILE)
    moba_kaug = _moba_key_onehot(S)
    hb = 3 * N_HEADS_A
    ab_w_out_b = ab_w_out.astype(BF16)
    mla_w_out_b = mla_w_out.astype(BF16)

    for layer in range(DEPTH):
        j = layer // 2
        g0, b0 = ln_g[layer, 0][None, :], ln_b[layer, 0][None, :]
        g1, b1 = ln_g[layer, 1][None, :], ln_b[layer, 1][None, :]
        if layer % 2 == 0:
            w_in = ab_w_in[j]
            w_qkv = jnp.concatenate([w_in[:, :3 * DA], w_in[:, nf:]], axis=1).astype(BF16)
            w_f = _pad_cols(w_in[:, 3 * DA:nf], LANES).astype(BF16)
            b_f = _pad_cols(ab_forget_bias[j][None, :], LANES)
            heads = mm_heads(xb, w_qkv, colscale, tm=1024, tn=1024)
            qaug, kaug = fox_gate(xb, w_f, b_f)
            ya = flash_attention([(heads, 0), (qaug, 0)], [(heads, N_HEADS_A), (kaug, 0)],
                                 (heads, 2 * N_HEADS_A), causal_b3,
                                 n_heads=N_HEADS_A, name="flash_fox")
            qaug_b = moba_gate(heads, rel_bias, n_heads=N_HEADS_B, q_off=hb,
                               k_off=hb + N_HEADS_B)
            yb = flash_attention([(heads, hb), (qaug_b, 0)],
                                 [(heads, hb + N_HEADS_B), (moba_kaug, None)],
                                 (heads, hb + 2 * N_HEADS_B), moba_b3,
                                 n_heads=N_HEADS_B, name="flash_moba", prev_biased=True)
            y, w_out = [ya, yb], ab_w_out_b
        else:
            w_in = _pad_cols(mla_w_in[j], MLA_Q_RANK + MLA_KV_RANK + LANES).astype(BF16)
            cq, ckv, kr = mla_in(xb, w_in, mla_q_norm[j][None, :], mla_kv_norm[j][None, :],
                                 cos_t, sin_a, sin_b)
            w_uq = mla_w_uq[j].reshape(MLA_Q_RANK, MLA_HEADS, MLA_NOPE + MLA_ROPE)
            w_uq = jnp.pad(w_uq, ((0, 0), (0, 0), (0, 2 * LANES - MLA_NOPE - MLA_ROPE)))
            w_uq = w_uq.reshape(MLA_Q_RANK, MLA_HEADS * 2 * LANES).astype(BF16)
            q_full = mla_q(cq, w_uq, cos_t, sin_a, sin_b,
                           qscale=(MLA_NOPE + MLA_ROPE) ** -0.5 * LOG2E)
            k_full, v = mla_kv(ckv, mla_w_ukv[j].astype(BF16), kr)
            y = [flash_attention([(q_full, 0)], [(k_full, 0)], (v, 0), causal_b3,
                                 n_heads=MLA_HEADS, name="flash_mla")]
            w_out = mla_w_out_b
        xf, xb = mm_res_ln(y, w_out, j, xf, g0, b0, tm=512)
        hmid, w_down = ffn_gate_up(xb, ffn_w_gate, ffn_w_up, ffn_w_down, layer, tm=1024, tf=512)
        xf, xb = mm_res_ln([hmid], w_down[None], 0, xf, g1, b1, tm=256)
    return xf.reshape(1, S, D_MODEL)
```

```python
import functools
import math

import numpy as np
import jax
import jax.numpy as jnp
from jax import lax
from jax.experimental import pallas as pl
from jax.experimental.pallas import tpu as pltpu

F32 = jnp.float32
BF16 = jnp.bfloat16

D_MODEL = 2048
DEPTH = 4
HEAD_DIM = 128
N_HEADS_A = 8
N_HEADS_B = 8
MOBA_BLOCK = 256
MOBA_TOPK = 3
N_BUCKETS = 32
MAX_DISTANCE = 128
MLA_HEADS = 16
MLA_Q_RANK = 512
MLA_KV_RANK = 512
MLA_NOPE = 128
MLA_ROPE = 64
MLA_V = 128
ROPE_THETA = 10000.0
DEEPNORM_ALPHA = (2 * DEPTH) ** 0.25
DA = N_HEADS_A * HEAD_DIM
DB = N_HEADS_B * HEAD_DIM
LN_EPS = 1e-5
RMS_EPS = 1e-6
LOG2E = math.log2(math.e)

LANES = 128
VMEM_BUDGET_BYTES = 56 * 2**20
MASK_VALUE = -1e30
ATTN_TILE = 2 * MOBA_BLOCK


def _vmem_limit(block_bytes, scratch_bytes=0, temp_bytes=0):
    est = 2 * block_bytes + scratch_bytes + temp_bytes + (4 << 20)
    return int(min(max(est, 16 << 20), VMEM_BUDGET_BYTES))


def _params(vmem_bytes, ngrid, flags=None):
    return pltpu.CompilerParams(
        dimension_semantics=("arbitrary",) * ngrid, vmem_limit_bytes=vmem_bytes, flags=flags)


def _split3(x):
    hi = x.astype(BF16).astype(F32)
    r1 = x - hi
    lo = r1.astype(BF16).astype(F32)
    return hi, lo, r1 - lo


def _mm_heads_kernel(x_ref, w_ref, cs_ref, o_ref, *, width):
    acc = jnp.dot(x_ref[...], w_ref[...], preferred_element_type=F32) * cs_ref[...]
    for hh in range(o_ref.shape[0]):
        o_ref[hh] = acc[:, hh * width:(hh + 1) * width].astype(o_ref.dtype)


def mm_heads(x, w, colscale, *, tm, tn, width=HEAD_DIM):
    M, K = x.shape
    N = w.shape[1]
    nh = tn // width
    blk = tm * K * 2 + K * tn * 2 + tm * tn * 2 + tn * 4
    return pl.pallas_call(
        functools.partial(_mm_heads_kernel, width=width),
        grid=(M // tm, N // tn),
        in_specs=[pl.BlockSpec((tm, K), lambda i, j: (i, 0)),
                  pl.BlockSpec((K, tn), lambda i, j: (0, j)),
                  pl.BlockSpec((1, tn), lambda i, j: (0, j))],
        out_specs=pl.BlockSpec((nh, tm, width), lambda i, j: (j, i, 0)),
        out_shape=jax.ShapeDtypeStruct((N // width, M, width), BF16),
        compiler_params=_params(_vmem_limit(blk, temp_bytes=2 * tm * tn * 4), 2),
        name="mm_heads",
    )(x, w, colscale)


def _fox_gate_kernel(x_ref, wf_ref, bf_ref, selq_ref, selk_ref, cq_ref, ck_ref,
                     qaug_ref, kaug_ref, carry_ref, *, tb, nh):
    @pl.when(pl.program_id(0) == 0)
    def _():
        carry_ref[...] = jnp.zeros_like(carry_ref)

    z = jnp.dot(x_ref[...], wf_ref[...], preferred_element_type=F32) + bf_ref[...]
    lf = jnp.minimum(z, 0.0) - jnp.log1p(jnp.exp(-jnp.abs(z)))
    row = lax.broadcasted_iota(jnp.int32, (tb, tb), 0)
    col = lax.broadcasted_iota(jnp.int32, (tb, tb), 1)
    tri = jnp.where(row >= col, 1.0, 0.0).astype(BF16)
    lf_terms = jnp.concatenate(_split3(lf), axis=1).astype(BF16)
    part = jnp.dot(tri, lf_terms, preferred_element_type=F32)
    cs = (part[:, :LANES] + part[:, LANES:2 * LANES] + part[:, 2 * LANES:]) + carry_ref[...]
    carry_ref[...] = cs[tb - 1:tb, :]
    lane = lax.broadcasted_iota(jnp.int32, (tb, LANES), 1)
    hi, lo, lo2 = [jnp.where(lane < nh, term, 0.0) for term in _split3(cs * LOG2E)]
    terms = (hi + pltpu.roll(lo, nh, 1) + pltpu.roll(lo2, 2 * nh, 1)).astype(BF16)
    qa = jnp.dot(terms, selq_ref[...], preferred_element_type=F32) + cq_ref[...]
    ka = jnp.dot(terms, selk_ref[...], preferred_element_type=F32) + ck_ref[...]
    for h in range(nh):
        qaug_ref[h] = qa[:, h * LANES:(h + 1) * LANES].astype(BF16)
        kaug_ref[h] = ka[:, h * LANES:(h + 1) * LANES].astype(BF16)


def _fox_selectors(nh):
    selq = np.zeros((LANES, nh * LANES), np.float32)
    selk = np.zeros((LANES, nh * LANES), np.float32)
    cq = np.zeros((1, nh * LANES), np.float32)
    ck = np.zeros((1, nh * LANES), np.float32)
    for h in range(nh):
        for term in range(3):
            selq[term * nh + h, h * LANES + term] = 1.0
            selk[term * nh + h, h * LANES + 3 + term] = -1.0
        cq[0, h * LANES + 3:h * LANES + 6] = 1.0
        ck[0, h * LANES:h * LANES + 3] = 1.0
    return (jnp.asarray(selq, dtype=BF16), jnp.asarray(selk, dtype=BF16),
            jnp.asarray(cq), jnp.asarray(ck))


def fox_gate(xb, wf_pad, bf_pad, *, tb=512, nh=N_HEADS_A):
    S, K = xb.shape
    selq, selk, cq, ck = _fox_selectors(nh)
    blk = (tb * K * 2 + K * LANES * 2 + 2 * nh * tb * LANES * 2
           + 2 * LANES * nh * LANES * 2)
    aug = jax.ShapeDtypeStruct((nh, S, LANES), BF16)
    fixed = lambda i: (0, 0)
    return pl.pallas_call(
        functools.partial(_fox_gate_kernel, tb=tb, nh=nh),
        grid=(S // tb,),
        in_specs=[pl.BlockSpec((tb, K), lambda i: (i, 0)),
                  pl.BlockSpec((K, LANES), fixed),
                  pl.BlockSpec((1, LANES), fixed),
                  pl.BlockSpec(selq.shape, fixed), pl.BlockSpec(selk.shape, fixed),
                  pl.BlockSpec(cq.shape, fixed), pl.BlockSpec(ck.shape, fixed)],
        out_specs=[pl.BlockSpec((nh, tb, LANES), lambda i: (0, i, 0)),
                   pl.BlockSpec((nh, tb, LANES), lambda i: (0, i, 0))],
        out_shape=[aug, aug],
        scratch_shapes=[pltpu.VMEM((1, LANES), F32)],
        compiler_params=_params(
            _vmem_limit(blk, temp_bytes=4 * tb * tb * 4 + 4 * tb * nh * LANES * 4), 1),
        name="fox_gate",
    )(xb, wf_pad, bf_pad, selq, selk, cq, ck)


def _moba_gate_kernel(rb_ref, q_ref, k_ref, qaug_ref, kmean_ref, *, nb, tr):
    B = MOBA_BLOCK
    h = pl.program_id(0)
    i = pl.program_id(1)
    far_bias = rb_ref[N_BUCKETS - 1, h] * LOG2E

    @pl.when(i == 0)
    def _():
        kf = k_ref[0].astype(F32).reshape(nb, B, HEAD_DIM)
        kmean_ref[...] = jnp.sum(kf, axis=1) * (1.0 / B)

    blk = lax.broadcasted_iota(jnp.int32, (nb, tr), 0)
    blk_f = blk.astype(F32)
    own = i * (tr // B) + lax.broadcasted_iota(jnp.int32, (nb, tr), 1) // B
    g = lax.dot_general(kmean_ref[...], q_ref[0].astype(F32), (((1,), (1,)), ((), ())),
                        preferred_element_type=F32, precision=lax.Precision.HIGHEST)
    neg_inf = -jnp.inf
    g = jnp.where(blk < own, g, neg_inf)
    mb = jnp.where(blk == own, 0.0, MASK_VALUE).astype(F32)
    for _ in range(MOBA_TOPK):
        mx = jnp.max(g, axis=0, keepdims=True)
        hit = jnp.logical_and(g == mx, mx > neg_inf)
        idx = jnp.min(jnp.where(hit, blk_f, float(nb)), axis=0, keepdims=True)
        pick = blk_f == idx
        mb = jnp.where(pick, far_bias, mb)
        g = jnp.where(pick, neg_inf, g)
    mb_hi = mb.astype(BF16).astype(F32)
    mb_lo = jnp.where(mb > 0.5 * MASK_VALUE, mb - mb_hi, 0.0)
    aug_t = jnp.concatenate([mb_hi, mb_lo, jnp.zeros((LANES - 2 * nb, tr), F32)], axis=0)
    qaug_ref[0] = jnp.transpose(aug_t).astype(BF16)


def moba_gate(heads, rel_bias, *, n_heads, q_off, k_off, tr=1024):
    S = heads.shape[1]
    nb = S // MOBA_BLOCK
    assert 2 * nb <= LANES
    blk = tr * HEAD_DIM * 2 + S * HEAD_DIM * 2 + tr * LANES * 2
    return pl.pallas_call(
        functools.partial(_moba_gate_kernel, nb=nb, tr=tr),
        grid=(n_heads, S // tr),
        in_specs=[pl.BlockSpec(memory_space=pltpu.SMEM),
                  pl.BlockSpec((1, tr, HEAD_DIM), lambda h, i: (h + q_off, i, 0)),
                  pl.BlockSpec((1, S, HEAD_DIM), lambda h, i: (h + k_off, 0, 0))],
        out_specs=pl.BlockSpec((1, tr, LANES), lambda h, i: (h, i, 0)),
        out_shape=jax.ShapeDtypeStruct((n_heads, S, LANES), BF16),
        scratch_shapes=[pltpu.VMEM((nb, HEAD_DIM), F32)],
        compiler_params=_params(
            _vmem_limit(blk, nb * HEAD_DIM * 4,
                        temp_bytes=S * HEAD_DIM * 4 + 12 * tr * LANES * 4), 2),
        name="moba_gate",
    )(rel_bias, heads, heads)


def _moba_bias_kernel(rb_ref, own_ref, prev_ref, o_ref):
    B = MOBA_BLOCK
    h = pl.program_id(0)
    far = rb_ref[N_BUCKETS - 1, h]
    own_bucket = own_ref[...]
    prev_bucket = prev_ref[...]
    own = jnp.zeros((B, B), F32)
    prev = jnp.zeros((B, B), F32)
    for b in range(N_BUCKETS):
        val = rb_ref[b, h]
        own = jnp.where(own_bucket == b, val, own)
        prev = jnp.where(prev_bucket == b, val, prev)
    own = jnp.where(own_bucket < 0, MASK_VALUE, own * LOG2E)
    prev = (prev - far) * LOG2E
    o_ref[0] = jnp.zeros(o_ref.shape[1:], F32)
    o_ref[0, 0:B, B:2 * B] = prev
    o_ref[0, 0:B, 2 * B:3 * B] = own
    o_ref[0, B:2 * B, 2 * B:3 * B] = prev
    o_ref[0, B:2 * B, 3 * B:4 * B] = own


def _t5_bucket_table(n):
    rel = np.arange(n)
    max_exact = N_BUCKETS // 2
    nf = np.maximum(rel, 1).astype(np.float32)
    large = max_exact + (np.log(nf / np.float32(max_exact))
                         / np.float32(math.log(MAX_DISTANCE / max_exact))
                         * np.float32(N_BUCKETS - max_exact)).astype(np.int32)
    large = np.minimum(large, N_BUCKETS - 1)
    return np.where(rel < max_exact, rel, large).astype(np.int32)


def moba_bias(rel_bias, *, n_heads, t):
    B = MOBA_BLOCK
    assert t == 2 * B
    table = _t5_bucket_table(2 * B)
    rel = np.arange(B)[:, None] - np.arange(B)[None, :]
    own_bucket = np.where(rel >= 0, table[np.maximum(rel, 0)], -1).astype(np.int32)
    prev_bucket = table[rel + B].astype(np.int32)
    blk = t * 2 * t * 4 + 2 * B * B * 4
    return pl.pallas_call(
        _moba_bias_kernel,
        grid=(n_heads,),
        in_specs=[pl.BlockSpec(memory_space=pltpu.SMEM),
                  pl.BlockSpec((B, B), lambda h: (0, 0)),
                  pl.BlockSpec((B, B), lambda h: (0, 0))],
        out_specs=pl.BlockSpec((1, t, 2 * t), lambda h: (h, 0, 0)),
        out_shape=jax.ShapeDtypeStruct((n_heads, t, 2 * t), F32),
        compiler_params=_params(_vmem_limit(blk, temp_bytes=8 * B * B * 4), 1),
        name="moba_bias",
    )(rel_bias, jnp.asarray(own_bucket), jnp.asarray(prev_bucket))


def _causal_bias(t):
    rows = np.arange(t)[:, None]
    cols = np.arange(2 * t)[None, :] - t
    return jnp.asarray(np.where(cols <= rows, 0.0, MASK_VALUE).astype(np.float32)[None])


FINAL_TILES = 4
def _ones_column(rows):
    lane = lax.broadcasted_iota(jnp.int32, (rows, LANES), 1)
    return jnp.where(lane == 0, 1.0, 0.0).astype(BF16)


def _qk(q, k):
    return lax.dot_general(q, k, (((1,), (1,)), ((), ())), preferred_element_type=F32)


def _flash_kernel(*refs, t, nparts, nsplit, group, prev_biased):
    q_refs = refs[:nparts]
    k_refs = refs[nparts:2 * nparts]
    v_ref, b2_ref, o_ref, m_ref, acc_ref = refs[2 * nparts:]
    qi = pl.program_id(1)
    dv = v_ref.shape[-1]
    r = t // nsplit
    chains = [(g, c) for g in range(group) for c in range(nsplit)]

    def head(ref, g):
        return g if ref.shape[0] == group else 0

    qs = [jnp.concatenate([ref[head(ref, g), c * r:(c + 1) * r, :] for ref in q_refs], axis=1)
          for g, c in chains]
    m_ref[...] = jnp.full_like(m_ref, MASK_VALUE)
    acc_ref[...] = jnp.zeros_like(acc_ref)

    def step(tile, width, n_biased=0):
        ks = pl.multiple_of(tile * t, t)
        ones = _ones_column(width)
        bw = n_biased * t
        ss = []
        for n, (g, c) in enumerate(chains):
            k = jnp.concatenate([ref[head(ref, g), pl.ds(ks, width), :] for ref in k_refs],
                                axis=1)
            s = _qk(qs[n], k)
            if bw:
                bias = b2_ref[head(b2_ref, g), c * r:(c + 1) * r, 2 * t - bw:2 * t]
                tail = s[:, width - bw:] + bias
                s = tail if bw == width else jnp.concatenate([s[:, :width - bw], tail], axis=1)
            ss.append(s)
        for n, (g, c) in enumerate(chains):
            v1 = jnp.concatenate([v_ref[g, pl.ds(ks, width), :], ones], axis=1)
            m = m_ref[n]
            m_new = jnp.maximum(m, jnp.max(ss[n], axis=1, keepdims=True))
            p = jnp.exp2(ss[n] - jnp.concatenate([m_new] * (width // LANES), axis=1))
            a = jnp.exp2(m - m_new)
            acc_ref[n] = (jnp.concatenate([a] * ((dv + LANES) // LANES), axis=1) * acc_ref[n]
                          + jnp.dot(p.astype(BF16), v1, preferred_element_type=F32))
            m_ref[n] = m_new

    n_biased = jnp.minimum(2 if prev_biased else 1, qi + 1)
    lead = qi + 1 - n_biased
    quads = lead // FINAL_TILES

    def body(kb, carry):
        step(FINAL_TILES * kb, FINAL_TILES * t)
        return carry

    lax.fori_loop(0, quads, body, 0)
    start = quads * FINAL_TILES
    rest = qi + 1 - start
    if prev_biased:
        @pl.when(rest > FINAL_TILES)
        def _():
            step(start, t)
    spill = (rest > FINAL_TILES).astype(jnp.int32)
    for width_tiles in range(1, FINAL_TILES + 1):
        @pl.when(rest - spill == width_tiles)
        def _(width_tiles=width_tiles):
            step(start + spill, width_tiles * t,
                 n_biased=min(2 if prev_biased else 1, width_tiles))
    for g in range(group):
        out = []
        for c in range(nsplit):
            acc = acc_ref[g * nsplit + c]
            out.append(acc[:, :dv] * (1.0 / acc[:, dv:dv + 1]))
        o_ref[:, g * dv:(g + 1) * dv] = jnp.concatenate(out, axis=0).astype(o_ref.dtype)


def flash_attention(q_parts, k_parts, v_part, b2, *, n_heads, name, t=ATTN_TILE, nsplit=1,
                    group=2, prev_biased=False):
    v, v_off = v_part
    S, dv = v.shape[1], v.shape[2]

    def spec(arr, off, rows):
        g = group if off is not None and arr.shape[0] > 1 else 1
        assert off is None or off % group == 0
        blk_shape = (g, t if rows else arr.shape[1], arr.shape[2])
        if g == 1:
            index = lambda h, i: (0, i if rows else 0, 0)
        else:
            index = lambda h, i: (h + off // group, i if rows else 0, 0)
        return pl.BlockSpec(blk_shape, index), math.prod(blk_shape) * arr.dtype.itemsize

    operands = ([(a, o, True) for a, o in q_parts] + [(a, o, False) for a, o in k_parts]
                + [(v, v_off, False), (b2, 0 if b2.shape[0] > 1 else None, False)])
    in_specs, blk = [], t * group * dv * 2
    for arr, off, rows in operands:
        s, nbytes = spec(arr, off, rows)
        in_specs.append(s)
        blk += nbytes
    nchain = group * nsplit
    r = t // nsplit
    return pl.pallas_call(
        functools.partial(_flash_kernel, t=t, nparts=len(q_parts), nsplit=nsplit, group=group,
                          prev_biased=prev_biased),
        grid=(n_heads // group, S // t),
        in_specs=in_specs,
        out_specs=pl.BlockSpec((t, group * dv), lambda h, i: (i, h)),
        out_shape=jax.ShapeDtypeStruct((S, n_heads * dv), BF16),
        scratch_shapes=[pltpu.VMEM((nchain, r, LANES), F32),
                        pltpu.VMEM((nchain, r, dv + LANES), F32)],
        compiler_params=_params(
            _vmem_limit(blk, nchain * r * (dv + 2 * LANES) * 4,
                        temp_bytes=6 * group * t * t * 4), 2),
        name=name,
    )(*[arr for arr, _, _ in operands])


def _moba_key_onehot(S):
    nb = S // MOBA_BLOCK
    lanes = np.arange(LANES)[None, :]
    blk_of = (np.arange(S) // MOBA_BLOCK)[:, None]
    onehot = (lanes % nb == blk_of) & (lanes < 2 * nb)
    return jnp.asarray(onehot.astype(np.float32)[None], dtype=BF16)


def _mm_res_ln_kernel(*refs, n_a, chunks):
    a_refs = refs[:n_a]
    w_ref, x_ref, g_ref, b_ref, o_ref, obf_ref = refs[n_a:]
    rc = x_ref.shape[0] // chunks
    for c in range(chunks):
        rows = slice(c * rc, (c + 1) * rc)
        y, row = None, 0
        for a_ref in a_refs:
            ka = a_ref.shape[1]
            d = jnp.dot(a_ref[rows, :], w_ref[row:row + ka, :], preferred_element_type=F32)
            y = d if y is None else y + d
            row += ka
        z = DEEPNORM_ALPHA * x_ref[rows, :] + y
        mu = jnp.mean(z, axis=-1, keepdims=True)
        zc = z - mu
        var = jnp.mean(zc * zc, axis=-1, keepdims=True)
        out = zc * lax.rsqrt(var + LN_EPS) * g_ref[...] + b_ref[...]
        o_ref[rows, :] = out
        obf_ref[rows, :] = out.astype(BF16)


def mm_res_ln(a_list, w_stack, layer, x, g, b, *, tm, chunks=2):
    M = a_list[0].shape[0]
    K = sum(a.shape[1] for a in a_list)
    N = w_stack.shape[2]
    blk = tm * K * 2 + tm * N * 4 + 2 * N * 4 + tm * N * 4 + tm * N * 2
    return pl.pallas_call(
        functools.partial(_mm_res_ln_kernel, n_a=len(a_list), chunks=chunks),
        grid=(M // tm,),
        in_specs=[pl.BlockSpec((tm, a.shape[1]), lambda i: (i, 0)) for a in a_list] + [
            pl.BlockSpec((None, K, N), lambda i: (layer, 0, 0), pipeline_mode=pl.Buffered(1)),
            pl.BlockSpec((tm, N), lambda i: (i, 0)),
            pl.BlockSpec((1, N), lambda i: (0, 0)),
            pl.BlockSpec((1, N), lambda i: (0, 0))],
        out_specs=[pl.BlockSpec((tm, N), lambda i: (i, 0)),
                   pl.BlockSpec((tm, N), lambda i: (i, 0))],
        out_shape=[jax.ShapeDtypeStruct((M, N), F32),
                   jax.ShapeDtypeStruct((M, N), BF16)],
        compiler_params=_params(_vmem_limit(blk, K * N * 2, temp_bytes=3 * tm * N * 4), 1),
        name="mm_res_ln",
    )(*a_list, w_stack, x, g, b)


def _ffn_gu_kernel(x_ref, wg_ref, wu_ref, o_ref, wgb_ref, wub_ref):
    @pl.when(pl.program_id(1) == 0)
    def _():
        wgb_ref[...] = wg_ref[...].astype(BF16)
        wub_ref[...] = wu_ref[...].astype(BF16)

    x = x_ref[...]
    g = jnp.dot(x, wgb_ref[...], preferred_element_type=F32)
    u = jnp.dot(x, wub_ref[...], preferred_element_type=F32)
    o_ref[...] = (g * jax.nn.sigmoid(g) * u).astype(o_ref.dtype)


def ffn_gate_up(xb, wg_stack, wu_stack, layer, *, tm, tf):
    M, K = xb.shape
    F = wg_stack.shape[2]
    blk = tm * K * 2 + 2 * K * tf * 4 + tm * tf * 2
    w_spec = pl.BlockSpec((None, K, tf), lambda j, i: (layer, 0, j))
    return pl.pallas_call(
        _ffn_gu_kernel,
        grid=(F // tf, M // tm),
        in_specs=[pl.BlockSpec((tm, K), lambda j, i: (i, 0)), w_spec, w_spec],
        out_specs=pl.BlockSpec((tm, tf), lambda j, i: (i, j)),
        out_shape=jax.ShapeDtypeStruct((M, F), BF16),
        scratch_shapes=[pltpu.VMEM((K, tf), BF16), pltpu.VMEM((K, tf), BF16)],
        compiler_params=_params(
            _vmem_limit(blk, 2 * K * tf * 2, temp_bytes=4 * tm * tf * 4), 2),
        name="ffn_gate_up",
    )(xb, wg_stack, wu_stack)


def _rope_lanes(r, cos_t, sin_a, sin_b):
    return (r * cos_t + pltpu.roll(r, LANES - MLA_ROPE // 2, 1) * sin_a
            + pltpu.roll(r, MLA_ROPE // 2, 1) * sin_b)


def _rms(x, g):
    return x * lax.rsqrt(jnp.mean(x * x, axis=-1, keepdims=True) + RMS_EPS) * g


def _mla_in_kernel(x_ref, w_ref, gq_ref, gkv_ref, cos_ref, sa_ref, sb_ref,
                   cq_ref, ckv_ref, kr_ref):
    h = jnp.dot(x_ref[...], w_ref[...], preferred_element_type=F32)
    cq_ref[...] = _rms(h[:, :MLA_Q_RANK], gq_ref[...]).astype(BF16)
    ckv_ref[...] = _rms(h[:, MLA_Q_RANK:MLA_Q_RANK + MLA_KV_RANK], gkv_ref[...]).astype(BF16)
    r = h[:, MLA_Q_RANK + MLA_KV_RANK:]
    kr_ref[...] = _rope_lanes(r, cos_ref[...], sa_ref[...], sb_ref[...]).astype(BF16)


def mla_in(xb, w_pad, gq, gkv, cos_t, sin_a, sin_b, *, tm=512):
    S, K = xb.shape
    N = w_pad.shape[1]
    blk = (tm * K * 2 + K * N * 2 + 3 * tm * LANES * 4
           + tm * (MLA_Q_RANK + MLA_KV_RANK + LANES) * 2)
    row = lambda i: (i, 0)
    fixed = lambda i: (0, 0)
    return pl.pallas_call(
        _mla_in_kernel,
        grid=(S // tm,),
        in_specs=[pl.BlockSpec((tm, K), row), pl.BlockSpec((K, N), fixed),
                  pl.BlockSpec((1, MLA_Q_RANK), fixed), pl.BlockSpec((1, MLA_KV_RANK), fixed),
                  pl.BlockSpec((tm, LANES), row), pl.BlockSpec((tm, LANES), row),
                  pl.BlockSpec((tm, LANES), row)],
        out_specs=[pl.BlockSpec((tm, MLA_Q_RANK), row), pl.BlockSpec((tm, MLA_KV_RANK), row),
                   pl.BlockSpec((tm, LANES), row)],
        out_shape=[jax.ShapeDtypeStruct((S, MLA_Q_RANK), BF16),
                   jax.ShapeDtypeStruct((S, MLA_KV_RANK), BF16),
                   jax.ShapeDtypeStruct((S, LANES), BF16)],
        compiler_params=_params(_vmem_limit(blk, temp_bytes=3 * tm * N * 4), 1),
        name="mla_in",
    )(xb, w_pad, gq, gkv, cos_t, sin_a, sin_b)


def _mla_q_kernel(cq_ref, w_ref, cos_ref, sa_ref, sb_ref, q_ref, *, qscale):
    acc = jnp.dot(cq_ref[...], w_ref[...], preferred_element_type=F32) * qscale
    cos_t, sin_a, sin_b = cos_ref[...], sa_ref[...], sb_ref[...]
    for hh in range(q_ref.shape[0]):
        base = hh * 2 * LANES
        q_ref[hh, :, 0:LANES] = acc[:, base:base + LANES].astype(BF16)
        r = acc[:, base + LANES:base + 2 * LANES]
        q_ref[hh, :, LANES:2 * LANES] = _rope_lanes(r, cos_t, sin_a, sin_b).astype(BF16)


def mla_q(cq, w_pad, cos_t, sin_a, sin_b, *, qscale, tm=512):
    S, K = cq.shape
    N = w_pad.shape[1]
    nh = N // (2 * LANES)
    blk = tm * K * 2 + K * N * 2 + 3 * tm * LANES * 4 + tm * N * 2
    row = lambda i: (i, 0)
    return pl.pallas_call(
        functools.partial(_mla_q_kernel, qscale=qscale),
        grid=(S // tm,),
        in_specs=[pl.BlockSpec((tm, K), row), pl.BlockSpec((K, N), lambda i: (0, 0)),
                  pl.BlockSpec((tm, LANES), row), pl.BlockSpec((tm, LANES), row),
                  pl.BlockSpec((tm, LANES), row)],
        out_specs=pl.BlockSpec((nh, tm, 2 * LANES), lambda i: (0, i, 0)),
        out_shape=jax.ShapeDtypeStruct((nh, S, 2 * LANES), BF16),
        compiler_params=_params(_vmem_limit(blk, temp_bytes=2 * tm * N * 4), 1),
        name="mla_q",
    )(cq, w_pad, cos_t, sin_a, sin_b)


def _mla_kv_kernel(ckv_ref, w_ref, kr_ref, k_ref, v_ref):
    acc = jnp.dot(ckv_ref[...], w_ref[...], preferred_element_type=F32)
    kr = kr_ref[...]
    for hh in range(k_ref.shape[0]):
        base = hh * (MLA_NOPE + MLA_V)
        k_ref[hh, :, 0:LANES] = acc[:, base:base + MLA_NOPE].astype(BF16)
        k_ref[hh, :, LANES:2 * LANES] = kr
        v_ref[hh] = acc[:, base + MLA_NOPE:base + MLA_NOPE + MLA_V].astype(BF16)


def mla_kv(ckv, w, kr, *, tm=512):
    S, K = ckv.shape
    N = w.shape[1]
    nh = N // (MLA_NOPE + MLA_V)
    blk = tm * K * 2 + K * N * 2 + tm * LANES * 2 + nh * tm * (2 * LANES + MLA_V) * 2
    row = lambda i: (i, 0)
    return pl.pallas_call(
        _mla_kv_kernel,
        grid=(S // tm,),
        in_specs=[pl.BlockSpec((tm, K), row), pl.BlockSpec((K, N), lambda i: (0, 0)),
                  pl.BlockSpec((tm, LANES), row)],
        out_specs=[pl.BlockSpec((nh, tm, 2 * LANES), lambda i: (0, i, 0)),
                   pl.BlockSpec((nh, tm, MLA_V), lambda i: (0, i, 0))],
        out_shape=[jax.ShapeDtypeStruct((nh, S, 2 * LANES), BF16),
                   jax.ShapeDtypeStruct((nh, S, MLA_V), BF16)],
        compiler_params=_params(_vmem_limit(blk, temp_bytes=2 * tm * N * 4), 1),
        name="mla_kv",
    )(ckv, w, kr)


def _rope_lane_tables(S):
    half = MLA_ROPE // 2
    inv = ROPE_THETA ** (-np.arange(0, MLA_ROPE, 2, dtype=np.float64) / MLA_ROPE)
    ang = np.arange(S, dtype=np.float64)[:, None] * inv[None, :]
    cos, sin = np.cos(ang), np.sin(ang)
    z = np.zeros((S, half))
    z2 = np.zeros((S, LANES - MLA_ROPE))
    cos_t = np.concatenate([cos, cos, z2], axis=1)
    sin_a = np.concatenate([-sin, z, z2], axis=1)
    sin_b = np.concatenate([z, sin, z2], axis=1)
    return tuple(jnp.asarray(a.astype(np.float32)) for a in (cos_t, sin_a, sin_b))


def _pad_cols(w, n):
    return jnp.pad(w, ((0, 0), (0, n - w.shape[1])))


def kernel(x, ab_w_in, ab_forget_bias, ab_w_out, rel_bias, mla_w_in, mla_q_norm,
           mla_kv_norm, mla_w_uq, mla_w_ukv, mla_w_out, ffn_w_gate, ffn_w_up,
           ffn_w_down, ln_g, ln_b):
    S = x.shape[1]
    xf = x.reshape(S, D_MODEL)
    xb = xf.astype(BF16)
    cos_t, sin_a, sin_b = _rope_lane_tables(S)
    nf = 3 * DA + N_HEADS_A
    qs = HEAD_DIM ** -0.5 * LOG2E
    colscale = np.ones((1, 3 * DA + 3 * DB), np.float32)
    colscale[:, :DA] = qs
    colscale[:, 3 * DA:3 * DA + DB] = qs
    colscale = jnp.asarray(colscale)
    causal_b2 = _causal_bias(ATTN_TILE)
    moba_b2 = moba_bias(rel_bias, n_heads=N_HEADS_B, t=ATTN_TILE)
    moba_kaug = _moba_key_onehot(S)
    hb = 3 * N_HEADS_A
    ab_w_out_b = ab_w_out.astype(BF16)
    mla_w_out_b = mla_w_out.astype(BF16)
    ffn_w_down_b = ffn_w_down.astype(BF16)

    for layer in range(DEPTH):
        j = layer // 2
        g0, b0 = ln_g[layer, 0][None, :], ln_b[layer, 0][None, :]
        g1, b1 = ln_g[layer, 1][None, :], ln_b[layer, 1][None, :]
        if layer % 2 == 0:
            w_in = ab_w_in[j]
            w_qkv = jnp.concatenate([w_in[:, :3 * DA], w_in[:, nf:]], axis=1).astype(BF16)
            w_f = _pad_cols(w_in[:, 3 * DA:nf], LANES).astype(BF16)
            b_f = _pad_cols(ab_forget_bias[j][None, :], LANES)
            heads = mm_heads(xb, w_qkv, colscale, tm=1024, tn=1024)
            qaug, kaug = fox_gate(xb, w_f, b_f)
            ya = flash_attention([(heads, 0), (qaug, 0)], [(heads, N_HEADS_A), (kaug, 0)],
                                 (heads, 2 * N_HEADS_A), causal_b2,
                                 n_heads=N_HEADS_A, name="flash_fox")
            qaug_b = moba_gate(heads, rel_bias, n_heads=N_HEADS_B, q_off=hb,
                               k_off=hb + N_HEADS_B)
            yb = flash_attention([(heads, hb), (qaug_b, 0)],
                                 [(heads, hb + N_HEADS_B), (moba_kaug, None)],
                                 (heads, hb + 2 * N_HEADS_B), moba_b2,
                                 n_heads=N_HEADS_B, name="flash_moba", prev_biased=True)
            y, w_out = [ya, yb], ab_w_out_b
        else:
            w_in = _pad_cols(mla_w_in[j], MLA_Q_RANK + MLA_KV_RANK + LANES).astype(BF16)
            cq, ckv, kr = mla_in(xb, w_in, mla_q_norm[j][None, :], mla_kv_norm[j][None, :],
                                 cos_t, sin_a, sin_b)
            w_uq = mla_w_uq[j].reshape(MLA_Q_RANK, MLA_HEADS, MLA_NOPE + MLA_ROPE)
            w_uq = jnp.pad(w_uq, ((0, 0), (0, 0), (0, 2 * LANES - MLA_NOPE - MLA_ROPE)))
            w_uq = w_uq.reshape(MLA_Q_RANK, MLA_HEADS * 2 * LANES).astype(BF16)
            q_full = mla_q(cq, w_uq, cos_t, sin_a, sin_b,
                           qscale=(MLA_NOPE + MLA_ROPE) ** -0.5 * LOG2E)
            k_full, v = mla_kv(ckv, mla_w_ukv[j].astype(BF16), kr)
            y = [flash_attention([(q_full, 0)], [(k_full, 0)], (v, 0), causal_b2,
                                 n_heads=MLA_HEADS, name="flash_mla")]
            w_out = mla_w_out_b
        xf, xb = mm_res_ln(y, w_out, j, xf, g0, b0, tm=512)
        hmid = ffn_gate_up(xb, ffn_w_gate, ffn_w_up, layer, tm=1024, tf=512)
        xf, xb = mm_res_ln([hmid], ffn_w_down_b, layer, xf, g1, b1, tm=256)
    return xf.reshape(1, S, D_MODEL)
```

```python
import functools
import math

import numpy as np
import jax
import jax.numpy as jnp
from jax import lax
from jax.experimental import pallas as pl
from jax.experimental.pallas import tpu as pltpu

F32 = jnp.float32
BF16 = jnp.bfloat16

D_MODEL = 2048
DEPTH = 4
HEAD_DIM = 128
N_HEADS_A = 8
N_HEADS_B = 8
MOBA_BLOCK = 256
MOBA_TOPK = 3
N_BUCKETS = 32
MAX_DISTANCE = 128
MLA_HEADS = 16
MLA_Q_RANK = 512
MLA_KV_RANK = 512
MLA_NOPE = 128
MLA_ROPE = 64
MLA_V = 128
ROPE_THETA = 10000.0
DEEPNORM_ALPHA = (2 * DEPTH) ** 0.25
DA = N_HEADS_A * HEAD_DIM
DB = N_HEADS_B * HEAD_DIM
LN_EPS = 1e-5
RMS_EPS = 1e-6
LOG2E = math.log2(math.e)

LANES = 128
VMEM_BUDGET_BYTES = 56 * 2**20
VMEM_FLOOR_BYTES = 16 * 2**20
VMEM_SPILL_BYTES = 4 * 2**20
MASK_VALUE = -1e30
ATTN_TILE = 2 * MOBA_BLOCK


def _vmem_limit(block_bytes, scratch_bytes=0, temp_bytes=0):
    est = 2 * block_bytes + scratch_bytes + temp_bytes + VMEM_SPILL_BYTES
    return int(min(max(est, VMEM_FLOOR_BYTES), VMEM_BUDGET_BYTES))


def _params(vmem_bytes, ngrid):
    return pltpu.CompilerParams(
        dimension_semantics=("arbitrary",) * ngrid, vmem_limit_bytes=vmem_bytes)


def _split3(x):
    hi = x.astype(BF16).astype(F32)
    r1 = x - hi
    lo = r1.astype(BF16).astype(F32)
    return hi, lo, r1 - lo


def _qk(a, b):
    return lax.dot_general(a, b, (((1,), (1,)), ((), ())), preferred_element_type=F32)


def _mm_heads_kernel(x_ref, w_ref, cs_ref, o_ref, *, width):
    acc = jnp.dot(x_ref[...], w_ref[...], preferred_element_type=F32) * cs_ref[...]
    for hh in range(o_ref.shape[0]):
        o_ref[hh] = acc[:, hh * width:(hh + 1) * width].astype(o_ref.dtype)


def mm_heads(x, w, colscale, *, tm, tn, width=HEAD_DIM):
    M, K = x.shape
    N = w.shape[1]
    nh = tn // width
    blk = tm * K * 2 + K * tn * 2 + tm * tn * 2 + tn * 4
    return pl.pallas_call(
        functools.partial(_mm_heads_kernel, width=width),
        grid=(M // tm, N // tn),
        in_specs=[pl.BlockSpec((tm, K), lambda i, j: (i, 0)),
                  pl.BlockSpec((K, tn), lambda i, j: (0, j)),
                  pl.BlockSpec((1, tn), lambda i, j: (0, j))],
        out_specs=pl.BlockSpec((nh, tm, width), lambda i, j: (j, i, 0)),
        out_shape=jax.ShapeDtypeStruct((N // width, M, width), BF16),
        compiler_params=_params(_vmem_limit(blk, temp_bytes=2 * tm * tn * 4), 2),
        name="mm_heads",
    )(x, w, colscale)


def _fox_gate_kernel(x_ref, wf_ref, bf_ref, selq_ref, selk_ref, cq_ref, ck_ref,
                     qaug_ref, kaug_ref, carry_ref, *, tb, nh):
    @pl.when(pl.program_id(0) == 0)
    def _():
        carry_ref[...] = jnp.zeros_like(carry_ref)

    z = jnp.dot(x_ref[...], wf_ref[...], preferred_element_type=F32) + bf_ref[...]
    lf = jnp.minimum(z, 0.0) - jnp.log1p(jnp.exp(-jnp.abs(z)))
    row = lax.broadcasted_iota(jnp.int32, (tb, tb), 0)
    col = lax.broadcasted_iota(jnp.int32, (tb, tb), 1)
    tri = jnp.where(row >= col, 1.0, 0.0).astype(BF16)
    lf_terms = jnp.concatenate(_split3(lf), axis=1).astype(BF16)
    part = jnp.dot(tri, lf_terms, preferred_element_type=F32)
    cs = (part[:, :LANES] + part[:, LANES:2 * LANES] + part[:, 2 * LANES:]) + carry_ref[...]
    carry_ref[...] = cs[tb - 1:tb, :]
    lane = lax.broadcasted_iota(jnp.int32, (tb, LANES), 1)
    hi, lo, lo2 = [jnp.where(lane < nh, term, 0.0) for term in _split3(cs * LOG2E)]
    terms = (hi + pltpu.roll(lo, nh, 1) + pltpu.roll(lo2, 2 * nh, 1)).astype(BF16)
    qa = jnp.dot(terms, selq_ref[...], preferred_element_type=F32) + cq_ref[...]
    ka = jnp.dot(terms, selk_ref[...], preferred_element_type=F32) + ck_ref[...]
    for h in range(nh):
        qaug_ref[h] = qa[:, h * LANES:(h + 1) * LANES].astype(BF16)
        kaug_ref[h] = ka[:, h * LANES:(h + 1) * LANES].astype(BF16)


def _fox_selectors(nh):
    selq = np.zeros((LANES, nh * LANES), np.float32)
    selk = np.zeros((LANES, nh * LANES), np.float32)
    cq = np.zeros((1, nh * LANES), np.float32)
    ck = np.zeros((1, nh * LANES), np.float32)
    for h in range(nh):
        for term in range(3):
            selq[term * nh + h, h * LANES + term] = 1.0
            selk[term * nh + h, h * LANES + 3 + term] = -1.0
        cq[0, h * LANES + 3:h * LANES + 6] = 1.0
        ck[0, h * LANES:h * LANES + 3] = 1.0
    return (jnp.asarray(selq, dtype=BF16), jnp.asarray(selk, dtype=BF16),
            jnp.asarray(cq), jnp.asarray(ck))


def fox_gate(xb, wf_pad, bf_pad, *, tb=512, nh=N_HEADS_A):
    S, K = xb.shape
    selq, selk, cq, ck = _fox_selectors(nh)
    blk = (tb * K * 2 + K * LANES * 2 + 2 * nh * tb * LANES * 2
           + 2 * LANES * nh * LANES * 2)
    aug = jax.ShapeDtypeStruct((nh, S, LANES), BF16)
    fixed = lambda i: (0, 0)
    return pl.pallas_call(
        functools.partial(_fox_gate_kernel, tb=tb, nh=nh),
        grid=(S // tb,),
        in_specs=[pl.BlockSpec((tb, K), lambda i: (i, 0)),
                  pl.BlockSpec((K, LANES), fixed),
                  pl.BlockSpec((1, LANES), fixed),
                  pl.BlockSpec(selq.shape, fixed), pl.BlockSpec(selk.shape, fixed),
                  pl.BlockSpec(cq.shape, fixed), pl.BlockSpec(ck.shape, fixed)],
        out_specs=[pl.BlockSpec((nh, tb, LANES), lambda i: (0, i, 0)),
                   pl.BlockSpec((nh, tb, LANES), lambda i: (0, i, 0))],
        out_shape=[aug, aug],
        scratch_shapes=[pltpu.VMEM((1, LANES), F32)],
        compiler_params=_params(
            _vmem_limit(blk, temp_bytes=4 * tb * tb * 4 + 4 * tb * nh * LANES * 4), 1),
        name="fox_gate",
    )(xb, wf_pad, bf_pad, selq, selk, cq, ck)


def _moba_gate_kernel(rb_ref, q_ref, k_ref, qaug_ref, kmean_ref, *, nb, tr):
    B = MOBA_BLOCK
    h = pl.program_id(0)
    i = pl.program_id(1)
    far_bias = rb_ref[N_BUCKETS - 1, h] * LOG2E

    @pl.when(i == 0)
    def _():
        kf = k_ref[0].astype(F32).reshape(nb, B, HEAD_DIM)
        terms = _split3(jnp.sum(kf, axis=1) * (1.0 / B))
        kmean_ref[...] = jnp.concatenate(terms, axis=0).astype(BF16)

    blk = lax.broadcasted_iota(jnp.int32, (nb, tr), 0)
    blk_f = blk.astype(F32)
    own = i * (tr // B) + lax.broadcasted_iota(jnp.int32, (nb, tr), 1) // B
    g3 = _qk(kmean_ref[...], q_ref[0])
    g = g3[0:nb] + g3[nb:2 * nb] + g3[2 * nb:3 * nb]
    neg_inf = -jnp.inf
    g = jnp.where(blk < own, g, neg_inf)
    mb = jnp.where(blk == own, 0.0, MASK_VALUE).astype(F32)
    for _ in range(MOBA_TOPK):
        mx = jnp.max(g, axis=0, keepdims=True)
        hit = jnp.logical_and(g == mx, mx > neg_inf)
        idx = jnp.min(jnp.where(hit, blk_f, float(nb)), axis=0, keepdims=True)
        pick = blk_f == idx
        mb = jnp.where(pick, far_bias, mb)
        g = jnp.where(pick, neg_inf, g)
    mb_hi = mb.astype(BF16).astype(F32)
    mb_lo = jnp.where(mb > 0.5 * MASK_VALUE, mb - mb_hi, 0.0)
    aug_t = jnp.concatenate([mb_hi, mb_lo, jnp.zeros((LANES - 2 * nb, tr), F32)], axis=0)
    qaug_ref[0] = jnp.transpose(aug_t).astype(BF16)


def moba_gate(heads, rel_bias, *, n_heads, q_off, k_off, tr=1024):
    S = heads.shape[1]
    nb = S // MOBA_BLOCK
    assert 2 * nb <= LANES
    blk = tr * HEAD_DIM * 2 + S * HEAD_DIM * 2 + tr * LANES * 2
    return pl.pallas_call(
        functools.partial(_moba_gate_kernel, nb=nb, tr=tr),
        grid=(n_heads, S // tr),
        in_specs=[pl.BlockSpec(memory_space=pltpu.SMEM),
                  pl.BlockSpec((1, tr, HEAD_DIM), lambda h, i: (h + q_off, i, 0)),
                  pl.BlockSpec((1, S, HEAD_DIM), lambda h, i: (h + k_off, 0, 0))],
        out_specs=pl.BlockSpec((1, tr, LANES), lambda h, i: (h, i, 0)),
        out_shape=jax.ShapeDtypeStruct((n_heads, S, LANES), BF16),
        scratch_shapes=[pltpu.VMEM((3 * nb, HEAD_DIM), BF16)],
        compiler_params=_params(
            _vmem_limit(blk, 3 * nb * HEAD_DIM * 2,
                        temp_bytes=S * HEAD_DIM * 4 + 12 * tr * LANES * 4), 2),
        name="moba_gate",
    )(rel_bias, heads, heads)


def _moba_bias_kernel(rb_ref, own_ref, prev_ref, o_ref):
    B = MOBA_BLOCK
    h = pl.program_id(0)
    far = rb_ref[N_BUCKETS - 1, h]
    own_bucket = own_ref[...]
    prev_bucket = prev_ref[...]
    own = jnp.zeros((B, B), F32)
    prev = jnp.zeros((B, B), F32)
    for b in range(N_BUCKETS):
        val = rb_ref[b, h]
        own = jnp.where(own_bucket == b, val, own)
        prev = jnp.where(prev_bucket == b, val, prev)
    own = jnp.where(own_bucket < 0, MASK_VALUE, own * LOG2E)
    prev = (prev - far) * LOG2E
    o_ref[0] = jnp.zeros(o_ref.shape[1:], F32)
    o_ref[0, 0:B, B:2 * B] = prev
    o_ref[0, 0:B, 2 * B:3 * B] = own
    o_ref[0, B:2 * B, 2 * B:3 * B] = prev
    o_ref[0, B:2 * B, 3 * B:4 * B] = own


def _t5_bucket_table(n):
    rel = np.arange(n)
    max_exact = N_BUCKETS // 2
    nf = np.maximum(rel, 1).astype(np.float32)
    large = max_exact + (np.log(nf / np.float32(max_exact))
                         / np.float32(math.log(MAX_DISTANCE / max_exact))
                         * np.float32(N_BUCKETS - max_exact)).astype(np.int32)
    large = np.minimum(large, N_BUCKETS - 1)
    return np.where(rel < max_exact, rel, large).astype(np.int32)


def moba_bias(rel_bias, *, n_heads, t):
    B = MOBA_BLOCK
    assert t == 2 * B
    table = _t5_bucket_table(2 * B)
    rel = np.arange(B)[:, None] - np.arange(B)[None, :]
    own_bucket = np.where(rel >= 0, table[np.maximum(rel, 0)], -1).astype(np.int32)
    prev_bucket = table[rel + B].astype(np.int32)
    blk = t * 2 * t * 4 + 2 * B * B * 4
    return pl.pallas_call(
        _moba_bias_kernel,
        grid=(n_heads,),
        in_specs=[pl.BlockSpec(memory_space=pltpu.SMEM),
                  pl.BlockSpec((B, B), lambda h: (0, 0)),
                  pl.BlockSpec((B, B), lambda h: (0, 0))],
        out_specs=pl.BlockSpec((1, t, 2 * t), lambda h: (h, 0, 0)),
        out_shape=jax.ShapeDtypeStruct((n_heads, t, 2 * t), F32),
        compiler_params=_params(_vmem_limit(blk, temp_bytes=8 * B * B * 4), 1),
        name="moba_bias",
    )(rel_bias, jnp.asarray(own_bucket), jnp.asarray(prev_bucket))


def _causal_bias(t):
    rows = np.arange(t)[:, None]
    cols = np.arange(2 * t)[None, :] - t
    return jnp.asarray(np.where(cols <= rows, 0.0, MASK_VALUE).astype(np.float32)[None])


FINAL_TILES = 4


def _ones_column(rows):
    lane = lax.broadcasted_iota(jnp.int32, (rows, LANES), 1)
    return jnp.where(lane == 0, 1.0, 0.0).astype(BF16)


def _flash_kernel(*refs, t, nparts, nsplit, group, prev_biased):
    q_refs = refs[:nparts]
    k_refs = refs[nparts:2 * nparts]
    v_ref, b2_ref, o_ref, m_ref, acc_ref = refs[2 * nparts:]
    qi = pl.program_id(1)
    dv = v_ref.shape[-1]
    r = t // nsplit
    chains = [(g, c) for g in range(group) for c in range(nsplit)]

    def head(ref, g):
        return g if ref.shape[0] == group else 0

    qs = [jnp.concatenate([ref[head(ref, g), c * r:(c + 1) * r, :] for ref in q_refs], axis=1)
          for g, c in chains]
    m_ref[...] = jnp.full_like(m_ref, MASK_VALUE)
    acc_ref[...] = jnp.zeros_like(acc_ref)

    def step(tile, width, n_biased=0):
        ks = pl.multiple_of(tile * t, t)
        ones = _ones_column(width)
        bw = n_biased * t
        ss = []
        for n, (g, c) in enumerate(chains):
            k = jnp.concatenate([ref[head(ref, g), pl.ds(ks, width), :] for ref in k_refs],
                                axis=1)
            s = _qk(qs[n], k)
            if bw:
                bias = b2_ref[head(b2_ref, g), c * r:(c + 1) * r, 2 * t - bw:2 * t]
                tail = s[:, width - bw:] + bias
                s = tail if bw == width else jnp.concatenate([s[:, :width - bw], tail], axis=1)
            ss.append(s)
        for n, (g, c) in enumerate(chains):
            v1 = jnp.concatenate([v_ref[g, pl.ds(ks, width), :], ones], axis=1)
            m = m_ref[n]
            m_new = jnp.maximum(m, jnp.max(ss[n], axis=1, keepdims=True))
            p = jnp.exp2(ss[n] - jnp.concatenate([m_new] * (width // LANES), axis=1))
            a = jnp.exp2(m - m_new)
            acc_ref[n] = (jnp.concatenate([a] * ((dv + LANES) // LANES), axis=1) * acc_ref[n]
                          + jnp.dot(p.astype(BF16), v1, preferred_element_type=F32))
            m_ref[n] = m_new

    n_biased = jnp.minimum(2 if prev_biased else 1, qi + 1)
    lead = qi + 1 - n_biased
    quads = lead // FINAL_TILES

    def body(kb, carry):
        step(FINAL_TILES * kb, FINAL_TILES * t)
        return carry

    lax.fori_loop(0, quads, body, 0)
    start = quads * FINAL_TILES
    rest = qi + 1 - start
    if prev_biased:
        @pl.when(rest > FINAL_TILES)
        def _():
            step(start, t)
    spill = (rest > FINAL_TILES).astype(jnp.int32)
    for width_tiles in range(1, FINAL_TILES + 1):
        @pl.when(rest - spill == width_tiles)
        def _(width_tiles=width_tiles):
            step(start + spill, width_tiles * t,
                 n_biased=min(2 if prev_biased else 1, width_tiles))
    for g in range(group):
        out = []
        for c in range(nsplit):
            acc = acc_ref[g * nsplit + c]
            out.append(acc[:, :dv] * (1.0 / acc[:, dv:dv + 1]))
        o_ref[:, g * dv:(g + 1) * dv] = jnp.concatenate(out, axis=0).astype(o_ref.dtype)


def flash_attention(q_parts, k_parts, v_part, b2, *, n_heads, name, t=ATTN_TILE, nsplit=1,
                    group=2, prev_biased=False):
    v, v_off = v_part
    S, dv = v.shape[1], v.shape[2]

    def spec(arr, off, rows):
        g = group if off is not None and arr.shape[0] > 1 else 1
        assert off is None or off % group == 0
        blk_shape = (g, t if rows else arr.shape[1], arr.shape[2])
        if g == 1:
            index = lambda h, i: (0, i if rows else 0, 0)
        else:
            index = lambda h, i: (h + off // group, i if rows else 0, 0)
        return pl.BlockSpec(blk_shape, index), math.prod(blk_shape) * arr.dtype.itemsize

    operands = ([(a, o, True) for a, o in q_parts] + [(a, o, False) for a, o in k_parts]
                + [(v, v_off, False), (b2, 0 if b2.shape[0] > 1 else None, False)])
    in_specs, blk = [], t * group * dv * 2
    for arr, off, rows in operands:
        s, nbytes = spec(arr, off, rows)
        in_specs.append(s)
        blk += nbytes
    nchain = group * nsplit
    r = t // nsplit
    return pl.pallas_call(
        functools.partial(_flash_kernel, t=t, nparts=len(q_parts), nsplit=nsplit, group=group,
                          prev_biased=prev_biased),
        grid=(n_heads // group, S // t),
        in_specs=in_specs,
        out_specs=pl.BlockSpec((t, group * dv), lambda h, i: (i, h)),
        out_shape=jax.ShapeDtypeStruct((S, n_heads * dv), BF16),
        scratch_shapes=[pltpu.VMEM((nchain, r, LANES), F32),
                        pltpu.VMEM((nchain, r, dv + LANES), F32)],
        compiler_params=_params(
            _vmem_limit(blk, nchain * r * (dv + 2 * LANES) * 4,
                        temp_bytes=6 * group * t * t * 4), 2),
        name=name,
    )(*[arr for arr, _, _ in operands])


def _moba_key_onehot(S):
    nb = S // MOBA_BLOCK
    lanes = np.arange(LANES)[None, :]
    blk_of = (np.arange(S) // MOBA_BLOCK)[:, None]
    onehot = (lanes % nb == blk_of) & (lanes < 2 * nb)
    return jnp.asarray(onehot.astype(np.float32)[None], dtype=BF16)


def _mm_res_ln_kernel(*refs, n_a, chunks):
    a_refs = refs[:n_a]
    w_ref, x_ref, g_ref, b_ref, o_ref, obf_ref = refs[n_a:]
    rc = x_ref.shape[0] // chunks
    for c in range(chunks):
        rows = slice(c * rc, (c + 1) * rc)
        y, row = None, 0
        for a_ref in a_refs:
            ka = a_ref.shape[1]
            d = jnp.dot(a_ref[rows, :], w_ref[row:row + ka, :], preferred_element_type=F32)
            y = d if y is None else y + d
            row += ka
        z = DEEPNORM_ALPHA * x_ref[rows, :] + y
        mu = jnp.mean(z, axis=-1, keepdims=True)
        zc = z - mu
        var = jnp.mean(zc * zc, axis=-1, keepdims=True)
        out = zc * lax.rsqrt(var + LN_EPS) * g_ref[...] + b_ref[...]
        o_ref[rows, :] = out
        obf_ref[rows, :] = out.astype(BF16)


def mm_res_ln(a_list, w_stack, layer, x, g, b, *, tm, chunks=2):
    M = a_list[0].shape[0]
    K = sum(a.shape[1] for a in a_list)
    N = w_stack.shape[2]
    blk = tm * K * 2 + tm * N * 4 + 2 * N * 4 + tm * N * 4 + tm * N * 2
    return pl.pallas_call(
        functools.partial(_mm_res_ln_kernel, n_a=len(a_list), chunks=chunks),
        grid=(M // tm,),
        in_specs=[pl.BlockSpec((tm, a.shape[1]), lambda i: (i, 0)) for a in a_list] + [
            pl.BlockSpec((None, K, N), lambda i: (layer, 0, 0), pipeline_mode=pl.Buffered(1)),
            pl.BlockSpec((tm, N), lambda i: (i, 0)),
            pl.BlockSpec((1, N), lambda i: (0, 0)),
            pl.BlockSpec((1, N), lambda i: (0, 0))],
        out_specs=[pl.BlockSpec((tm, N), lambda i: (i, 0)),
                   pl.BlockSpec((tm, N), lambda i: (i, 0))],
        out_shape=[jax.ShapeDtypeStruct((M, N), F32),
                   jax.ShapeDtypeStruct((M, N), BF16)],
        compiler_params=_params(_vmem_limit(blk, K * N * 2, temp_bytes=3 * tm * N * 4), 1),
        name="mm_res_ln",
    )(*a_list, w_stack, x, g, b)


def _ffn_gu_kernel(x_ref, wg_ref, wu_ref, o_ref, wgb_ref, wub_ref):
    @pl.when(pl.program_id(1) == 0)
    def _():
        wgb_ref[...] = wg_ref[...].astype(BF16)
        wub_ref[...] = wu_ref[...].astype(BF16)

    x = x_ref[...]
    g = jnp.dot(x, wgb_ref[...], preferred_element_type=F32)
    u = jnp.dot(x, wub_ref[...], preferred_element_type=F32)
    o_ref[...] = (g * jax.nn.sigmoid(g) * u).astype(o_ref.dtype)


def ffn_gate_up(xb, wg_stack, wu_stack, layer, *, tm, tf):
    M, K = xb.shape
    F = wg_stack.shape[2]
    blk = tm * K * 2 + 2 * K * tf * 4 + tm * tf * 2
    w_spec = pl.BlockSpec((None, K, tf), lambda j, i: (layer, 0, j))
    return pl.pallas_call(
        _ffn_gu_kernel,
        grid=(F // tf, M // tm),
        in_specs=[pl.BlockSpec((tm, K), lambda j, i: (i, 0)), w_spec, w_spec],
        out_specs=pl.BlockSpec((tm, tf), lambda j, i: (i, j)),
        out_shape=jax.ShapeDtypeStruct((M, F), BF16),
        scratch_shapes=[pltpu.VMEM((K, tf), BF16), pltpu.VMEM((K, tf), BF16)],
        compiler_params=_params(
            _vmem_limit(blk, 2 * K * tf * 2, temp_bytes=4 * tm * tf * 4), 2),
        name="ffn_gate_up",
    )(xb, wg_stack, wu_stack)


def _rope_lanes(r, cos_t, sin_a, sin_b):
    return (r * cos_t + pltpu.roll(r, LANES - MLA_ROPE // 2, 1) * sin_a
            + pltpu.roll(r, MLA_ROPE // 2, 1) * sin_b)


def _rms(x, g):
    return x * lax.rsqrt(jnp.mean(x * x, axis=-1, keepdims=True) + RMS_EPS) * g


def _mla_in_kernel(x_ref, w_ref, gq_ref, gkv_ref, cos_ref, sa_ref, sb_ref,
                   cq_ref, ckv_ref, kr_ref):
    half = x_ref.shape[0] // 2
    for rows in (slice(0, half), slice(half, 2 * half)):
        h = jnp.dot(x_ref[rows, :], w_ref[...], preferred_element_type=F32)
        cq_ref[rows, :] = _rms(h[:, :MLA_Q_RANK], gq_ref[...]).astype(BF16)
        ckv_ref[rows, :] = _rms(h[:, MLA_Q_RANK:MLA_Q_RANK + MLA_KV_RANK],
                                gkv_ref[...]).astype(BF16)
        r = h[:, MLA_Q_RANK + MLA_KV_RANK:]
        kr_ref[rows, :] = _rope_lanes(r, cos_ref[rows, :], sa_ref[rows, :],
                                      sb_ref[rows, :]).astype(BF16)


def mla_in(xb, w_pad, gq, gkv, cos_t, sin_a, sin_b, *, tm=512):
    S, K = xb.shape
    N = w_pad.shape[1]
    blk = (tm * K * 2 + K * N * 2 + 3 * tm * LANES * 4
           + tm * (MLA_Q_RANK + MLA_KV_RANK + LANES) * 2)
    row = lambda i: (i, 0)
    fixed = lambda i: (0, 0)
    return pl.pallas_call(
        _mla_in_kernel,
        grid=(S // tm,),
        in_specs=[pl.BlockSpec((tm, K), row), pl.BlockSpec((K, N), fixed),
                  pl.BlockSpec((1, MLA_Q_RANK), fixed), pl.BlockSpec((1, MLA_KV_RANK), fixed),
                  pl.BlockSpec((tm, LANES), row), pl.BlockSpec((tm, LANES), row),
                  pl.BlockSpec((tm, LANES), row)],
        out_specs=[pl.BlockSpec((tm, MLA_Q_RANK), row), pl.BlockSpec((tm, MLA_KV_RANK), row),
                   pl.BlockSpec((tm, LANES), row)],
        out_shape=[jax.ShapeDtypeStruct((S, MLA_Q_RANK), BF16),
                   jax.ShapeDtypeStruct((S, MLA_KV_RANK), BF16),
                   jax.ShapeDtypeStruct((S, LANES), BF16)],
        compiler_params=_params(_vmem_limit(blk, temp_bytes=3 * tm * N * 4), 1),
        name="mla_in",
    )(xb, w_pad, gq, gkv, cos_t, sin_a, sin_b)


def _mla_q_kernel(cq_ref, w_ref, cos_ref, sa_ref, sb_ref, q_ref, *, qscale):
    acc = jnp.dot(cq_ref[...], w_ref[...], preferred_element_type=F32) * qscale
    cos_t, sin_a, sin_b = cos_ref[...], sa_ref[...], sb_ref[...]
    for hh in range(q_ref.shape[0]):
        base = hh * 2 * LANES
        q_ref[hh, :, 0:LANES] = acc[:, base:base + LANES].astype(BF16)
        r = acc[:, base + LANES:base + 2 * LANES]
        q_ref[hh, :, LANES:2 * LANES] = _rope_lanes(r, cos_t, sin_a, sin_b).astype(BF16)


def mla_q(cq, w_pad, cos_t, sin_a, sin_b, *, qscale, tm=512):
    S, K = cq.shape
    N = w_pad.shape[1]
    nh = N // (2 * LANES)
    blk = tm * K * 2 + K * N * 2 + 3 * tm * LANES * 4 + tm * N * 2
    row = lambda i: (i, 0)
    return pl.pallas_call(
        functools.partial(_mla_q_kernel, qscale=qscale),
        grid=(S // tm,),
        in_specs=[pl.BlockSpec((tm, K), row), pl.BlockSpec((K, N), lambda i: (0, 0)),
                  pl.BlockSpec((tm, LANES), row), pl.BlockSpec((tm, LANES), row),
                  pl.BlockSpec((tm, LANES), row)],
        out_specs=pl.BlockSpec((nh, tm, 2 * LANES), lambda i: (0, i, 0)),
        out_shape=jax.ShapeDtypeStruct((nh, S, 2 * LANES), BF16),
        compiler_params=_params(_vmem_limit(blk, temp_bytes=2 * tm * N * 4), 1),
        name="mla_q",
    )(cq, w_pad, cos_t, sin_a, sin_b)


def _mla_kv_kernel(ckv_ref, w_ref, kr_ref, k_ref, v_ref):
    acc = jnp.dot(ckv_ref[...], w_ref[...], preferred_element_type=F32)
    kr = kr_ref[...]
    for hh in range(k_ref.shape[0]):
        base = hh * (MLA_NOPE + MLA_V)
        k_ref[hh, :, 0:LANES] = acc[:, base:base + MLA_NOPE].astype(BF16)
        k_ref[hh, :, LANES:2 * LANES] = kr
        v_ref[hh] = acc[:, base + MLA_NOPE:base + MLA_NOPE + MLA_V].astype(BF16)


def mla_kv(ckv, w, kr, *, tm=512):
    S, K = ckv.shape
    N = w.shape[1]
    nh = N // (MLA_NOPE + MLA_V)
    blk = tm * K * 2 + K * N * 2 + tm * LANES * 2 + nh * tm * (2 * LANES + MLA_V) * 2
    row = lambda i: (i, 0)
    return pl.pallas_call(
        _mla_kv_kernel,
        grid=(S // tm,),
        in_specs=[pl.BlockSpec((tm, K), row), pl.BlockSpec((K, N), lambda i: (0, 0)),
                  pl.BlockSpec((tm, LANES), row)],
        out_specs=[pl.BlockSpec((nh, tm, 2 * LANES), lambda i: (0, i, 0)),
                   pl.BlockSpec((nh, tm, MLA_V), lambda i: (0, i, 0))],
        out_shape=[jax.ShapeDtypeStruct((nh, S, 2 * LANES), BF16),
                   jax.ShapeDtypeStruct((nh, S, MLA_V), BF16)],
        compiler_params=_params(_vmem_limit(blk, temp_bytes=2 * tm * N * 4), 1),
        name="mla_kv",
    )(ckv, w, kr)


def _rope_lane_tables(S):
    half = MLA_ROPE // 2
    inv = ROPE_THETA ** (-np.arange(0, MLA_ROPE, 2, dtype=np.float64) / MLA_ROPE)
    ang = np.arange(S, dtype=np.float64)[:, None] * inv[None, :]
    cos, sin = np.cos(ang), np.sin(ang)
    z = np.zeros((S, half))
    z2 = np.zeros((S, LANES - MLA_ROPE))
    cos_t = np.concatenate([cos, cos, z2], axis=1)
    sin_a = np.concatenate([-sin, z, z2], axis=1)
    sin_b = np.concatenate([z, sin, z2], axis=1)
    return tuple(jnp.asarray(a.astype(np.float32)) for a in (cos_t, sin_a, sin_b))


def _pad_cols(w, n):
    return jnp.pad(w, ((0, 0), (0, n - w.shape[1])))


def kernel(x, ab_w_in, ab_forget_bias, ab_w_out, rel_bias, mla_w_in, mla_q_norm,
           mla_kv_norm, mla_w_uq, mla_w_ukv, mla_w_out, ffn_w_gate, ffn_w_up,
           ffn_w_down, ln_g, ln_b):
    S = x.shape[1]
    xf = x.reshape(S, D_MODEL)
    xb = xf.astype(BF16)
    cos_t, sin_a, sin_b = _rope_lane_tables(S)
    nf = 3 * DA + N_HEADS_A
    qs = HEAD_DIM ** -0.5 * LOG2E
    colscale = np.ones((1, 3 * DA + 3 * DB), np.float32)
    colscale[:, :DA] = qs
    colscale[:, 3 * DA:3 * DA + DB] = qs
    colscale = jnp.asarray(colscale)
    causal_b2 = _causal_bias(ATTN_TILE)
    moba_b2 = moba_bias(rel_bias, n_heads=N_HEADS_B, t=ATTN_TILE)
    moba_kaug = _moba_key_onehot(S)
    hb = 3 * N_HEADS_A
    ab_w_out_b = ab_w_out.astype(BF16)
    mla_w_out_b = mla_w_out.astype(BF16)
    ffn_w_down_b = ffn_w_down.astype(BF16)

    for layer in range(DEPTH):
        j = layer // 2
        g0, b0 = ln_g[layer, 0][None, :], ln_b[layer, 0][None, :]
        g1, b1 = ln_g[layer, 1][None, :], ln_b[layer, 1][None, :]
        if layer % 2 == 0:
            w_in = ab_w_in[j]
            w_qkv = jnp.concatenate([w_in[:, :3 * DA], w_in[:, nf:]], axis=1).astype(BF16)
            w_f = _pad_cols(w_in[:, 3 * DA:nf], LANES).astype(BF16)
            b_f = _pad_cols(ab_forget_bias[j][None, :], LANES)
            heads = mm_heads(xb, w_qkv, colscale, tm=1024, tn=1024)
            qaug, kaug = fox_gate(xb, w_f, b_f)
            ya = flash_attention([(heads, 0), (qaug, 0)], [(heads, N_HEADS_A), (kaug, 0)],
                                 (heads, 2 * N_HEADS_A), causal_b2,
                                 n_heads=N_HEADS_A, name="flash_fox")
            qaug_b = moba_gate(heads, rel_bias, n_heads=N_HEADS_B, q_off=hb,
                               k_off=hb + N_HEADS_B)
            yb = flash_attention([(heads, hb), (qaug_b, 0)],
                                 [(heads, hb + N_HEADS_B), (moba_kaug, None)],
                                 (heads, hb + 2 * N_HEADS_B), moba_b2,
                                 n_heads=N_HEADS_B, name="flash_moba", prev_biased=True)
            y, w_out = [ya, yb], ab_w_out_b
        else:
            w_in = _pad_cols(mla_w_in[j], MLA_Q_RANK + MLA_KV_RANK + LANES).astype(BF16)
            cq, ckv, kr = mla_in(xb, w_in, mla_q_norm[j][None, :], mla_kv_norm[j][None, :],
                                 cos_t, sin_a, sin_b)
            w_uq = mla_w_uq[j].reshape(MLA_Q_RANK, MLA_HEADS, MLA_NOPE + MLA_ROPE)
            w_uq = jnp.pad(w_uq, ((0, 0), (0, 0), (0, 2 * LANES - MLA_NOPE - MLA_ROPE)))
            w_uq = w_uq.reshape(MLA_Q_RANK, MLA_HEADS * 2 * LANES).astype(BF16)
            q_full = mla_q(cq, w_uq, cos_t, sin_a, sin_b,
                           qscale=(MLA_NOPE + MLA_ROPE) ** -0.5 * LOG2E)
            k_full, v = mla_kv(ckv, mla_w_ukv[j].astype(BF16), kr)
            y = [flash_attention([(q_full, 0)], [(k_full, 0)], (v, 0), causal_b2,
                                 n_heads=MLA_HEADS, name="flash_mla")]
            w_out = mla_w_out_b
        xf, xb = mm_res_ln(y, w_out, j, xf, g0, b0, tm=512)
        hmid = ffn_gate_up(xb, ffn_w_gate, ffn_w_up, layer, tm=1024, tf=512)
        xf, xb = mm_res_ln([hmid], ffn_w_down_b, layer, xf, g1, b1, tm=256)
    return xf.reshape(1, S, D_MODEL)
```

```python
import functools
import math

import numpy as np
import jax
import jax.numpy as jnp
from jax import lax
from jax.experimental import pallas as pl
from jax.experimental.pallas import tpu as pltpu

F32 = jnp.float32
BF16 = jnp.bfloat16

D_MODEL = 2048
DEPTH = 4
HEAD_DIM = 128
N_HEADS_A = 8
N_HEADS_B = 8
MOBA_BLOCK = 256
MOBA_TOPK = 3
N_BUCKETS = 32
MAX_DISTANCE = 128
MLA_HEADS = 16
MLA_Q_RANK = 512
MLA_KV_RANK = 512
MLA_NOPE = 128
MLA_ROPE = 64
MLA_V = 128
ROPE_THETA = 10000.0
DEEPNORM_ALPHA = (2 * DEPTH) ** 0.25
DA = N_HEADS_A * HEAD_DIM
DB = N_HEADS_B * HEAD_DIM
LN_EPS = 1e-5
RMS_EPS = 1e-6
LOG2E = math.log2(math.e)

LANES = 128
VMEM_BUDGET_BYTES = 56 * 2**20
VMEM_FLOOR_BYTES = 16 * 2**20
VMEM_SPILL_BYTES = 4 * 2**20
MASK_VALUE = -1e30
ATTN_TILE = 2 * MOBA_BLOCK


def _vmem_limit(block_bytes, scratch_bytes=0, temp_bytes=0):
    est = 2 * block_bytes + scratch_bytes + temp_bytes + VMEM_SPILL_BYTES
    return int(min(max(est, VMEM_FLOOR_BYTES), VMEM_BUDGET_BYTES))


def _params(vmem_bytes, ngrid, fuse=None):
    return pltpu.CompilerParams(
        dimension_semantics=("arbitrary",) * ngrid, vmem_limit_bytes=vmem_bytes,
        allow_input_fusion=fuse)


def _split3(x):
    hi = x.astype(BF16).astype(F32)
    r1 = x - hi
    lo = r1.astype(BF16).astype(F32)
    return hi, lo, r1 - lo


def _qk(a, b):
    return lax.dot_general(a, b, (((1,), (1,)), ((), ())), preferred_element_type=F32)


def _mm_heads_kernel(x_ref, w_ref, cs_ref, o_ref, *, width):
    acc = jnp.dot(x_ref[...], w_ref[...], preferred_element_type=F32) * cs_ref[...]
    for hh in range(o_ref.shape[0]):
        o_ref[hh] = acc[:, hh * width:(hh + 1) * width].astype(o_ref.dtype)


def mm_heads(x, w, colscale, *, tm, tn, width=HEAD_DIM):
    M, K = x.shape
    N = w.shape[1]
    nh = tn // width
    blk = tm * K * 2 + K * tn * 2 + tm * tn * 2 + tn * 4
    return pl.pallas_call(
        functools.partial(_mm_heads_kernel, width=width),
        grid=(M // tm, N // tn),
        in_specs=[pl.BlockSpec((tm, K), lambda i, j: (i, 0)),
                  pl.BlockSpec((K, tn), lambda i, j: (0, j)),
                  pl.BlockSpec((1, tn), lambda i, j: (0, j))],
        out_specs=pl.BlockSpec((nh, tm, width), lambda i, j: (j, i, 0)),
        out_shape=jax.ShapeDtypeStruct((N // width, M, width), BF16),
        compiler_params=_params(_vmem_limit(blk, temp_bytes=2 * tm * tn * 4), 2,
                                fuse=[True, True, False]),
        name="mm_heads",
    )(x, w, colscale)


def _fox_gate_kernel(x_ref, wf_ref, bf_ref, selq_ref, selk_ref, cq_ref, ck_ref,
                     qaug_ref, kaug_ref, carry_ref, *, tb, nh):
    @pl.when(pl.program_id(0) == 0)
    def _():
        carry_ref[...] = jnp.zeros_like(carry_ref)

    z = jnp.dot(x_ref[...], wf_ref[...], preferred_element_type=F32) + bf_ref[...]
    lf = jnp.minimum(z, 0.0) - jnp.log1p(jnp.exp(-jnp.abs(z)))
    row = lax.broadcasted_iota(jnp.int32, (tb, tb), 0)
    col = lax.broadcasted_iota(jnp.int32, (tb, tb), 1)
    tri = jnp.where(row >= col, 1.0, 0.0).astype(BF16)
    lf_terms = jnp.concatenate(_split3(lf), axis=1).astype(BF16)
    part = jnp.dot(tri, lf_terms, preferred_element_type=F32)
    cs = (part[:, :LANES] + part[:, LANES:2 * LANES] + part[:, 2 * LANES:]) + carry_ref[...]
    carry_ref[...] = cs[tb - 1:tb, :]
    lane = lax.broadcasted_iota(jnp.int32, (tb, LANES), 1)
    hi, lo, lo2 = [jnp.where(lane < nh, term, 0.0) for term in _split3(cs * LOG2E)]
    terms = (hi + pltpu.roll(lo, nh, 1) + pltpu.roll(lo2, 2 * nh, 1)).astype(BF16)
    qa = jnp.dot(terms, selq_ref[...], preferred_element_type=F32) + cq_ref[...]
    ka = jnp.dot(terms, selk_ref[...], preferred_element_type=F32) + ck_ref[...]
    for h in range(nh):
        qaug_ref[h] = qa[:, h * LANES:(h + 1) * LANES].astype(BF16)
        kaug_ref[h] = ka[:, h * LANES:(h + 1) * LANES].astype(BF16)


def _fox_selectors(nh):
    selq = np.zeros((LANES, nh * LANES), np.float32)
    selk = np.zeros((LANES, nh * LANES), np.float32)
    cq = np.zeros((1, nh * LANES), np.float32)
    ck = np.zeros((1, nh * LANES), np.float32)
    for h in range(nh):
        for term in range(3):
            selq[term * nh + h, h * LANES + term] = 1.0
            selk[term * nh + h, h * LANES + 3 + term] = -1.0
        cq[0, h * LANES + 3:h * LANES + 6] = 1.0
        ck[0, h * LANES:h * LANES + 3] = 1.0
    return (jnp.asarray(selq, dtype=BF16), jnp.asarray(selk, dtype=BF16),
            jnp.asarray(cq), jnp.asarray(ck))


def fox_gate(xb, wf_pad, bf_pad, *, tb=512, nh=N_HEADS_A):
    S, K = xb.shape
    selq, selk, cq, ck = _fox_selectors(nh)
    blk = (tb * K * 2 + K * LANES * 2 + 2 * nh * tb * LANES * 2
           + 2 * LANES * nh * LANES * 2)
    aug = jax.ShapeDtypeStruct((nh, S, LANES), BF16)
    fixed = lambda i: (0, 0)
    return pl.pallas_call(
        functools.partial(_fox_gate_kernel, tb=tb, nh=nh),
        grid=(S // tb,),
        in_specs=[pl.BlockSpec((tb, K), lambda i: (i, 0)),
                  pl.BlockSpec((K, LANES), fixed),
                  pl.BlockSpec((1, LANES), fixed),
                  pl.BlockSpec(selq.shape, fixed), pl.BlockSpec(selk.shape, fixed),
                  pl.BlockSpec(cq.shape, fixed), pl.BlockSpec(ck.shape, fixed)],
        out_specs=[pl.BlockSpec((nh, tb, LANES), lambda i: (0, i, 0)),
                   pl.BlockSpec((nh, tb, LANES), lambda i: (0, i, 0))],
        out_shape=[aug, aug],
        scratch_shapes=[pltpu.VMEM((1, LANES), F32)],
        compiler_params=_params(
            _vmem_limit(blk, temp_bytes=4 * tb * tb * 4 + 4 * tb * nh * LANES * 4), 1,
            fuse=[True, True, True, False, False, False, False]),
        name="fox_gate",
    )(xb, wf_pad, bf_pad, selq, selk, cq, ck)


def _moba_gate_kernel(rb_ref, q_ref, k_ref, qaug_ref, kmean_ref, *, nb, tr):
    B = MOBA_BLOCK
    h = pl.program_id(0)
    i = pl.program_id(1)
    far_bias = rb_ref[N_BUCKETS - 1, h] * LOG2E

    @pl.when(i == 0)
    def _():
        kf = k_ref[0].astype(F32).reshape(nb, B, HEAD_DIM)
        terms = _split3(jnp.sum(kf, axis=1) * (1.0 / B))
        kmean_ref[...] = jnp.concatenate(terms, axis=0).astype(BF16)

    blk = lax.broadcasted_iota(jnp.int32, (nb, tr), 0)
    blk_f = blk.astype(F32)
    own = i * (tr // B) + lax.broadcasted_iota(jnp.int32, (nb, tr), 1) // B
    g3 = _qk(kmean_ref[...], q_ref[0])
    g = g3[0:nb] + g3[nb:2 * nb] + g3[2 * nb:3 * nb]
    neg_inf = -jnp.inf
    g = jnp.where(blk < own, g, neg_inf)
    mb = jnp.where(blk == own, 0.0, MASK_VALUE).astype(F32)
    for _ in range(MOBA_TOPK):
        mx = jnp.max(g, axis=0, keepdims=True)
        hit = jnp.logical_and(g == mx, mx > neg_inf)
        idx = jnp.min(jnp.where(hit, blk_f, float(nb)), axis=0, keepdims=True)
        pick = blk_f == idx
        mb = jnp.where(pick, far_bias, mb)
        g = jnp.where(pick, neg_inf, g)
    mb_hi = mb.astype(BF16).astype(F32)
    mb_lo = jnp.where(mb > 0.5 * MASK_VALUE, mb - mb_hi, 0.0)
    aug_t = jnp.concatenate([mb_hi, mb_lo, jnp.zeros((LANES - 2 * nb, tr), F32)], axis=0)
    qaug_ref[0] = jnp.transpose(aug_t).astype(BF16)


def moba_gate(heads, rel_bias, *, n_heads, q_off, k_off, tr=1024):
    S = heads.shape[1]
    nb = S // MOBA_BLOCK
    assert 2 * nb <= LANES
    blk = tr * HEAD_DIM * 2 + S * HEAD_DIM * 2 + tr * LANES * 2
    return pl.pallas_call(
        functools.partial(_moba_gate_kernel, nb=nb, tr=tr),
        grid=(n_heads, S // tr),
        in_specs=[pl.BlockSpec(memory_space=pltpu.SMEM),
                  pl.BlockSpec((1, tr, HEAD_DIM), lambda h, i: (h + q_off, i, 0)),
                  pl.BlockSpec((1, S, HEAD_DIM), lambda h, i: (h + k_off, 0, 0))],
        out_specs=pl.BlockSpec((1, tr, LANES), lambda h, i: (h, i, 0)),
        out_shape=jax.ShapeDtypeStruct((n_heads, S, LANES), BF16),
        scratch_shapes=[pltpu.VMEM((3 * nb, HEAD_DIM), BF16)],
        compiler_params=_params(
            _vmem_limit(blk, 3 * nb * HEAD_DIM * 2,
                        temp_bytes=S * HEAD_DIM * 4 + 12 * tr * LANES * 4), 2),
        name="moba_gate",
    )(rel_bias, heads, heads)


def _moba_bias_kernel(rb_ref, own_ref, prev_ref, o_ref):
    B = MOBA_BLOCK
    h = pl.program_id(0)
    far = rb_ref[N_BUCKETS - 1, h]
    own_bucket = own_ref[...]
    prev_bucket = prev_ref[...]
    own = jnp.zeros((B, B), F32)
    prev = jnp.zeros((B, B), F32)
    for b in range(N_BUCKETS):
        val = rb_ref[b, h]
        own = jnp.where(own_bucket == b, val, own)
        prev = jnp.where(prev_bucket == b, val, prev)
    own = jnp.where(own_bucket < 0, MASK_VALUE, own * LOG2E)
    prev = (prev - far) * LOG2E
    o_ref[0] = jnp.zeros(o_ref.shape[1:], F32)
    o_ref[0, 0:B, B:2 * B] = prev
    o_ref[0, 0:B, 2 * B:3 * B] = own
    o_ref[0, B:2 * B, 2 * B:3 * B] = prev
    o_ref[0, B:2 * B, 3 * B:4 * B] = own


def _t5_bucket_table(n):
    rel = np.arange(n)
    max_exact = N_BUCKETS // 2
    nf = np.maximum(rel, 1).astype(np.float32)
    large = max_exact + (np.log(nf / np.float32(max_exact))
                         / np.float32(math.log(MAX_DISTANCE / max_exact))
                         * np.float32(N_BUCKETS - max_exact)).astype(np.int32)
    large = np.minimum(large, N_BUCKETS - 1)
    return np.where(rel < max_exact, rel, large).astype(np.int32)


def moba_bias(rel_bias, *, n_heads, t):
    B = MOBA_BLOCK
    assert t == 2 * B
    table = _t5_bucket_table(2 * B)
    rel = np.arange(B)[:, None] - np.arange(B)[None, :]
    own_bucket = np.where(rel >= 0, table[np.maximum(rel, 0)], -1).astype(np.int32)
    prev_bucket = table[rel + B].astype(np.int32)
    blk = t * 2 * t * 4 + 2 * B * B * 4
    return pl.pallas_call(
        _moba_bias_kernel,
        grid=(n_heads,),
        in_specs=[pl.BlockSpec(memory_space=pltpu.SMEM),
                  pl.BlockSpec((B, B), lambda h: (0, 0)),
                  pl.BlockSpec((B, B), lambda h: (0, 0))],
        out_specs=pl.BlockSpec((1, t, 2 * t), lambda h: (h, 0, 0)),
        out_shape=jax.ShapeDtypeStruct((n_heads, t, 2 * t), F32),
        compiler_params=_params(_vmem_limit(blk, temp_bytes=8 * B * B * 4), 1),
        name="moba_bias",
    )(rel_bias, jnp.asarray(own_bucket), jnp.asarray(prev_bucket))


def _causal_bias(t):
    rows = np.arange(t)[:, None]
    cols = np.arange(2 * t)[None, :] - t
    return jnp.asarray(np.where(cols <= rows, 0.0, MASK_VALUE).astype(np.float32)[None])


FINAL_TILES = 4


def _ones_column(rows):
    lane = lax.broadcasted_iota(jnp.int32, (rows, LANES), 1)
    return jnp.where(lane == 0, 1.0, 0.0).astype(BF16)


def _flash_kernel(*refs, t, nparts, nsplit, group, prev_biased):
    q_refs = refs[:nparts]
    k_refs = refs[nparts:2 * nparts]
    v_ref, b2_ref, o_ref, m_ref, acc_ref = refs[2 * nparts:]
    qi = pl.program_id(1)
    dv = v_ref.shape[-1]
    r = t // nsplit
    chains = [(g, c) for g in range(group) for c in range(nsplit)]

    def head(ref, g):
        return g if ref.shape[0] == group else 0

    qs = [jnp.concatenate([ref[head(ref, g), c * r:(c + 1) * r, :] for ref in q_refs], axis=1)
          for g, c in chains]
    m_ref[...] = jnp.full_like(m_ref, MASK_VALUE)
    acc_ref[...] = jnp.zeros_like(acc_ref)

    def step(tile, width, n_biased=0):
        ks = pl.multiple_of(tile * t, t)
        ones = _ones_column(width)
        bw = n_biased * t
        ss = []
        for n, (g, c) in enumerate(chains):
            k = jnp.concatenate([ref[head(ref, g), pl.ds(ks, width), :] for ref in k_refs],
                                axis=1)
            s = _qk(qs[n], k)
            if bw:
                bias = b2_ref[head(b2_ref, g), c * r:(c + 1) * r, 2 * t - bw:2 * t]
                tail = s[:, width - bw:] + bias
                s = tail if bw == width else jnp.concatenate([s[:, :width - bw], tail], axis=1)
            ss.append(s)
        for n, (g, c) in enumerate(chains):
            v1 = jnp.concatenate([v_ref[g, pl.ds(ks, width), :], ones], axis=1)
            m = m_ref[n]
            m_new = jnp.maximum(m, jnp.max(ss[n], axis=1, keepdims=True))
            p = jnp.exp2(ss[n] - jnp.concatenate([m_new] * (width // LANES), axis=1))
            a = jnp.exp2(m - m_new)
            acc_ref[n] = (jnp.concatenate([a] * ((dv + LANES) // LANES), axis=1) * acc_ref[n]
                          + jnp.dot(p.astype(BF16), v1, preferred_element_type=F32))
            m_ref[n] = m_new

    n_biased = jnp.minimum(2 if prev_biased else 1, qi + 1)
    lead = qi + 1 - n_biased
    quads = lead // FINAL_TILES

    def body(kb, carry):
        step(FINAL_TILES * kb, FINAL_TILES * t)
        return carry

    lax.fori_loop(0, quads, body, 0)
    start = quads * FINAL_TILES
    rest = qi + 1 - start
    if prev_biased:
        @pl.when(rest > FINAL_TILES)
        def _():
            step(start, t)
    spill = (rest > FINAL_TILES).astype(jnp.int32)
    for width_tiles in range(1, FINAL_TILES + 1):
        @pl.when(rest - spill == width_tiles)
        def _(width_tiles=width_tiles):
            step(start + spill, width_tiles * t,
                 n_biased=min(2 if prev_biased else 1, width_tiles))
    for g in range(group):
        out = []
        for c in range(nsplit):
            acc = acc_ref[g * nsplit + c]
            out.append(acc[:, :dv] * (1.0 / acc[:, dv:dv + 1]))
        o_ref[:, g * dv:(g + 1) * dv] = jnp.concatenate(out, axis=0).astype(o_ref.dtype)


def flash_attention(q_parts, k_parts, v_part, b2, *, n_heads, name, t=ATTN_TILE, nsplit=1,
                    group=2, prev_biased=False):
    v, v_off = v_part
    S, dv = v.shape[1], v.shape[2]

    def spec(arr, off, rows):
        g = group if off is not None and arr.shape[0] > 1 else 1
        assert off is None or off % group == 0
        blk_shape = (g, t if rows else arr.shape[1], arr.shape[2])
        if g == 1:
            index = lambda h, i: (0, i if rows else 0, 0)
        else:
            index = lambda h, i: (h + off // group, i if rows else 0, 0)
        return pl.BlockSpec(blk_shape, index), math.prod(blk_shape) * arr.dtype.itemsize

    operands = ([(a, o, True) for a, o in q_parts] + [(a, o, False) for a, o in k_parts]
                + [(v, v_off, False), (b2, 0 if b2.shape[0] > 1 else None, False)])
    in_specs, blk = [], t * group * dv * 2
    for arr, off, rows in operands:
        s, nbytes = spec(arr, off, rows)
        in_specs.append(s)
        blk += nbytes
    nchain = group * nsplit
    r = t // nsplit
    return pl.pallas_call(
        functools.partial(_flash_kernel, t=t, nparts=len(q_parts), nsplit=nsplit, group=group,
                          prev_biased=prev_biased),
        grid=(n_heads // group, S // t),
        in_specs=in_specs,
        out_specs=pl.BlockSpec((t, group * dv), lambda h, i: (i, h)),
        out_shape=jax.ShapeDtypeStruct((S, n_heads * dv), BF16),
        scratch_shapes=[pltpu.VMEM((nchain, r, LANES), F32),
                        pltpu.VMEM((nchain, r, dv + LANES), F32)],
        compiler_params=_params(
            _vmem_limit(blk, nchain * r * (dv + 2 * LANES) * 4,
                        temp_bytes=6 * group * t * t * 4), 2),
        name=name,
    )(*[arr for arr, _, _ in operands])


def _moba_key_onehot(S):
    nb = S // MOBA_BLOCK
    lanes = np.arange(LANES)[None, :]
    blk_of = (np.arange(S) // MOBA_BLOCK)[:, None]
    onehot = (lanes % nb == blk_of) & (lanes < 2 * nb)
    return jnp.asarray(onehot.astype(np.float32)[None], dtype=BF16)


def _mm_res_ln_kernel(*refs, n_a, chunks):
    a_refs = refs[:n_a]
    w_ref, x_ref, g_ref, b_ref, o_ref, obf_ref = refs[n_a:]
    rc = x_ref.shape[0] // chunks
    for c in range(chunks):
        rows = slice(c * rc, (c + 1) * rc)
        y, row = None, 0
        for a_ref in a_refs:
            ka = a_ref.shape[1]
            d = jnp.dot(a_ref[rows, :], w_ref[row:row + ka, :], preferred_element_type=F32)
            y = d if y is None else y + d
            row += ka
        z = DEEPNORM_ALPHA * x_ref[rows, :] + y
        mu = jnp.mean(z, axis=-1, keepdims=True)
        zc = z - mu
        var = jnp.mean(zc * zc, axis=-1, keepdims=True)
        out = zc * lax.rsqrt(var + LN_EPS) * g_ref[...] + b_ref[...]
        o_ref[rows, :] = out
        obf_ref[rows, :] = out.astype(BF16)


def mm_res_ln(a_list, w_stack, layer, x, g, b, *, tm, chunks=2):
    M = a_list[0].shape[0]
    K = sum(a.shape[1] for a in a_list)
    N = w_stack.shape[2]
    blk = tm * K * 2 + tm * N * 4 + 2 * N * 4 + tm * N * 4 + tm * N * 2
    return pl.pallas_call(
        functools.partial(_mm_res_ln_kernel, n_a=len(a_list), chunks=chunks),
        grid=(M // tm,),
        in_specs=[pl.BlockSpec((tm, a.shape[1]), lambda i: (i, 0)) for a in a_list] + [
            pl.BlockSpec((None, K, N), lambda i: (layer, 0, 0), pipeline_mode=pl.Buffered(1)),
            pl.BlockSpec((tm, N), lambda i: (i, 0)),
            pl.BlockSpec((1, N), lambda i: (0, 0)),
            pl.BlockSpec((1, N), lambda i: (0, 0))],
        out_specs=[pl.BlockSpec((tm, N), lambda i: (i, 0)),
                   pl.BlockSpec((tm, N), lambda i: (i, 0))],
        out_shape=[jax.ShapeDtypeStruct((M, N), F32),
                   jax.ShapeDtypeStruct((M, N), BF16)],
        compiler_params=_params(_vmem_limit(blk, K * N * 2, temp_bytes=3 * tm * N * 4), 1,
                                fuse=[False] * len(a_list) + [True, False, False, False]),
        name="mm_res_ln",
    )(*a_list, w_stack, x, g, b)


def _ffn_gu_kernel(x_ref, wg_ref, wu_ref, o_ref, wgb_ref, wub_ref):
    @pl.when(pl.program_id(1) == 0)
    def _():
        wgb_ref[...] = wg_ref[...].astype(BF16)
        wub_ref[...] = wu_ref[...].astype(BF16)

    x = x_ref[...]
    g = jnp.dot(x, wgb_ref[...], preferred_element_type=F32)
    u = jnp.dot(x, wub_ref[...], preferred_element_type=F32)
    o_ref[...] = (g * jax.nn.sigmoid(g) * u).astype(o_ref.dtype)


def ffn_gate_up(xb, wg_stack, wu_stack, layer, *, tm, tf):
    M, K = xb.shape
    F = wg_stack.shape[2]
    blk = tm * K * 2 + 2 * K * tf * 4 + tm * tf * 2
    w_spec = pl.BlockSpec((None, K, tf), lambda j, i: (layer, 0, j))
    return pl.pallas_call(
        _ffn_gu_kernel,
        grid=(F // tf, M // tm),
        in_specs=[pl.BlockSpec((tm, K), lambda j, i: (i, 0)), w_spec, w_spec],
        out_specs=pl.BlockSpec((tm, tf), lambda j, i: (i, j)),
        out_shape=jax.ShapeDtypeStruct((M, F), BF16),
        scratch_shapes=[pltpu.VMEM((K, tf), BF16), pltpu.VMEM((K, tf), BF16)],
        compiler_params=_params(
            _vmem_limit(blk, 2 * K * tf * 2, temp_bytes=4 * tm * tf * 4), 2),
        name="ffn_gate_up",
    )(xb, wg_stack, wu_stack)


def _rope_lanes(r, cos_t, sin_a, sin_b):
    return (r * cos_t + pltpu.roll(r, LANES - MLA_ROPE // 2, 1) * sin_a
            + pltpu.roll(r, MLA_ROPE // 2, 1) * sin_b)


def _rms(x, g):
    return x * lax.rsqrt(jnp.mean(x * x, axis=-1, keepdims=True) + RMS_EPS) * g


def _mla_in_kernel(x_ref, w_ref, gq_ref, gkv_ref, cos_ref, sa_ref, sb_ref,
                   cq_ref, ckv_ref, kr_ref):
    half = x_ref.shape[0] // 2
    for rows in (slice(0, half), slice(half, 2 * half)):
        h = jnp.dot(x_ref[rows, :], w_ref[...], preferred_element_type=F32)
        cq_ref[rows, :] = _rms(h[:, :MLA_Q_RANK], gq_ref[...]).astype(BF16)
        ckv_ref[rows, :] = _rms(h[:, MLA_Q_RANK:MLA_Q_RANK + MLA_KV_RANK],
                                gkv_ref[...]).astype(BF16)
        r = h[:, MLA_Q_RANK + MLA_KV_RANK:]
        kr_ref[rows, :] = _rope_lanes(r, cos_ref[rows, :], sa_ref[rows, :],
                                      sb_ref[rows, :]).astype(BF16)


def mla_in(xb, w_pad, gq, gkv, cos_t, sin_a, sin_b, *, tm=512):
    S, K = xb.shape
    N = w_pad.shape[1]
    blk = (tm * K * 2 + K * N * 2 + 3 * tm * LANES * 4
           + tm * (MLA_Q_RANK + MLA_KV_RANK + LANES) * 2)
    row = lambda i: (i, 0)
    fixed = lambda i: (0, 0)
    return pl.pallas_call(
        _mla_in_kernel,
        grid=(S // tm,),
        in_specs=[pl.BlockSpec((tm, K), row), pl.BlockSpec((K, N), fixed),
                  pl.BlockSpec((1, MLA_Q_RANK), fixed), pl.BlockSpec((1, MLA_KV_RANK), fixed),
                  pl.BlockSpec((tm, LANES), row), pl.BlockSpec((tm, LANES), row),
                  pl.BlockSpec((tm, LANES), row)],
        out_specs=[pl.BlockSpec((tm, MLA_Q_RANK), row), pl.BlockSpec((tm, MLA_KV_RANK), row),
                   pl.BlockSpec((tm, LANES), row)],
        out_shape=[jax.ShapeDtypeStruct((S, MLA_Q_RANK), BF16),
                   jax.ShapeDtypeStruct((S, MLA_KV_RANK), BF16),
                   jax.ShapeDtypeStruct((S, LANES), BF16)],
        compiler_params=_params(_vmem_limit(blk, temp_bytes=3 * tm * N * 4), 1,
                                fuse=[True, True] + [False] * 5),
        name="mla_in",
    )(xb, w_pad, gq, gkv, cos_t, sin_a, sin_b)


def _mla_q_kernel(cq_ref, w_ref, cos_ref, sa_ref, sb_ref, q_ref, *, qscale):
    acc = jnp.dot(cq_ref[...], w_ref[...], preferred_element_type=F32) * qscale
    cos_t, sin_a, sin_b = cos_ref[...], sa_ref[...], sb_ref[...]
    for hh in range(q_ref.shape[0]):
        base = hh * 2 * LANES
        q_ref[hh, :, 0:LANES] = acc[:, base:base + LANES].astype(BF16)
        r = acc[:, base + LANES:base + 2 * LANES]
        q_ref[hh, :, LANES:2 * LANES] = _rope_lanes(r, cos_t, sin_a, sin_b).astype(BF16)


def mla_q(cq, w_pad, cos_t, sin_a, sin_b, *, qscale, tm=512):
    S, K = cq.shape
    N = w_pad.shape[1]
    nh = N // (2 * LANES)
    blk = tm * K * 2 + K * N * 2 + 3 * tm * LANES * 4 + tm * N * 2
    row = lambda i: (i, 0)
    return pl.pallas_call(
        functools.partial(_mla_q_kernel, qscale=qscale),
        grid=(S // tm,),
        in_specs=[pl.BlockSpec((tm, K), row), pl.BlockSpec((K, N), lambda i: (0, 0)),
                  pl.BlockSpec((tm, LANES), row), pl.BlockSpec((tm, LANES), row),
                  pl.BlockSpec((tm, LANES), row)],
        out_specs=pl.BlockSpec((nh, tm, 2 * LANES), lambda i: (0, i, 0)),
        out_shape=jax.ShapeDtypeStruct((nh, S, 2 * LANES), BF16),
        compiler_params=_params(_vmem_limit(blk, temp_bytes=2 * tm * N * 4), 1,
                                fuse=[False, True, False, False, False]),
        name="mla_q",
    )(cq, w_pad, cos_t, sin_a, sin_b)


def _mla_kv_kernel(ckv_ref, w_ref, kr_ref, k_ref, v_ref):
    acc = jnp.dot(ckv_ref[...], w_ref[...], preferred_element_type=F32)
    kr = kr_ref[...]
    for hh in range(k_ref.shape[0]):
        base = hh * (MLA_NOPE + MLA_V)
        k_ref[hh, :, 0:LANES] = acc[:, base:base + MLA_NOPE].astype(BF16)
        k_ref[hh, :, LANES:2 * LANES] = kr
        v_ref[hh] = acc[:, base + MLA_NOPE:base + MLA_NOPE + MLA_V].astype(BF16)


def mla_kv(ckv, w, kr, *, tm=512):
    S, K = ckv.shape
    N = w.shape[1]
    nh = N // (MLA_NOPE + MLA_V)
    blk = tm * K * 2 + K * N * 2 + tm * LANES * 2 + nh * tm * (2 * LANES + MLA_V) * 2
    row = lambda i: (i, 0)
    return pl.pallas_call(
        _mla_kv_kernel,
        grid=(S // tm,),
        in_specs=[pl.BlockSpec((tm, K), row), pl.BlockSpec((K, N), lambda i: (0, 0)),
                  pl.BlockSpec((tm, LANES), row)],
        out_specs=[pl.BlockSpec((nh, tm, 2 * LANES), lambda i: (0, i, 0)),
                   pl.BlockSpec((nh, tm, MLA_V), lambda i: (0, i, 0))],
        out_shape=[jax.ShapeDtypeStruct((nh, S, 2 * LANES), BF16),
                   jax.ShapeDtypeStruct((nh, S, MLA_V), BF16)],
        compiler_params=_params(_vmem_limit(blk, temp_bytes=2 * tm * N * 4), 1,
                                fuse=[False, True, False]),
        name="mla_kv",
    )(ckv, w, kr)


def _rope_lane_tables(S):
    half = MLA_ROPE // 2
    inv = ROPE_THETA ** (-np.arange(0, MLA_ROPE, 2, dtype=np.float64) / MLA_ROPE)
    ang = np.arange(S, dtype=np.float64)[:, None] * inv[None, :]
    cos, sin = np.cos(ang), np.sin(ang)
    z = np.zeros((S, half))
    z2 = np.zeros((S, LANES - MLA_ROPE))
    cos_t = np.concatenate([cos, cos, z2], axis=1)
    sin_a = np.concatenate([-sin, z, z2], axis=1)
    sin_b = np.concatenate([z, sin, z2], axis=1)
    return tuple(jnp.asarray(a.astype(np.float32)) for a in (cos_t, sin_a, sin_b))


def _pad_cols(w, n):
    return jnp.pad(w, ((0, 0), (0, n - w.shape[1])))


def kernel(x, ab_w_in, ab_forget_bias, ab_w_out, rel_bias, mla_w_in, mla_q_norm,
           mla_kv_norm, mla_w_uq, mla_w_ukv, mla_w_out, ffn_w_gate, ffn_w_up,
           ffn_w_down, ln_g, ln_b):
    S = x.shape[1]
    xf = x.reshape(S, D_MODEL)
    xb = xf.astype(BF16)
    cos_t, sin_a, sin_b = _rope_lane_tables(S)
    nf = 3 * DA + N_HEADS_A
    qs = HEAD_DIM ** -0.5 * LOG2E
    colscale = np.ones((1, 3 * DA + 3 * DB), np.float32)
    colscale[:, :DA] = qs
    colscale[:, 3 * DA:3 * DA + DB] = qs
    colscale = jnp.asarray(colscale)
    causal_b2 = _causal_bias(ATTN_TILE)
    moba_b2 = moba_bias(rel_bias, n_heads=N_HEADS_B, t=ATTN_TILE)
    moba_kaug = _moba_key_onehot(S)
    hb = 3 * N_HEADS_A
    ab_w_out_b = ab_w_out.astype(BF16)
    mla_w_out_b = mla_w_out.astype(BF16)
    ffn_w_down_b = ffn_w_down.astype(BF16)

    for layer in range(DEPTH):
        j = layer // 2
        g0, b0 = ln_g[layer, 0][None, :], ln_b[layer, 0][None, :]
        g1, b1 = ln_g[layer, 1][None, :], ln_b[layer, 1][None, :]
        if layer % 2 == 0:
            w_in = ab_w_in[j]
            w_qkv = jnp.concatenate([w_in[:, :3 * DA], w_in[:, nf:]], axis=1).astype(BF16)
            w_f = _pad_cols(w_in[:, 3 * DA:nf], LANES).astype(BF16)
            b_f = _pad_cols(ab_forget_bias[j][None, :], LANES)
            heads = mm_heads(xb, w_qkv, colscale, tm=1024, tn=1024)
            qaug, kaug = fox_gate(xb, w_f, b_f)
            ya = flash_attention([(heads, 0), (qaug, 0)], [(heads, N_HEADS_A), (kaug, 0)],
                                 (heads, 2 * N_HEADS_A), causal_b2,
                                 n_heads=N_HEADS_A, name="flash_fox")
            qaug_b = moba_gate(heads, rel_bias, n_heads=N_HEADS_B, q_off=hb,
                               k_off=hb + N_HEADS_B)
            yb = flash_attention([(heads, hb), (qaug_b, 0)],
                                 [(heads, hb + N_HEADS_B), (moba_kaug, None)],
                                 (heads, hb + 2 * N_HEADS_B), moba_b2,
                                 n_heads=N_HEADS_B, name="flash_moba", prev_biased=True)
            y, w_out = [ya, yb], ab_w_out_b
        else:
            w_in = _pad_cols(mla_w_in[j], MLA_Q_RANK + MLA_KV_RANK + LANES).astype(BF16)
            cq, ckv, kr = mla_in(xb, w_in, mla_q_norm[j][None, :], mla_kv_norm[j][None, :],
                                 cos_t, sin_a, sin_b)
            w_uq = mla_w_uq[j].reshape(MLA_Q_RANK, MLA_HEADS, MLA_NOPE + MLA_ROPE)
            w_uq = jnp.pad(w_uq, ((0, 0), (0, 0), (0, 2 * LANES - MLA_NOPE - MLA_ROPE)))
            w_uq = w_uq.reshape(MLA_Q_RANK, MLA_HEADS * 2 * LANES).astype(BF16)
            q_full = mla_q(cq, w_uq, cos_t, sin_a, sin_b,
                           qscale=(MLA_NOPE + MLA_ROPE) ** -0.5 * LOG2E)
            k_full, v = mla_kv(ckv, mla_w_ukv[j].astype(BF16), kr)
            y = [flash_attention([(q_full, 0)], [(k_full, 0)], (v, 0), causal_b2,
                                 n_heads=MLA_HEADS, name="flash_mla")]
            w_out = mla_w_out_b
        xf, xb = mm_res_ln(y, w_out, j, xf, g0, b0, tm=512)
        hmid = ffn_gate_up(xb, ffn_w_gate, ffn_w_up, layer, tm=1024, tf=512)
        xf, xb = mm_res_ln([hmid], ffn_w_down_b, layer, xf, g1, b1, tm=256)
    return xf.reshape(1, S, D_MODEL)
```

```python
import functools
import math

import numpy as np
import jax
import jax.numpy as jnp
from jax import lax
from jax.experimental import pallas as pl
from jax.experimental.pallas import tpu as pltpu

F32 = jnp.float32
BF16 = jnp.bfloat16

D_MODEL = 2048
DEPTH = 4
HEAD_DIM = 128
N_HEADS_A = 8
N_HEADS_B = 8
MOBA_BLOCK = 256
MOBA_TOPK = 3
N_BUCKETS = 32
MAX_DISTANCE = 128
MLA_HEADS = 16
MLA_Q_RANK = 512
MLA_KV_RANK = 512
MLA_NOPE = 128
MLA_ROPE = 64
MLA_V = 128
ROPE_THETA = 10000.0
DEEPNORM_ALPHA = (2 * DEPTH) ** 0.25
DA = N_HEADS_A * HEAD_DIM
DB = N_HEADS_B * HEAD_DIM
LN_EPS = 1e-5
RMS_EPS = 1e-6
LOG2E = math.log2(math.e)

LANES = 128
VMEM_BUDGET_BYTES = 56 * 2**20
VMEM_FLOOR_BYTES = 16 * 2**20
VMEM_SPILL_BYTES = 4 * 2**20
MASK_VALUE = -1e30
ATTN_TILE = 2 * MOBA_BLOCK


def _vmem_limit(block_bytes, scratch_bytes=0, temp_bytes=0):
    est = 2 * block_bytes + scratch_bytes + temp_bytes + VMEM_SPILL_BYTES
    return int(min(max(est, VMEM_FLOOR_BYTES), VMEM_BUDGET_BYTES))


def _params(vmem_bytes, ngrid):
    return pltpu.CompilerParams(
        dimension_semantics=("arbitrary",) * ngrid, vmem_limit_bytes=vmem_bytes)


def _split3(x):
    hi = x.astype(BF16).astype(F32)
    r1 = x - hi
    lo = r1.astype(BF16).astype(F32)
    return hi, lo, r1 - lo


def _qk(a, b):
    return lax.dot_general(a, b, (((1,), (1,)), ((), ())), preferred_element_type=F32)


def _mm_heads_kernel(x_ref, w_ref, cs_ref, o_ref, *, width):
    acc = jnp.dot(x_ref[...], w_ref[...], preferred_element_type=F32) * cs_ref[...]
    for hh in range(o_ref.shape[0]):
        o_ref[hh] = acc[:, hh * width:(hh + 1) * width].astype(o_ref.dtype)


def mm_heads(x, w, colscale, *, tm, tn, width=HEAD_DIM):
    M, K = x.shape
    N = w.shape[1]
    nh = tn // width
    blk = tm * K * 2 + K * tn * 2 + tm * tn * 2 + tn * 4
    return pl.pallas_call(
        functools.partial(_mm_heads_kernel, width=width),
        grid=(M // tm, N // tn),
        in_specs=[pl.BlockSpec((tm, K), lambda i, j: (i, 0)),
                  pl.BlockSpec((K, tn), lambda i, j: (0, j)),
                  pl.BlockSpec((1, tn), lambda i, j: (0, j))],
        out_specs=pl.BlockSpec((nh, tm, width), lambda i, j: (j, i, 0)),
        out_shape=jax.ShapeDtypeStruct((N // width, M, width), BF16),
        compiler_params=_params(_vmem_limit(blk, temp_bytes=2 * tm * tn * 4), 2),
        name="mm_heads",
    )(x, w, colscale)


def _fox_gate_kernel(x_ref, wf_ref, bf_ref, selq_ref, selk_ref, cq_ref, ck_ref,
                     qaug_ref, kaug_ref, carry_ref, *, tb, nh):
    @pl.when(pl.program_id(0) == 0)
    def _():
        carry_ref[...] = jnp.zeros_like(carry_ref)

    z = jnp.dot(x_ref[...], wf_ref[...], preferred_element_type=F32) + bf_ref[...]
    lf = jnp.minimum(z, 0.0) - jnp.log1p(jnp.exp(-jnp.abs(z)))
    row = lax.broadcasted_iota(jnp.int32, (tb, tb), 0)
    col = lax.broadcasted_iota(jnp.int32, (tb, tb), 1)
    tri = jnp.where(row >= col, 1.0, 0.0).astype(BF16)
    lf_terms = jnp.concatenate(_split3(lf), axis=1).astype(BF16)
    part = jnp.dot(tri, lf_terms, preferred_element_type=F32)
    cs = (part[:, :LANES] + part[:, LANES:2 * LANES] + part[:, 2 * LANES:]) + carry_ref[...]
    carry_ref[...] = cs[tb - 1:tb, :]
    lane = lax.broadcasted_iota(jnp.int32, (tb, LANES), 1)
    hi, lo, lo2 = [jnp.where(lane < nh, term, 0.0) for term in _split3(cs * LOG2E)]
    terms = (hi + pltpu.roll(lo, nh, 1) + pltpu.roll(lo2, 2 * nh, 1)).astype(BF16)
    qa = jnp.dot(terms, selq_ref[...], preferred_element_type=F32) + cq_ref[...]
    ka = jnp.dot(terms, selk_ref[...], preferred_element_type=F32) + ck_ref[...]
    for h in range(nh):
        qaug_ref[h] = qa[:, h * LANES:(h + 1) * LANES].astype(BF16)
        kaug_ref[h] = ka[:, h * LANES:(h + 1) * LANES].astype(BF16)


def _fox_selectors(nh):
    selq = np.zeros((LANES, nh * LANES), np.float32)
    selk = np.zeros((LANES, nh * LANES), np.float32)
    cq = np.zeros((1, nh * LANES), np.float32)
    ck = np.zeros((1, nh * LANES), np.float32)
    for h in range(nh):
        for term in range(3):
            selq[term * nh + h, h * LANES + term] = 1.0
            selk[term * nh + h, h * LANES + 3 + term] = -1.0
        cq[0, h * LANES + 3:h * LANES + 6] = 1.0
        ck[0, h * LANES:h * LANES + 3] = 1.0
    return (jnp.asarray(selq, dtype=BF16), jnp.asarray(selk, dtype=BF16),
            jnp.asarray(cq), jnp.asarray(ck))


def fox_gate(xb, wf_pad, bf_pad, *, tb=512, nh=N_HEADS_A):
    S, K = xb.shape
    selq, selk, cq, ck = _fox_selectors(nh)
    blk = (tb * K * 2 + K * LANES * 2 + 2 * nh * tb * LANES * 2
           + 2 * LANES * nh * LANES * 2)
    aug = jax.ShapeDtypeStruct((nh, S, LANES), BF16)
    fixed = lambda i: (0, 0)
    return pl.pallas_call(
        functools.partial(_fox_gate_kernel, tb=tb, nh=nh),
        grid=(S // tb,),
        in_specs=[pl.BlockSpec((tb, K), lambda i: (i, 0)),
                  pl.BlockSpec((K, LANES), fixed),
                  pl.BlockSpec((1, LANES), fixed),
                  pl.BlockSpec(selq.shape, fixed), pl.BlockSpec(selk.shape, fixed),
                  pl.BlockSpec(cq.shape, fixed), pl.BlockSpec(ck.shape, fixed)],
        out_specs=[pl.BlockSpec((nh, tb, LANES), lambda i: (0, i, 0)),
                   pl.BlockSpec((nh, tb, LANES), lambda i: (0, i, 0))],
        out_shape=[aug, aug],
        scratch_shapes=[pltpu.VMEM((1, LANES), F32)],
        compiler_params=_params(
            _vmem_limit(blk, temp_bytes=4 * tb * tb * 4 + 4 * tb * nh * LANES * 4), 1),
        name="fox_gate",
    )(xb, wf_pad, bf_pad, selq, selk, cq, ck)


def _moba_gate_kernel(rb_ref, q_ref, k_ref, qaug_ref, kmean_ref, *, nb, tr):
    B = MOBA_BLOCK
    h = pl.program_id(0)
    i = pl.program_id(1)
    far_bias = rb_ref[N_BUCKETS - 1, h] * LOG2E

    @pl.when(i == 0)
    def _():
        kf = k_ref[0].astype(F32).reshape(nb, B, HEAD_DIM)
        terms = _split3(jnp.sum(kf, axis=1) * (1.0 / B))
        kmean_ref[...] = jnp.concatenate(terms, axis=0).astype(BF16)

    blk = lax.broadcasted_iota(jnp.int32, (nb, tr), 0)
    blk_f = blk.astype(F32)
    own = i * (tr // B) + lax.broadcasted_iota(jnp.int32, (nb, tr), 1) // B
    g3 = _qk(kmean_ref[...], q_ref[0])
    g = g3[0:nb] + g3[nb:2 * nb] + g3[2 * nb:3 * nb]
    neg_inf = -jnp.inf
    g = jnp.where(blk < own, g, neg_inf)
    mb = jnp.where(blk == own, 0.0, MASK_VALUE).astype(F32)
    for _ in range(MOBA_TOPK):
        mx = jnp.max(g, axis=0, keepdims=True)
        hit = jnp.logical_and(g == mx, mx > neg_inf)
        idx = jnp.min(jnp.where(hit, blk_f, float(nb)), axis=0, keepdims=True)
        pick = blk_f == idx
        mb = jnp.where(pick, far_bias, mb)
        g = jnp.where(pick, neg_inf, g)
    mb_hi = mb.astype(BF16).astype(F32)
    mb_lo = jnp.where(mb > 0.5 * MASK_VALUE, mb - mb_hi, 0.0)
    aug_t = jnp.concatenate([mb_hi, mb_lo, jnp.zeros((LANES - 2 * nb, tr), F32)], axis=0)
    qaug_ref[0] = jnp.transpose(aug_t).astype(BF16)


def moba_gate(heads, rel_bias, *, n_heads, q_off, k_off, tr=1024):
    S = heads.shape[1]
    nb = S // MOBA_BLOCK
    assert 2 * nb <= LANES
    blk = tr * HEAD_DIM * 2 + S * HEAD_DIM * 2 + tr * LANES * 2
    return pl.pallas_call(
        functools.partial(_moba_gate_kernel, nb=nb, tr=tr),
        grid=(n_heads, S // tr),
        in_specs=[pl.BlockSpec(memory_space=pltpu.SMEM),
                  pl.BlockSpec((1, tr, HEAD_DIM), lambda h, i: (h + q_off, i, 0)),
                  pl.BlockSpec((1, S, HEAD_DIM), lambda h, i: (h + k_off, 0, 0))],
        out_specs=pl.BlockSpec((1, tr, LANES), lambda h, i: (h, i, 0)),
        out_shape=jax.ShapeDtypeStruct((n_heads, S, LANES), BF16),
        scratch_shapes=[pltpu.VMEM((3 * nb, HEAD_DIM), BF16)],
        compiler_params=_params(
            _vmem_limit(blk, 3 * nb * HEAD_DIM * 2,
                        temp_bytes=S * HEAD_DIM * 4 + 12 * tr * LANES * 4), 2),
        name="moba_gate",
    )(rel_bias, heads, heads)


def _moba_bias_kernel(rb_ref, own_ref, prev_ref, o_ref):
    B = MOBA_BLOCK
    h = pl.program_id(0)
    far = rb_ref[N_BUCKETS - 1, h]
    own_bucket = own_ref[...]
    prev_bucket = prev_ref[...]
    own = jnp.zeros((B, B), F32)
    prev = jnp.zeros((B, B), F32)
    for b in range(N_BUCKETS):
        val = rb_ref[b, h]
        own = jnp.where(own_bucket == b, val, own)
        prev = jnp.where(prev_bucket == b, val, prev)
    own = jnp.where(own_bucket < 0, MASK_VALUE, own * LOG2E)
    prev = (prev - far) * LOG2E
    o_ref[0] = jnp.zeros(o_ref.shape[1:], F32)
    o_ref[0, 0:B, B:2 * B] = prev
    o_ref[0, 0:B, 2 * B:3 * B] = own
    o_ref[0, B:2 * B, 2 * B:3 * B] = prev
    o_ref[0, B:2 * B, 3 * B:4 * B] = own


def _t5_bucket_table(n):
    rel = np.arange(n)
    max_exact = N_BUCKETS // 2
    nf = np.maximum(rel, 1).astype(np.float32)
    large = max_exact + (np.log(nf / np.float32(max_exact))
                         / np.float32(math.log(MAX_DISTANCE / max_exact))
                         * np.float32(N_BUCKETS - max_exact)).astype(np.int32)
    large = np.minimum(large, N_BUCKETS - 1)
    return np.where(rel < max_exact, rel, large).astype(np.int32)


def moba_bias(rel_bias, *, n_heads, t):
    B = MOBA_BLOCK
    assert t == 2 * B
    table = _t5_bucket_table(2 * B)
    rel = np.arange(B)[:, None] - np.arange(B)[None, :]
    own_bucket = np.where(rel >= 0, table[np.maximum(rel, 0)], -1).astype(np.int32)
    prev_bucket = table[rel + B].astype(np.int32)
    blk = t * 2 * t * 4 + 2 * B * B * 4
    return pl.pallas_call(
        _moba_bias_kernel,
        grid=(n_heads,),
        in_specs=[pl.BlockSpec(memory_space=pltpu.SMEM),
                  pl.BlockSpec((B, B), lambda h: (0, 0)),
                  pl.BlockSpec((B, B), lambda h: (0, 0))],
        out_specs=pl.BlockSpec((1, t, 2 * t), lambda h: (h, 0, 0)),
        out_shape=jax.ShapeDtypeStruct((n_heads, t, 2 * t), F32),
        compiler_params=_params(_vmem_limit(blk, temp_bytes=8 * B * B * 4), 1),
        name="moba_bias",
    )(rel_bias, jnp.asarray(own_bucket), jnp.asarray(prev_bucket))


def _causal_bias(t):
    rows = np.arange(t)[:, None]
    cols = np.arange(2 * t)[None, :] - t
    return jnp.asarray(np.where(cols <= rows, 0.0, MASK_VALUE).astype(np.float32)[None])


FINAL_TILES = 4


def _ones_column(rows):
    lane = lax.broadcasted_iota(jnp.int32, (rows, LANES), 1)
    return jnp.where(lane == 0, 1.0, 0.0).astype(BF16)


def _flash_kernel(*refs, t, nparts, nsplit, group, prev_biased):
    q_refs = refs[:nparts]
    k_refs = refs[nparts:2 * nparts]
    v_ref, b2_ref, o_ref, m_ref, acc_ref = refs[2 * nparts:]
    qi = pl.program_id(1)
    dv = v_ref.shape[-1]
    r = t // nsplit
    chains = [(g, c) for g in range(group) for c in range(nsplit)]

    def head(ref, g):
        return g if ref.shape[0] == group else 0

    qs = [jnp.concatenate([ref[head(ref, g), c * r:(c + 1) * r, :] for ref in q_refs], axis=1)
          for g, c in chains]
    m_ref[...] = jnp.full_like(m_ref, MASK_VALUE)
    acc_ref[...] = jnp.zeros_like(acc_ref)

    def step(tile, width, n_biased=0):
        ks = pl.multiple_of(tile * t, t)
        ones = _ones_column(width)
        bw = n_biased * t
        ss = []
        for n, (g, c) in enumerate(chains):
            k = jnp.concatenate([ref[head(ref, g), pl.ds(ks, width), :] for ref in k_refs],
                                axis=1)
            s = _qk(qs[n], k)
            if bw:
                bias = b2_ref[head(b2_ref, g), c * r:(c + 1) * r, 2 * t - bw:2 * t]
                tail = s[:, width - bw:] + bias
                s = tail if bw == width else jnp.concatenate([s[:, :width - bw], tail], axis=1)
            ss.append(s)
        for n, (g, c) in enumerate(chains):
            v1 = jnp.concatenate([v_ref[g, pl.ds(ks, width), :], ones], axis=1)
            m = m_ref[n]
            m_new = jnp.maximum(m, jnp.max(ss[n], axis=1, keepdims=True))
            p = jnp.exp2(ss[n] - jnp.concatenate([m_new] * (width // LANES), axis=1))
            a = jnp.exp2(m - m_new)
            acc_ref[n] = (jnp.concatenate([a] * ((dv + LANES) // LANES), axis=1) * acc_ref[n]
                          + jnp.dot(p.astype(BF16), v1, preferred_element_type=F32))
            m_ref[n] = m_new

    n_biased = jnp.minimum(2 if prev_biased else 1, qi + 1)
    lead = qi + 1 - n_biased
    quads = lead // FINAL_TILES

    def body(kb, carry):
        step(FINAL_TILES * kb, FINAL_TILES * t)
        return carry

    lax.fori_loop(0, quads, body, 0)
    start = quads * FINAL_TILES
    rest = qi + 1 - start
    if prev_biased:
        @pl.when(rest > FINAL_TILES)
        def _():
            step(start, t)
    spill = (rest > FINAL_TILES).astype(jnp.int32)
    for width_tiles in range(1, FINAL_TILES + 1):
        @pl.when(rest - spill == width_tiles)
        def _(width_tiles=width_tiles):
            step(start + spill, width_tiles * t,
                 n_biased=min(2 if prev_biased else 1, width_tiles))
    for g in range(group):
        out = []
        for c in range(nsplit):
            acc = acc_ref[g * nsplit + c]
            out.append(acc[:, :dv] * (1.0 / acc[:, dv:dv + 1]))
        o_ref[:, g * dv:(g + 1) * dv] = jnp.concatenate(out, axis=0).astype(o_ref.dtype)


def flash_attention(q_parts, k_parts, v_part, b2, *, n_heads, name, t=ATTN_TILE, nsplit=1,
                    group=2, prev_biased=False):
    v, v_off = v_part
    S, dv = v.shape[1], v.shape[2]

    def spec(arr, off, rows):
        g = group if off is not None and arr.shape[0] > 1 else 1
        assert off is None or off % group == 0
        blk_shape = (g, t if rows else arr.shape[1], arr.shape[2])
        if g == 1:
            index = lambda h, i: (0, i if rows else 0, 0)
        else:
            index = lambda h, i: (h + off // group, i if rows else 0, 0)
        return pl.BlockSpec(blk_shape, index), math.prod(blk_shape) * arr.dtype.itemsize

    operands = ([(a, o, True) for a, o in q_parts] + [(a, o, False) for a, o in k_parts]
                + [(v, v_off, False), (b2, 0 if b2.shape[0] > 1 else None, False)])
    in_specs, blk = [], t * group * dv * 2
    for arr, off, rows in operands:
        s, nbytes = spec(arr, off, rows)
        in_specs.append(s)
        blk += nbytes
    nchain = group * nsplit
    r = t // nsplit
    return pl.pallas_call(
        functools.partial(_flash_kernel, t=t, nparts=len(q_parts), nsplit=nsplit, group=group,
                          prev_biased=prev_biased),
        grid=(n_heads // group, S // t),
        in_specs=in_specs,
        out_specs=pl.BlockSpec((t, group * dv), lambda h, i: (i, h)),
        out_shape=jax.ShapeDtypeStruct((S, n_heads * dv), BF16),
        scratch_shapes=[pltpu.VMEM((nchain, r, LANES), F32),
                        pltpu.VMEM((nchain, r, dv + LANES), F32)],
        compiler_params=_params(
            _vmem_limit(blk, nchain * r * (dv + 2 * LANES) * 4,
                        temp_bytes=6 * group * t * t * 4), 2),
        name=name,
    )(*[arr for arr, _, _ in operands])


def _moba_key_onehot(S):
    nb = S // MOBA_BLOCK
    lanes = np.arange(LANES)[None, :]
    blk_of = (np.arange(S) // MOBA_BLOCK)[:, None]
    onehot = (lanes % nb == blk_of) & (lanes < 2 * nb)
    return jnp.asarray(onehot.astype(np.float32)[None], dtype=BF16)


def _mm_res_ln_kernel(*refs, n_a, chunks, layer, kc):
    a_refs = refs[:n_a]
    w_hbm, x_ref, g_ref, b_ref, o_ref, obf_ref, w_ref, stage_ref, sem = refs[n_a:]

    @pl.when(pl.program_id(0) == 0)
    def _():
        n_chunks = w_ref.shape[0] // kc

        def chunk_copy(c):
            return pltpu.make_async_copy(w_hbm.at[layer, pl.ds(c * kc, kc), :],
                                         stage_ref.at[c % 2], sem.at[c % 2])

        chunk_copy(0).start()
        for c in range(n_chunks):
            if c + 1 < n_chunks:
                chunk_copy(c + 1).start()
            chunk_copy(c).wait()
            w_ref[c * kc:(c + 1) * kc, :] = stage_ref[c % 2].astype(BF16)

    rc = x_ref.shape[0] // chunks
    for c in range(chunks):
        rows = slice(c * rc, (c + 1) * rc)
        y, row = None, 0
        for a_ref in a_refs:
            ka = a_ref.shape[1]
            d = jnp.dot(a_ref[rows, :], w_ref[row:row + ka, :], preferred_element_type=F32)
            y = d if y is None else y + d
            row += ka
        z = DEEPNORM_ALPHA * x_ref[rows, :] + y
        mu = jnp.mean(z, axis=-1, keepdims=True)
        zc = z - mu
        var = jnp.mean(zc * zc, axis=-1, keepdims=True)
        out = zc * lax.rsqrt(var + LN_EPS) * g_ref[...] + b_ref[...]
        o_ref[rows, :] = out
        obf_ref[rows, :] = out.astype(BF16)


def mm_res_ln(a_list, w_stack, layer, x, g, b, *, tm, chunks=2, kc=512):
    M = a_list[0].shape[0]
    K = sum(a.shape[1] for a in a_list)
    N = w_stack.shape[2]
    assert w_stack.shape[1] == K and K % kc == 0
    blk = tm * K * 2 + tm * N * 4 + 2 * N * 4 + tm * N * 4 + tm * N * 2
    return pl.pallas_call(
        functools.partial(_mm_res_ln_kernel, n_a=len(a_list), chunks=chunks, layer=layer,
                          kc=kc),
        grid=(M // tm,),
        in_specs=[pl.BlockSpec((tm, a.shape[1]), lambda i: (i, 0)) for a in a_list] + [
            pl.BlockSpec(memory_space=pl.ANY),
            pl.BlockSpec((tm, N), lambda i: (i, 0)),
            pl.BlockSpec((1, N), lambda i: (0, 0)),
            pl.BlockSpec((1, N), lambda i: (0, 0))],
        out_specs=[pl.BlockSpec((tm, N), lambda i: (i, 0)),
                   pl.BlockSpec((tm, N), lambda i: (i, 0))],
        out_shape=[jax.ShapeDtypeStruct((M, N), F32),
                   jax.ShapeDtypeStruct((M, N), BF16)],
        scratch_shapes=[pltpu.VMEM((K, N), BF16), pltpu.VMEM((2, kc, N), F32),
                        pltpu.SemaphoreType.DMA((2,))],
        compiler_params=_params(
            _vmem_limit(blk, K * N * 2 + 2 * kc * N * 4, temp_bytes=3 * tm * N * 4), 1),
        name="mm_res_ln",
    )(*a_list, w_stack, x, g, b)


def _ffn_gu_kernel(x_ref, wg_ref, wu_ref, o_ref, wgb_ref, wub_ref):
    @pl.when(pl.program_id(1) == 0)
    def _():
        wgb_ref[...] = wg_ref[...].astype(BF16)
        wub_ref[...] = wu_ref[...].astype(BF16)

    x = x_ref[...]
    g = jnp.dot(x, wgb_ref[...], preferred_element_type=F32)
    u = jnp.dot(x, wub_ref[...], preferred_element_type=F32)
    o_ref[...] = (g * jax.nn.sigmoid(g) * u).astype(o_ref.dtype)


def ffn_gate_up(xb, wg_stack, wu_stack, layer, *, tm, tf):
    M, K = xb.shape
    F = wg_stack.shape[2]
    blk = tm * K * 2 + 2 * K * tf * 4 + tm * tf * 2
    w_spec = pl.BlockSpec((None, K, tf), lambda j, i: (layer, 0, j))
    return pl.pallas_call(
        _ffn_gu_kernel,
        grid=(F // tf, M // tm),
        in_specs=[pl.BlockSpec((tm, K), lambda j, i: (i, 0)), w_spec, w_spec],
        out_specs=pl.BlockSpec((tm, tf), lambda j, i: (i, j)),
        out_shape=jax.ShapeDtypeStruct((M, F), BF16),
        scratch_shapes=[pltpu.VMEM((K, tf), BF16), pltpu.VMEM((K, tf), BF16)],
        compiler_params=_params(
            _vmem_limit(blk, 2 * K * tf * 2, temp_bytes=4 * tm * tf * 4), 2),
        name="ffn_gate_up",
    )(xb, wg_stack, wu_stack)


def _rope_lanes(r, cos_t, sin_a, sin_b):
    return (r * cos_t + pltpu.roll(r, LANES - MLA_ROPE // 2, 1) * sin_a
            + pltpu.roll(r, MLA_ROPE // 2, 1) * sin_b)


def _rms(x, g):
    return x * lax.rsqrt(jnp.mean(x * x, axis=-1, keepdims=True) + RMS_EPS) * g


def _mla_in_kernel(x_ref, w_ref, gq_ref, gkv_ref, cos_ref, sa_ref, sb_ref,
                   cq_ref, ckv_ref, kr_ref):
    half = x_ref.shape[0] // 2
    for rows in (slice(0, half), slice(half, 2 * half)):
        h = jnp.dot(x_ref[rows, :], w_ref[...], preferred_element_type=F32)
        cq_ref[rows, :] = _rms(h[:, :MLA_Q_RANK], gq_ref[...]).astype(BF16)
        ckv_ref[rows, :] = _rms(h[:, MLA_Q_RANK:MLA_Q_RANK + MLA_KV_RANK],
                                gkv_ref[...]).astype(BF16)
        r = h[:, MLA_Q_RANK + MLA_KV_RANK:]
        kr_ref[rows, :] = _rope_lanes(r, cos_ref[rows, :], sa_ref[rows, :],
                                      sb_ref[rows, :]).astype(BF16)


def mla_in(xb, w_pad, gq, gkv, cos_t, sin_a, sin_b, *, tm=512):
    S, K = xb.shape
    N = w_pad.shape[1]
    blk = (tm * K * 2 + K * N * 2 + 3 * tm * LANES * 4
           + tm * (MLA_Q_RANK + MLA_KV_RANK + LANES) * 2)
    row = lambda i: (i, 0)
    fixed = lambda i: (0, 0)
    return pl.pallas_call(
        _mla_in_kernel,
        grid=(S // tm,),
        in_specs=[pl.BlockSpec((tm, K), row), pl.BlockSpec((K, N), fixed),
                  pl.BlockSpec((1, MLA_Q_RANK), fixed), pl.BlockSpec((1, MLA_KV_RANK), fixed),
                  pl.BlockSpec((tm, LANES), row), pl.BlockSpec((tm, LANES), row),
                  pl.BlockSpec((tm, LANES), row)],
        out_specs=[pl.BlockSpec((tm, MLA_Q_RANK), row), pl.BlockSpec((tm, MLA_KV_RANK), row),
                   pl.BlockSpec((tm, LANES), row)],
        out_shape=[jax.ShapeDtypeStruct((S, MLA_Q_RANK), BF16),
                   jax.ShapeDtypeStruct((S, MLA_KV_RANK), BF16),
                   jax.ShapeDtypeStruct((S, LANES), BF16)],
        compiler_params=_params(_vmem_limit(blk, temp_bytes=3 * tm * N * 4), 1),
        name="mla_in",
    )(xb, w_pad, gq, gkv, cos_t, sin_a, sin_b)


def _mla_q_kernel(cq_ref, w_ref, cos_ref, sa_ref, sb_ref, q_ref, *, qscale):
    acc = jnp.dot(cq_ref[...], w_ref[...], preferred_element_type=F32) * qscale
    cos_t, sin_a, sin_b = cos_ref[...], sa_ref[...], sb_ref[...]
    for hh in range(q_ref.shape[0]):
        base = hh * 2 * LANES
        q_ref[hh, :, 0:LANES] = acc[:, base:base + LANES].astype(BF16)
        r = acc[:, base + LANES:base + 2 * LANES]
        q_ref[hh, :, LANES:2 * LANES] = _rope_lanes(r, cos_t, sin_a, sin_b).astype(BF16)


def mla_q(cq, w_pad, cos_t, sin_a, sin_b, *, qscale, tm=512):
    S, K = cq.shape
    N = w_pad.shape[1]
    nh = N // (2 * LANES)
    blk = tm * K * 2 + K * N * 2 + 3 * tm * LANES * 4 + tm * N * 2
    row = lambda i: (i, 0)
    return pl.pallas_call(
        functools.partial(_mla_q_kernel, qscale=qscale),
        grid=(S // tm,),
        in_specs=[pl.BlockSpec((tm, K), row), pl.BlockSpec((K, N), lambda i: (0, 0)),
                  pl.BlockSpec((tm, LANES), row), pl.BlockSpec((tm, LANES), row),
                  pl.BlockSpec((tm, LANES), row)],
        out_specs=pl.BlockSpec((nh, tm, 2 * LANES), lambda i: (0, i, 0)),
        out_shape=jax.ShapeDtypeStruct((nh, S, 2 * LANES), BF16),
        compiler_params=_params(_vmem_limit(blk, temp_bytes=2 * tm * N * 4), 1),
        name="mla_q",
    )(cq, w_pad, cos_t, sin_a, sin_b)


def _mla_kv_kernel(ckv_ref, w_ref, kr_ref, k_ref, v_ref):
    acc = jnp.dot(ckv_ref[...], w_ref[...], preferred_element_type=F32)
    kr = kr_ref[...]
    for hh in range(k_ref.shape[0]):
        base = hh * (MLA_NOPE + MLA_V)
        k_ref[hh, :, 0:LANES] = acc[:, base:base + MLA_NOPE].astype(BF16)
        k_ref[hh, :, LANES:2 * LANES] = kr
        v_ref[hh] = acc[:, base + MLA_NOPE:base + MLA_NOPE + MLA_V].astype(BF16)


def mla_kv(ckv, w, kr, *, tm=512):
    S, K = ckv.shape
    N = w.shape[1]
    nh = N // (MLA_NOPE + MLA_V)
    blk = tm * K * 2 + K * N * 2 + tm * LANES * 2 + nh * tm * (2 * LANES + MLA_V) * 2
    row = lambda i: (i, 0)
    return pl.pallas_call(
        _mla_kv_kernel,
        grid=(S // tm,),
        in_specs=[pl.BlockSpec((tm, K), row), pl.BlockSpec((K, N), lambda i: (0, 0)),
                  pl.BlockSpec((tm, LANES), row)],
        out_specs=[pl.BlockSpec((nh, tm, 2 * LANES), lambda i: (0, i, 0)),
                   pl.BlockSpec((nh, tm, MLA_V), lambda i: (0, i, 0))],
        out_shape=[jax.ShapeDtypeStruct((nh, S, 2 * LANES), BF16),
                   jax.ShapeDtypeStruct((nh, S, MLA_V), BF16)],
        compiler_params=_params(_vmem_limit(blk, temp_bytes=2 * tm * N * 4), 1),
        name="mla_kv",
    )(ckv, w, kr)


def _rope_lane_tables(S):
    half = MLA_ROPE // 2
    inv = ROPE_THETA ** (-np.arange(0, MLA_ROPE, 2, dtype=np.float64) / MLA_ROPE)
    ang = np.arange(S, dtype=np.float64)[:, None] * inv[None, :]
    cos, sin = np.cos(ang), np.sin(ang)
    z = np.zeros((S, half))
    z2 = np.zeros((S, LANES - MLA_ROPE))
    cos_t = np.concatenate([cos, cos, z2], axis=1)
    sin_a = np.concatenate([-sin, z, z2], axis=1)
    sin_b = np.concatenate([z, sin, z2], axis=1)
    return tuple(jnp.asarray(a.astype(np.float32)) for a in (cos_t, sin_a, sin_b))


def _pad_cols(w, n):
    return jnp.pad(w, ((0, 0), (0, n - w.shape[1])))


def kernel(x, ab_w_in, ab_forget_bias, ab_w_out, rel_bias, mla_w_in, mla_q_norm,
           mla_kv_norm, mla_w_uq, mla_w_ukv, mla_w_out, ffn_w_gate, ffn_w_up,
           ffn_w_down, ln_g, ln_b):
    S = x.shape[1]
    xf = x.reshape(S, D_MODEL)
    xb = xf.astype(BF16)
    cos_t, sin_a, sin_b = _rope_lane_tables(S)
    nf = 3 * DA + N_HEADS_A
    qs = HEAD_DIM ** -0.5 * LOG2E
    colscale = np.ones((1, 3 * DA + 3 * DB), np.float32)
    colscale[:, :DA] = qs
    colscale[:, 3 * DA:3 * DA + DB] = qs
    colscale = jnp.asarray(colscale)
    causal_b2 = _causal_bias(ATTN_TILE)
    moba_b2 = moba_bias(rel_bias, n_heads=N_HEADS_B, t=ATTN_TILE)
    moba_kaug = _moba_key_onehot(S)
    hb = 3 * N_HEADS_A

    for layer in range(DEPTH):
        j = layer // 2
        g0, b0 = ln_g[layer, 0][None, :], ln_b[layer, 0][None, :]
        g1, b1 = ln_g[layer, 1][None, :], ln_b[layer, 1][None, :]
        if layer % 2 == 0:
            w_in = ab_w_in[j]
            w_qkv = jnp.concatenate([w_in[:, :3 * DA], w_in[:, nf:]], axis=1).astype(BF16)
            w_f = _pad_cols(w_in[:, 3 * DA:nf], LANES).astype(BF16)
            b_f = _pad_cols(ab_forget_bias[j][None, :], LANES)
            heads = mm_heads(xb, w_qkv, colscale, tm=1024, tn=1024)
            qaug, kaug = fox_gate(xb, w_f, b_f)
            ya = flash_attention([(heads, 0), (qaug, 0)], [(heads, N_HEADS_A), (kaug, 0)],
                                 (heads, 2 * N_HEADS_A), causal_b2,
                                 n_heads=N_HEADS_A, name="flash_fox")
            qaug_b = moba_gate(heads, rel_bias, n_heads=N_HEADS_B, q_off=hb,
                               k_off=hb + N_HEADS_B)
            yb = flash_attention([(heads, hb), (qaug_b, 0)],
                                 [(heads, hb + N_HEADS_B), (moba_kaug, None)],
                                 (heads, hb + 2 * N_HEADS_B), moba_b2,
                                 n_heads=N_HEADS_B, name="flash_moba", prev_biased=True)
            y, w_out = [ya, yb], ab_w_out
        else:
            w_in = _pad_cols(mla_w_in[j], MLA_Q_RANK + MLA_KV_RANK + LANES).astype(BF16)
            cq, ckv, kr = mla_in(xb, w_in, mla_q_norm[j][None, :], mla_kv_norm[j][None, :],
                                 cos_t, sin_a, sin_b)
            w_uq = mla_w_uq[j].reshape(MLA_Q_RANK, MLA_HEADS, MLA_NOPE + MLA_ROPE)
            w_uq = jnp.pad(w_uq, ((0, 0), (0, 0), (0, 2 * LANES - MLA_NOPE - MLA_ROPE)))
            w_uq = w_uq.reshape(MLA_Q_RANK, MLA_HEADS * 2 * LANES).astype(BF16)
            q_full = mla_q(cq, w_uq, cos_t, sin_a, sin_b,
                           qscale=(MLA_NOPE + MLA_ROPE) ** -0.5 * LOG2E)
            k_full, v = mla_kv(ckv, mla_w_ukv[j].astype(BF16), kr)
            y = [flash_attention([(q_full, 0)], [(k_full, 0)], (v, 0), causal_b2,
                                 n_heads=MLA_HEADS, name="flash_mla")]
            w_out = mla_w_out
        xf, xb = mm_res_ln(y, w_out, j, xf, g0, b0, tm=512)
        hmid = ffn_gate_up(xb, ffn_w_gate, ffn_w_up, layer, tm=1024, tf=512)
        xf, xb = mm_res_ln([hmid], ffn_w_down, layer, xf, g1, b1, tm=256)
    return xf.reshape(1, S, D_MODEL)
```

```python
import functools
import math

import numpy as np
import jax
import jax.numpy as jnp
from jax import lax
from jax.experimental import pallas as pl
from jax.experimental.pallas import tpu as pltpu

F32 = jnp.float32
BF16 = jnp.bfloat16

D_MODEL = 2048
DEPTH = 4
HEAD_DIM = 128
N_HEADS_A = 8
N_HEADS_B = 8
MOBA_BLOCK = 256
MOBA_TOPK = 3
N_BUCKETS = 32
MAX_DISTANCE = 128
MLA_HEADS = 16
MLA_Q_RANK = 512
MLA_KV_RANK = 512
MLA_NOPE = 128
MLA_ROPE = 64
MLA_V = 128
ROPE_THETA = 10000.0
DEEPNORM_ALPHA = (2 * DEPTH) ** 0.25
DA = N_HEADS_A * HEAD_DIM
DB = N_HEADS_B * HEAD_DIM
LN_EPS = 1e-5
RMS_EPS = 1e-6
LOG2E = math.log2(math.e)

LANES = 128
VMEM_BUDGET_BYTES = 56 * 2**20
VMEM_FLOOR_BYTES = 16 * 2**20
VMEM_SPILL_BYTES = 4 * 2**20
MASK_VALUE = -1e30
ATTN_TILE = 2 * MOBA_BLOCK


def _vmem_limit(block_bytes, scratch_bytes=0, temp_bytes=0):
    est = 2 * block_bytes + scratch_bytes + temp_bytes + VMEM_SPILL_BYTES
    return int(min(max(est, VMEM_FLOOR_BYTES), VMEM_BUDGET_BYTES))


def _params(vmem_bytes, ngrid):
    return pltpu.CompilerParams(
        dimension_semantics=("arbitrary",) * ngrid, vmem_limit_bytes=vmem_bytes)


def _split3(x):
    hi = x.astype(BF16).astype(F32)
    r1 = x - hi
    lo = r1.astype(BF16).astype(F32)
    return hi, lo, r1 - lo


def _qk(a, b):
    return lax.dot_general(a, b, (((1,), (1,)), ((), ())), preferred_element_type=F32)


def _mm_heads_kernel(x_ref, w_ref, cs_ref, o_ref, *, width):
    acc = jnp.dot(x_ref[...], w_ref[...], preferred_element_type=F32) * cs_ref[...]
    for hh in range(o_ref.shape[0]):
        o_ref[hh] = acc[:, hh * width:(hh + 1) * width].astype(o_ref.dtype)


def mm_heads(x, w, colscale, *, tm, tn, width=HEAD_DIM):
    M, K = x.shape
    N = w.shape[1]
    nh = tn // width
    blk = tm * K * 2 + K * tn * 2 + tm * tn * 2 + tn * 4
    return pl.pallas_call(
        functools.partial(_mm_heads_kernel, width=width),
        grid=(M // tm, N // tn),
        in_specs=[pl.BlockSpec((tm, K), lambda i, j: (i, 0)),
                  pl.BlockSpec((K, tn), lambda i, j: (0, j)),
                  pl.BlockSpec((1, tn), lambda i, j: (0, j))],
        out_specs=pl.BlockSpec((nh, tm, width), lambda i, j: (j, i, 0)),
        out_shape=jax.ShapeDtypeStruct((N // width, M, width), BF16),
        compiler_params=_params(_vmem_limit(blk, temp_bytes=2 * tm * tn * 4), 2),
        name="mm_heads",
    )(x, w, colscale)


def _fox_gate_kernel(x_ref, wf_ref, bf_ref, selq_ref, selk_ref, cq_ref, ck_ref,
                     qaug_ref, kaug_ref, carry_ref, *, tb, nh):
    @pl.when(pl.program_id(0) == 0)
    def _():
        carry_ref[...] = jnp.zeros_like(carry_ref)

    z = jnp.dot(x_ref[...], wf_ref[...], preferred_element_type=F32) + bf_ref[...]
    lf = jnp.minimum(z, 0.0) - jnp.log1p(jnp.exp(-jnp.abs(z)))
    row = lax.broadcasted_iota(jnp.int32, (tb, tb), 0)
    col = lax.broadcasted_iota(jnp.int32, (tb, tb), 1)
    tri = jnp.where(row >= col, 1.0, 0.0).astype(BF16)
    lf_terms = jnp.concatenate(_split3(lf), axis=1).astype(BF16)
    part = jnp.dot(tri, lf_terms, preferred_element_type=F32)
    cs = (part[:, :LANES] + part[:, LANES:2 * LANES] + part[:, 2 * LANES:]) + carry_ref[...]
    carry_ref[...] = cs[tb - 1:tb, :]
    lane = lax.broadcasted_iota(jnp.int32, (tb, LANES), 1)
    hi, lo, lo2 = [jnp.where(lane < nh, term, 0.0) for term in _split3(cs * LOG2E)]
    terms = (hi + pltpu.roll(lo, nh, 1) + pltpu.roll(lo2, 2 * nh, 1)).astype(BF16)
    qa = jnp.dot(terms, selq_ref[...], preferred_element_type=F32) + cq_ref[...]
    ka = jnp.dot(terms, selk_ref[...], preferred_element_type=F32) + ck_ref[...]
    for h in range(nh):
        qaug_ref[h] = qa[:, h * LANES:(h + 1) * LANES].astype(BF16)
        kaug_ref[h] = ka[:, h * LANES:(h + 1) * LANES].astype(BF16)


def _fox_selectors(nh):
    selq = np.zeros((LANES, nh * LANES), np.float32)
    selk = np.zeros((LANES, nh * LANES), np.float32)
    cq = np.zeros((1, nh * LANES), np.float32)
    ck = np.zeros((1, nh * LANES), np.float32)
    for h in range(nh):
        for term in range(3):
            selq[term * nh + h, h * LANES + term] = 1.0
            selk[term * nh + h, h * LANES + 3 + term] = -1.0
        cq[0, h * LANES + 3:h * LANES + 6] = 1.0
        ck[0, h * LANES:h * LANES + 3] = 1.0
    return (jnp.asarray(selq, dtype=BF16), jnp.asarray(selk, dtype=BF16),
            jnp.asarray(cq), jnp.asarray(ck))


def fox_gate(xb, wf_pad, bf_pad, *, tb=512, nh=N_HEADS_A):
    S, K = xb.shape
    selq, selk, cq, ck = _fox_selectors(nh)
    blk = (tb * K * 2 + K * LANES * 2 + 2 * nh * tb * LANES * 2
           + 2 * LANES * nh * LANES * 2)
    aug = jax.ShapeDtypeStruct((nh, S, LANES), BF16)
    fixed = lambda i: (0, 0)
    return pl.pallas_call(
        functools.partial(_fox_gate_kernel, tb=tb, nh=nh),
        grid=(S // tb,),
        in_specs=[pl.BlockSpec((tb, K), lambda i: (i, 0)),
                  pl.BlockSpec((K, LANES), fixed),
                  pl.BlockSpec((1, LANES), fixed),
                  pl.BlockSpec(selq.shape, fixed), pl.BlockSpec(selk.shape, fixed),
                  pl.BlockSpec(cq.shape, fixed), pl.BlockSpec(ck.shape, fixed)],
        out_specs=[pl.BlockSpec((nh, tb, LANES), lambda i: (0, i, 0)),
                   pl.BlockSpec((nh, tb, LANES), lambda i: (0, i, 0))],
        out_shape=[aug, aug],
        scratch_shapes=[pltpu.VMEM((1, LANES), F32)],
        compiler_params=_params(
            _vmem_limit(blk, temp_bytes=4 * tb * tb * 4 + 4 * tb * nh * LANES * 4), 1),
        name="fox_gate",
    )(xb, wf_pad, bf_pad, selq, selk, cq, ck)


def _moba_gate_kernel(rb_ref, q_ref, k_ref, qaug_ref, kmean_ref, *, nb, tr):
    B = MOBA_BLOCK
    h = pl.program_id(0)
    i = pl.program_id(1)
    far_bias = rb_ref[N_BUCKETS - 1, h] * LOG2E

    @pl.when(i == 0)
    def _():
        kf = k_ref[0].astype(F32).reshape(nb, B, HEAD_DIM)
        terms = _split3(jnp.sum(kf, axis=1) * (1.0 / B))
        kmean_ref[...] = jnp.concatenate(terms, axis=0).astype(BF16)

    blk = lax.broadcasted_iota(jnp.int32, (nb, tr), 0)
    blk_f = blk.astype(F32)
    own = i * (tr // B) + lax.broadcasted_iota(jnp.int32, (nb, tr), 1) // B
    g3 = _qk(kmean_ref[...], q_ref[0])
    g = g3[0:nb] + g3[nb:2 * nb] + g3[2 * nb:3 * nb]
    neg_inf = -jnp.inf
    g = jnp.where(blk < own, g, neg_inf)
    mb = jnp.where(blk == own, 0.0, MASK_VALUE).astype(F32)
    for _ in range(MOBA_TOPK):
        mx = jnp.max(g, axis=0, keepdims=True)
        hit = jnp.logical_and(g == mx, mx > neg_inf)
        idx = jnp.min(jnp.where(hit, blk_f, float(nb)), axis=0, keepdims=True)
        pick = blk_f == idx
        mb = jnp.where(pick, far_bias, mb)
        g = jnp.where(pick, neg_inf, g)
    mb_hi = mb.astype(BF16).astype(F32)
    mb_lo = jnp.where(mb > 0.5 * MASK_VALUE, mb - mb_hi, 0.0)
    aug_t = jnp.concatenate([mb_hi, mb_lo, jnp.zeros((LANES - 2 * nb, tr), F32)], axis=0)
    qaug_ref[0] = jnp.transpose(aug_t).astype(BF16)


def moba_gate(heads, rel_bias, *, n_heads, q_off, k_off, tr=1024):
    S = heads.shape[1]
    nb = S // MOBA_BLOCK
    assert 2 * nb <= LANES
    blk = tr * HEAD_DIM * 2 + S * HEAD_DIM * 2 + tr * LANES * 2
    return pl.pallas_call(
        functools.partial(_moba_gate_kernel, nb=nb, tr=tr),
        grid=(n_heads, S // tr),
        in_specs=[pl.BlockSpec(memory_space=pltpu.SMEM),
                  pl.BlockSpec((1, tr, HEAD_DIM), lambda h, i: (h + q_off, i, 0)),
                  pl.BlockSpec((1, S, HEAD_DIM), lambda h, i: (h + k_off, 0, 0))],
        out_specs=pl.BlockSpec((1, tr, LANES), lambda h, i: (h, i, 0)),
        out_shape=jax.ShapeDtypeStruct((n_heads, S, LANES), BF16),
        scratch_shapes=[pltpu.VMEM((3 * nb, HEAD_DIM), BF16)],
        compiler_params=_params(
            _vmem_limit(blk, 3 * nb * HEAD_DIM * 2,
                        temp_bytes=S * HEAD_DIM * 4 + 12 * tr * LANES * 4), 2),
        name="moba_gate",
    )(rel_bias, heads, heads)


def _moba_bias_kernel(rb_ref, own_ref, prev_ref, o_ref):
    B = MOBA_BLOCK
    h = pl.program_id(0)
    far = rb_ref[N_BUCKETS - 1, h]
    own_bucket = own_ref[...]
    prev_bucket = prev_ref[...]
    own = jnp.zeros((B, B), F32)
    prev = jnp.zeros((B, B), F32)
    for b in range(N_BUCKETS):
        val = rb_ref[b, h]
        own = jnp.where(own_bucket == b, val, own)
        prev = jnp.where(prev_bucket == b, val, prev)
    own = jnp.where(own_bucket < 0, MASK_VALUE, own * LOG2E)
    prev = (prev - far) * LOG2E
    o_ref[0] = jnp.zeros(o_ref.shape[1:], F32)
    o_ref[0, 0:B, B:2 * B] = prev
    o_ref[0, 0:B, 2 * B:3 * B] = own
    o_ref[0, B:2 * B, 2 * B:3 * B] = prev
    o_ref[0, B:2 * B, 3 * B:4 * B] = own


def _t5_bucket_table(n):
    rel = np.arange(n)
    max_exact = N_BUCKETS // 2
    nf = np.maximum(rel, 1).astype(np.float32)
    large = max_exact + (np.log(nf / np.float32(max_exact))
                         / np.float32(math.log(MAX_DISTANCE / max_exact))
                         * np.float32(N_BUCKETS - max_exact)).astype(np.int32)
    large = np.minimum(large, N_BUCKETS - 1)
    return np.where(rel < max_exact, rel, large).astype(np.int32)


def moba_bias(rel_bias, *, n_heads, t):
    B = MOBA_BLOCK
    assert t == 2 * B
    table = _t5_bucket_table(2 * B)
    rel = np.arange(B)[:, None] - np.arange(B)[None, :]
    own_bucket = np.where(rel >= 0, table[np.maximum(rel, 0)], -1).astype(np.int32)
    prev_bucket = table[rel + B].astype(np.int32)
    blk = t * 2 * t * 4 + 2 * B * B * 4
    return pl.pallas_call(
        _moba_bias_kernel,
        grid=(n_heads,),
        in_specs=[pl.BlockSpec(memory_space=pltpu.SMEM),
                  pl.BlockSpec((B, B), lambda h: (0, 0)),
                  pl.BlockSpec((B, B), lambda h: (0, 0))],
        out_specs=pl.BlockSpec((1, t, 2 * t), lambda h: (h, 0, 0)),
        out_shape=jax.ShapeDtypeStruct((n_heads, t, 2 * t), F32),
        compiler_params=_params(_vmem_limit(blk, temp_bytes=8 * B * B * 4), 1),
        name="moba_bias",
    )(rel_bias, jnp.asarray(own_bucket), jnp.asarray(prev_bucket))


def _causal_bias(t):
    rows = np.arange(t)[:, None]
    cols = np.arange(2 * t)[None, :] - t
    return jnp.asarray(np.where(cols <= rows, 0.0, MASK_VALUE).astype(np.float32)[None])


FINAL_TILES = 4


def _ones_column(rows):
    lane = lax.broadcasted_iota(jnp.int32, (rows, LANES), 1)
    return jnp.where(lane == 0, 1.0, 0.0).astype(BF16)


def _flash_kernel(*refs, t, nparts, nsplit, group, prev_biased):
    q_refs = refs[:nparts]
    k_refs = refs[nparts:2 * nparts]
    v_ref, b2_ref, o_ref, m_ref, acc_ref = refs[2 * nparts:]
    qi = pl.program_id(1)
    dv = v_ref.shape[-1]
    r = t // nsplit
    chains = [(g, c) for g in range(group) for c in range(nsplit)]

    def head(ref, g):
        return g if ref.shape[0] == group else 0

    qs = [jnp.concatenate([ref[head(ref, g), c * r:(c + 1) * r, :] for ref in q_refs], axis=1)
          for g, c in chains]
    m_ref[...] = jnp.full_like(m_ref, MASK_VALUE)
    acc_ref[...] = jnp.zeros_like(acc_ref)

    def step(tile, width, n_biased=0):
        ks = pl.multiple_of(tile * t, t)
        ones = _ones_column(width)
        bw = n_biased * t
        ss = []
        for n, (g, c) in enumerate(chains):
            k = jnp.concatenate([ref[head(ref, g), pl.ds(ks, width), :] for ref in k_refs],
                                axis=1)
            s = _qk(qs[n], k)
            if bw:
                bias = b2_ref[head(b2_ref, g), c * r:(c + 1) * r, 2 * t - bw:2 * t]
                tail = s[:, width - bw:] + bias
                s = tail if bw == width else jnp.concatenate([s[:, :width - bw], tail], axis=1)
            ss.append(s)
        for n, (g, c) in enumerate(chains):
            v1 = jnp.concatenate([v_ref[g, pl.ds(ks, width), :], ones], axis=1)
            m = m_ref[n]
            m_new = jnp.maximum(m, jnp.max(ss[n], axis=1, keepdims=True))
            p = jnp.exp2(ss[n] - jnp.concatenate([m_new] * (width // LANES), axis=1))
            a = jnp.exp2(m - m_new)
            acc_ref[n] = (jnp.concatenate([a] * ((dv + LANES) // LANES), axis=1) * acc_ref[n]
                          + jnp.dot(p.astype(BF16), v1, preferred_element_type=F32))
            m_ref[n] = m_new

    n_biased = jnp.minimum(2 if prev_biased else 1, qi + 1)
    lead = qi + 1 - n_biased
    quads = lead // FINAL_TILES

    def body(kb, carry):
        step(FINAL_TILES * kb, FINAL_TILES * t)
        return carry

    lax.fori_loop(0, quads, body, 0)
    start = quads * FINAL_TILES
    rest = qi + 1 - start
    if prev_biased:
        @pl.when(rest > FINAL_TILES)
        def _():
            step(start, t)
    spill = (rest > FINAL_TILES).astype(jnp.int32)
    for width_tiles in range(1, FINAL_TILES + 1):
        @pl.when(rest - spill == width_tiles)
        def _(width_tiles=width_tiles):
            step(start + spill, width_tiles * t,
                 n_biased=min(2 if prev_biased else 1, width_tiles))
    for g in range(group):
        out = []
        for c in range(nsplit):
            acc = acc_ref[g * nsplit + c]
            out.append(acc[:, :dv] * (1.0 / acc[:, dv:dv + 1]))
        o_ref[:, g * dv:(g + 1) * dv] = jnp.concatenate(out, axis=0).astype(o_ref.dtype)


def flash_attention(q_parts, k_parts, v_part, b2, *, n_heads, name, t=ATTN_TILE, nsplit=1,
                    group=2, prev_biased=False):
    v, v_off = v_part
    S, dv = v.shape[1], v.shape[2]

    def spec(arr, off, rows):
        g = group if off is not None and arr.shape[0] > 1 else 1
        assert off is None or off % group == 0
        blk_shape = (g, t if rows else arr.shape[1], arr.shape[2])
        if g == 1:
            index = lambda h, i: (0, i if rows else 0, 0)
        else:
            index = lambda h, i: (h + off // group, i if rows else 0, 0)
        return pl.BlockSpec(blk_shape, index), math.prod(blk_shape) * arr.dtype.itemsize

    operands = ([(a, o, True) for a, o in q_parts] + [(a, o, False) for a, o in k_parts]
                + [(v, v_off, False), (b2, 0 if b2.shape[0] > 1 else None, False)])
    in_specs, blk = [], t * group * dv * 2
    for arr, off, rows in operands:
        s, nbytes = spec(arr, off, rows)
        in_specs.append(s)
        blk += nbytes
    nchain = group * nsplit
    r = t // nsplit
    return pl.pallas_call(
        functools.partial(_flash_kernel, t=t, nparts=len(q_parts), nsplit=nsplit, group=group,
                          prev_biased=prev_biased),
        grid=(n_heads // group, S // t),
        in_specs=in_specs,
        out_specs=pl.BlockSpec((t, group * dv), lambda h, i: (i, h)),
        out_shape=jax.ShapeDtypeStruct((S, n_heads * dv), BF16),
        scratch_shapes=[pltpu.VMEM((nchain, r, LANES), F32),
                        pltpu.VMEM((nchain, r, dv + LANES), F32)],
        compiler_params=_params(
            _vmem_limit(blk, nchain * r * (dv + 2 * LANES) * 4,
                        temp_bytes=6 * group * t * t * 4), 2),
        name=name,
    )(*[arr for arr, _, _ in operands])


def _moba_key_onehot(S):
    nb = S // MOBA_BLOCK
    lanes = np.arange(LANES)[None, :]
    blk_of = (np.arange(S) // MOBA_BLOCK)[:, None]
    onehot = (lanes % nb == blk_of) & (lanes < 2 * nb)
    return jnp.asarray(onehot.astype(np.float32)[None], dtype=BF16)


def _mm_res_ln_kernel(*refs, n_a, chunks, layer, kc):
    a_refs = refs[:n_a]
    w_hbm, x_ref, g_ref, b_ref, o_ref, obf_ref, w_ref, stage_ref, sem = refs[n_a:]

    @pl.when(pl.program_id(0) == 0)
    def _():
        n_chunks = w_ref.shape[0] // kc

        def chunk_copy(c):
            return pltpu.make_async_copy(w_hbm.at[layer, pl.ds(c * kc, kc), :],
                                         stage_ref.at[c % 2], sem.at[c % 2])

        chunk_copy(0).start()
        for c in range(n_chunks):
            if c + 1 < n_chunks:
                chunk_copy(c + 1).start()
            chunk_copy(c).wait()
            w_ref[c * kc:(c + 1) * kc, :] = stage_ref[c % 2].astype(BF16)

    rc = x_ref.shape[0] // chunks
    for c in range(chunks):
        rows = slice(c * rc, (c + 1) * rc)
        y, row = None, 0
        for a_ref in a_refs:
            ka = a_ref.shape[1]
            d = jnp.dot(a_ref[rows, :], w_ref[row:row + ka, :], preferred_element_type=F32)
            y = d if y is None else y + d
            row += ka
        z = DEEPNORM_ALPHA * x_ref[rows, :] + y
        mu = jnp.mean(z, axis=-1, keepdims=True)
        zc = z - mu
        var = jnp.mean(zc * zc, axis=-1, keepdims=True)
        out = zc * lax.rsqrt(var + LN_EPS) * g_ref[...] + b_ref[...]
        o_ref[rows, :] = out
        obf_ref[rows, :] = out.astype(BF16)


def mm_res_ln(a_list, w_stack, layer, x, g, b, *, tm, chunks=2, kc=512):
    M = a_list[0].shape[0]
    K = sum(a.shape[1] for a in a_list)
    N = w_stack.shape[2]
    assert w_stack.shape[1] == K and K % kc == 0
    blk = tm * K * 2 + tm * N * 4 + 2 * N * 4 + tm * N * 4 + tm * N * 2
    return pl.pallas_call(
        functools.partial(_mm_res_ln_kernel, n_a=len(a_list), chunks=chunks, layer=layer,
                          kc=kc),
        grid=(M // tm,),
        in_specs=[pl.BlockSpec((tm, a.shape[1]), lambda i: (i, 0)) for a in a_list] + [
            pl.BlockSpec(memory_space=pl.ANY),
            pl.BlockSpec((tm, N), lambda i: (i, 0)),
            pl.BlockSpec((1, N), lambda i: (0, 0)),
            pl.BlockSpec((1, N), lambda i: (0, 0))],
        out_specs=[pl.BlockSpec((tm, N), lambda i: (i, 0)),
                   pl.BlockSpec((tm, N), lambda i: (i, 0))],
        out_shape=[jax.ShapeDtypeStruct((M, N), F32),
                   jax.ShapeDtypeStruct((M, N), BF16)],
        scratch_shapes=[pltpu.VMEM((K, N), BF16), pltpu.VMEM((2, kc, N), F32),
                        pltpu.SemaphoreType.DMA((2,))],
        compiler_params=_params(
            _vmem_limit(blk, K * N * 2 + 2 * kc * N * 4, temp_bytes=3 * tm * N * 4), 1),
        name="mm_res_ln",
    )(*a_list, w_stack, x, g, b)


def _ffn_gu_kernel(x_ref, wg_ref, wu_ref, o_ref, wgb_ref, wub_ref):
    @pl.when(pl.program_id(1) == 0)
    def _():
        wgb_ref[...] = wg_ref[...].astype(BF16)
        wub_ref[...] = wu_ref[...].astype(BF16)

    x = x_ref[...]
    g = jnp.dot(x, wgb_ref[...], preferred_element_type=F32)
    u = jnp.dot(x, wub_ref[...], preferred_element_type=F32)
    o_ref[...] = (g * jax.nn.sigmoid(g) * u).astype(o_ref.dtype)


def ffn_gate_up(xb, wg_stack, wu_stack, layer, *, tm, tf):
    M, K = xb.shape
    F = wg_stack.shape[2]
    blk = tm * K * 2 + 2 * K * tf * 4 + tm * tf * 2
    w_spec = pl.BlockSpec((None, K, tf), lambda j, i: (layer, 0, j))
    return pl.pallas_call(
        _ffn_gu_kernel,
        grid=(F // tf, M // tm),
        in_specs=[pl.BlockSpec((tm, K), lambda j, i: (i, 0)), w_spec, w_spec],
        out_specs=pl.BlockSpec((tm, tf), lambda j, i: (i, j)),
        out_shape=jax.ShapeDtypeStruct((M, F), BF16),
        scratch_shapes=[pltpu.VMEM((K, tf), BF16), pltpu.VMEM((K, tf), BF16)],
        compiler_params=_params(
            _vmem_limit(blk, 2 * K * tf * 2, temp_bytes=4 * tm * tf * 4), 2),
        name="ffn_gate_up",
    )(xb, wg_stack, wu_stack)


def _rope_lanes(r, cos_t, sin_a, sin_b):
    return (r * cos_t + pltpu.roll(r, LANES - MLA_ROPE // 2, 1) * sin_a
            + pltpu.roll(r, MLA_ROPE // 2, 1) * sin_b)


def _rms(x, g):
    return x * lax.rsqrt(jnp.mean(x * x, axis=-1, keepdims=True) + RMS_EPS) * g


def _fetch_weight_once(w_hbm, layer, stage_ref, sem):
    copy = pltpu.make_async_copy(w_hbm.at[layer], stage_ref, sem.at[0])
    copy.start()
    copy.wait()


def _mla_in_kernel(x_ref, w_hbm, gq_ref, gkv_ref, cos_ref, sa_ref, sb_ref,
                   cq_ref, ckv_ref, kr_ref, w_ref, stage_ref, sem, *, layer):
    @pl.when(pl.program_id(0) == 0)
    def _():
        _fetch_weight_once(w_hbm, layer, stage_ref, sem)
        n_in = stage_ref.shape[1]
        w_ref[:, 0:n_in] = stage_ref[...].astype(BF16)
        w_ref[:, n_in:] = jnp.zeros((w_ref.shape[0], w_ref.shape[1] - n_in), BF16)

    half = x_ref.shape[0] // 2
    for rows in (slice(0, half), slice(half, 2 * half)):
        h = jnp.dot(x_ref[rows, :], w_ref[...], preferred_element_type=F32)
        cq_ref[rows, :] = _rms(h[:, :MLA_Q_RANK], gq_ref[...]).astype(BF16)
        ckv_ref[rows, :] = _rms(h[:, MLA_Q_RANK:MLA_Q_RANK + MLA_KV_RANK],
                                gkv_ref[...]).astype(BF16)
        r = h[:, MLA_Q_RANK + MLA_KV_RANK:]
        kr_ref[rows, :] = _rope_lanes(r, cos_ref[rows, :], sa_ref[rows, :],
                                      sb_ref[rows, :]).astype(BF16)


def mla_in(xb, w_stack, layer, gq, gkv, cos_t, sin_a, sin_b, *, tm=512):
    S, K = xb.shape
    n_in = w_stack.shape[2]
    N = MLA_Q_RANK + MLA_KV_RANK + LANES
    blk = (tm * K * 2 + 3 * tm * LANES * 4
           + tm * (MLA_Q_RANK + MLA_KV_RANK + LANES) * 2)
    row = lambda i: (i, 0)
    fixed = lambda i: (0, 0)
    return pl.pallas_call(
        functools.partial(_mla_in_kernel, layer=layer),
        grid=(S // tm,),
        in_specs=[pl.BlockSpec((tm, K), row), pl.BlockSpec(memory_space=pl.ANY),
                  pl.BlockSpec((1, MLA_Q_RANK), fixed), pl.BlockSpec((1, MLA_KV_RANK), fixed),
                  pl.BlockSpec((tm, LANES), row), pl.BlockSpec((tm, LANES), row),
                  pl.BlockSpec((tm, LANES), row)],
        out_specs=[pl.BlockSpec((tm, MLA_Q_RANK), row), pl.BlockSpec((tm, MLA_KV_RANK), row),
                   pl.BlockSpec((tm, LANES), row)],
        out_shape=[jax.ShapeDtypeStruct((S, MLA_Q_RANK), BF16),
                   jax.ShapeDtypeStruct((S, MLA_KV_RANK), BF16),
                   jax.ShapeDtypeStruct((S, LANES), BF16)],
        scratch_shapes=[pltpu.VMEM((K, N), BF16), pltpu.VMEM((K, n_in), F32),
                        pltpu.SemaphoreType.DMA((1,))],
        compiler_params=_params(
            _vmem_limit(blk, K * N * 2 + K * N * 4, temp_bytes=3 * tm * N * 4), 1),
        name="mla_in",
    )(xb, w_stack, gq, gkv, cos_t, sin_a, sin_b)


def _mla_q_kernel(cq_ref, w_hbm, cos_ref, sa_ref, sb_ref, q_ref, w_ref, stage_ref, sem,
                  *, qscale, layer):
    @pl.when(pl.program_id(0) == 0)
    def _():
        _fetch_weight_once(w_hbm, layer, stage_ref, sem)
        wf = stage_ref[...]
        dh = MLA_NOPE + MLA_ROPE
        w_ref[...] = jnp.zeros_like(w_ref)
        for hh in range(q_ref.shape[0]):
            w_ref[:, hh * 2 * LANES:hh * 2 * LANES + dh] = wf[:, hh * dh:(hh + 1) * dh].astype(BF16)

    acc = jnp.dot(cq_ref[...], w_ref[...], preferred_element_type=F32) * qscale
    cos_t, sin_a, sin_b = cos_ref[...], sa_ref[...], sb_ref[...]
    for hh in range(q_ref.shape[0]):
        base = hh * 2 * LANES
        q_ref[hh, :, 0:LANES] = acc[:, base:base + LANES].astype(BF16)
        r = acc[:, base + LANES:base + 2 * LANES]
        q_ref[hh, :, LANES:2 * LANES] = _rope_lanes(r, cos_t, sin_a, sin_b).astype(BF16)


def mla_q(cq, w_stack, layer, cos_t, sin_a, sin_b, *, qscale, tm=512):
    S, K = cq.shape
    n_in = w_stack.shape[2]
    nh = n_in // (MLA_NOPE + MLA_ROPE)
    N = nh * 2 * LANES
    blk = tm * K * 2 + 3 * tm * LANES * 4 + tm * N * 2
    row = lambda i: (i, 0)
    return pl.pallas_call(
        functools.partial(_mla_q_kernel, qscale=qscale, layer=layer),
        grid=(S // tm,),
        in_specs=[pl.BlockSpec((tm, K), row), pl.BlockSpec(memory_space=pl.ANY),
                  pl.BlockSpec((tm, LANES), row), pl.BlockSpec((tm, LANES), row),
                  pl.BlockSpec((tm, LANES), row)],
        out_specs=pl.BlockSpec((nh, tm, 2 * LANES), lambda i: (0, i, 0)),
        out_shape=jax.ShapeDtypeStruct((nh, S, 2 * LANES), BF16),
        scratch_shapes=[pltpu.VMEM((K, N), BF16), pltpu.VMEM((K, n_in), F32),
                        pltpu.SemaphoreType.DMA((1,))],
        compiler_params=_params(
            _vmem_limit(blk, K * N * 2 + K * n_in * 4, temp_bytes=2 * tm * N * 4 + K * n_in * 4),
            1),
        name="mla_q",
    )(cq, w_stack, cos_t, sin_a, sin_b)


def _mla_kv_kernel(ckv_ref, w_ref, kr_ref, k_ref, v_ref):
    acc = jnp.dot(ckv_ref[...], w_ref[...], preferred_element_type=F32)
    kr = kr_ref[...]
    for hh in range(k_ref.shape[0]):
        base = hh * (MLA_NOPE + MLA_V)
        k_ref[hh, :, 0:LANES] = acc[:, base:base + MLA_NOPE].astype(BF16)
        k_ref[hh, :, LANES:2 * LANES] = kr
        v_ref[hh] = acc[:, base + MLA_NOPE:base + MLA_NOPE + MLA_V].astype(BF16)


def mla_kv(ckv, w, kr, *, tm=512):
    S, K = ckv.shape
    N = w.shape[1]
    nh = N // (MLA_NOPE + MLA_V)
    blk = tm * K * 2 + K * N * 2 + tm * LANES * 2 + nh * tm * (2 * LANES + MLA_V) * 2
    row = lambda i: (i, 0)
    return pl.pallas_call(
        _mla_kv_kernel,
        grid=(S // tm,),
        in_specs=[pl.BlockSpec((tm, K), row), pl.BlockSpec((K, N), lambda i: (0, 0)),
                  pl.BlockSpec((tm, LANES), row)],
        out_specs=[pl.BlockSpec((nh, tm, 2 * LANES), lambda i: (0, i, 0)),
                   pl.BlockSpec((nh, tm, MLA_V), lambda i: (0, i, 0))],
        out_shape=[jax.ShapeDtypeStruct((nh, S, 2 * LANES), BF16),
                   jax.ShapeDtypeStruct((nh, S, MLA_V), BF16)],
        compiler_params=_params(_vmem_limit(blk, temp_bytes=2 * tm * N * 4), 1),
        name="mla_kv",
    )(ckv, w, kr)


def _rope_lane_tables(S):
    half = MLA_ROPE // 2
    inv = ROPE_THETA ** (-np.arange(0, MLA_ROPE, 2, dtype=np.float64) / MLA_ROPE)
    ang = np.arange(S, dtype=np.float64)[:, None] * inv[None, :]
    cos, sin = np.cos(ang), np.sin(ang)
    z = np.zeros((S, half))
    z2 = np.zeros((S, LANES - MLA_ROPE))
    cos_t = np.concatenate([cos, cos, z2], axis=1)
    sin_a = np.concatenate([-sin, z, z2], axis=1)
    sin_b = np.concatenate([z, sin, z2], axis=1)
    return tuple(jnp.asarray(a.astype(np.float32)) for a in (cos_t, sin_a, sin_b))


def _pad_cols(w, n):
    return jnp.pad(w, ((0, 0), (0, n - w.shape[1])))


def kernel(x, ab_w_in, ab_forget_bias, ab_w_out, rel_bias, mla_w_in, mla_q_norm,
           mla_kv_norm, mla_w_uq, mla_w_ukv, mla_w_out, ffn_w_gate, ffn_w_up,
           ffn_w_down, ln_g, ln_b):
    S = x.shape[1]
    xf = x.reshape(S, D_MODEL)
    xb = xf.astype(BF16)
    cos_t, sin_a, sin_b = _rope_lane_tables(S)
    nf = 3 * DA + N_HEADS_A
    qs = HEAD_DIM ** -0.5 * LOG2E
    colscale = np.ones((1, 3 * DA + 3 * DB), np.float32)
    colscale[:, :DA] = qs
    colscale[:, 3 * DA:3 * DA + DB] = qs
    colscale = jnp.asarray(colscale)
    causal_b2 = _causal_bias(ATTN_TILE)
    moba_b2 = moba_bias(rel_bias, n_heads=N_HEADS_B, t=ATTN_TILE)
    moba_kaug = _moba_key_onehot(S)
    hb = 3 * N_HEADS_A

    for layer in range(DEPTH):
        j = layer // 2
        g0, b0 = ln_g[layer, 0][None, :], ln_b[layer, 0][None, :]
        g1, b1 = ln_g[layer, 1][None, :], ln_b[layer, 1][None, :]
        if layer % 2 == 0:
            w_in = ab_w_in[j]
            w_qkv = jnp.concatenate([w_in[:, :3 * DA], w_in[:, nf:]], axis=1).astype(BF16)
            w_f = _pad_cols(w_in[:, 3 * DA:nf], LANES).astype(BF16)
            b_f = _pad_cols(ab_forget_bias[j][None, :], LANES)
            heads = mm_heads(xb, w_qkv, colscale, tm=1024, tn=1024)
            qaug, kaug = fox_gate(xb, w_f, b_f)
            ya = flash_attention([(heads, 0), (qaug, 0)], [(heads, N_HEADS_A), (kaug, 0)],
                                 (heads, 2 * N_HEADS_A), causal_b2,
                                 n_heads=N_HEADS_A, name="flash_fox")
            qaug_b = moba_gate(heads, rel_bias, n_heads=N_HEADS_B, q_off=hb,
                               k_off=hb + N_HEADS_B)
            yb = flash_attention([(heads, hb), (qaug_b, 0)],
                                 [(heads, hb + N_HEADS_B), (moba_kaug, None)],
                                 (heads, hb + 2 * N_HEADS_B), moba_b2,
                                 n_heads=N_HEADS_B, name="flash_moba", prev_biased=True)
            y, w_out = [ya, yb], ab_w_out
        else:
            cq, ckv, kr = mla_in(xb, mla_w_in, j, mla_q_norm[j][None, :],
                                 mla_kv_norm[j][None, :], cos_t, sin_a, sin_b)
            q_full = mla_q(cq, mla_w_uq, j, cos_t, sin_a, sin_b,
                           qscale=(MLA_NOPE + MLA_ROPE) ** -0.5 * LOG2E)
            k_full, v = mla_kv(ckv, mla_w_ukv[j].astype(BF16), kr)
            y = [flash_attention([(q_full, 0)], [(k_full, 0)], (v, 0), causal_b2,
                                 n_heads=MLA_HEADS, name="flash_mla")]
            w_out = mla_w_out
        xf, xb = mm_res_ln(y, w_out, j, xf, g0, b0, tm=512)
        hmid = ffn_gate_up(xb, ffn_w_gate, ffn_w_up, layer, tm=1024, tf=512)
        xf, xb = mm_res_ln([hmid], ffn_w_down, layer, xf, g1, b1, tm=256)
    return xf.reshape(1, S, D_MODEL)
```

```python
import functools
import math

import numpy as np
import jax
import jax.numpy as jnp
from jax import lax
from jax.experimental import pallas as pl
from jax.experimental.pallas import tpu as pltpu

F32 = jnp.float32
BF16 = jnp.bfloat16

D_MODEL = 2048
DEPTH = 4
HEAD_DIM = 128
N_HEADS_A = 8
N_HEADS_B = 8
MOBA_BLOCK = 256
MOBA_TOPK = 3
N_BUCKETS = 32
MAX_DISTANCE = 128
MLA_HEADS = 16
MLA_Q_RANK = 512
MLA_KV_RANK = 512
MLA_NOPE = 128
MLA_ROPE = 64
MLA_V = 128
ROPE_THETA = 10000.0
DEEPNORM_ALPHA = (2 * DEPTH) ** 0.25
DA = N_HEADS_A * HEAD_DIM
DB = N_HEADS_B * HEAD_DIM
LN_EPS = 1e-5
RMS_EPS = 1e-6
LOG2E = math.log2(math.e)

LANES = 128
VMEM_BUDGET_BYTES = 56 * 2**20
VMEM_FLOOR_BYTES = 16 * 2**20
VMEM_SPILL_BYTES = 4 * 2**20
MASK_VALUE = -1e30
ATTN_TILE = 2 * MOBA_BLOCK


def _vmem_limit(block_bytes, scratch_bytes=0, temp_bytes=0):
    est = 2 * block_bytes + scratch_bytes + temp_bytes + VMEM_SPILL_BYTES
    return int(min(max(est, VMEM_FLOOR_BYTES), VMEM_BUDGET_BYTES))


def _params(vmem_bytes, ngrid):
    return pltpu.CompilerParams(
        dimension_semantics=("arbitrary",) * ngrid, vmem_limit_bytes=vmem_bytes)


def _split3(x):
    hi = x.astype(BF16).astype(F32)
    r1 = x - hi
    lo = r1.astype(BF16).astype(F32)
    return hi, lo, r1 - lo


def _qk(a, b):
    return lax.dot_general(a, b, (((1,), (1,)), ((), ())), preferred_element_type=F32)


def _mm_heads_kernel(x_ref, w_ref, cs_ref, o_ref, *, width):
    acc = jnp.dot(x_ref[...], w_ref[...], preferred_element_type=F32) * cs_ref[...]
    for hh in range(o_ref.shape[0]):
        o_ref[hh] = acc[:, hh * width:(hh + 1) * width].astype(o_ref.dtype)


def mm_heads(x, w, colscale, *, tm, tn, width=HEAD_DIM):
    M, K = x.shape
    N = w.shape[1]
    nh = tn // width
    blk = tm * K * 2 + K * tn * 2 + tm * tn * 2 + tn * 4
    return pl.pallas_call(
        functools.partial(_mm_heads_kernel, width=width),
        grid=(M // tm, N // tn),
        in_specs=[pl.BlockSpec((tm, K), lambda i, j: (i, 0)),
                  pl.BlockSpec((K, tn), lambda i, j: (0, j)),
                  pl.BlockSpec((1, tn), lambda i, j: (0, j))],
        out_specs=pl.BlockSpec((nh, tm, width), lambda i, j: (j, i, 0)),
        out_shape=jax.ShapeDtypeStruct((N // width, M, width), BF16),
        compiler_params=_params(_vmem_limit(blk, temp_bytes=2 * tm * tn * 4), 2),
        name="mm_heads",
    )(x, w, colscale)


def _fox_gate_kernel(x_ref, w_hbm, bf_ref, selq_ref, selk_ref, cq_ref, ck_ref,
                     qaug_ref, kaug_ref, carry_ref, wf_ref, stage_ref, sem,
                     *, tb, nh, layer, col0):
    @pl.when(pl.program_id(0) == 0)
    def _():
        carry_ref[...] = jnp.zeros_like(carry_ref)
        copy = pltpu.make_async_copy(w_hbm.at[layer, :, pl.ds(col0, LANES)], stage_ref,
                                     sem.at[0])
        copy.start()
        copy.wait()
        lane = lax.broadcasted_iota(jnp.int32, stage_ref.shape, 1)
        wf_ref[...] = jnp.where(lane < nh, stage_ref[...], 0.0).astype(BF16)

    z = jnp.dot(x_ref[...], wf_ref[...], preferred_element_type=F32) + bf_ref[...]
    lf = jnp.minimum(z, 0.0) - jnp.log1p(jnp.exp(-jnp.abs(z)))
    row = lax.broadcasted_iota(jnp.int32, (tb, tb), 0)
    col = lax.broadcasted_iota(jnp.int32, (tb, tb), 1)
    tri = jnp.where(row >= col, 1.0, 0.0).astype(BF16)
    lf_terms = jnp.concatenate(_split3(lf), axis=1).astype(BF16)
    part = jnp.dot(tri, lf_terms, preferred_element_type=F32)
    cs = (part[:, :LANES] + part[:, LANES:2 * LANES] + part[:, 2 * LANES:]) + carry_ref[...]
    carry_ref[...] = cs[tb - 1:tb, :]
    lane = lax.broadcasted_iota(jnp.int32, (tb, LANES), 1)
    hi, lo, lo2 = [jnp.where(lane < nh, term, 0.0) for term in _split3(cs * LOG2E)]
    terms = (hi + pltpu.roll(lo, nh, 1) + pltpu.roll(lo2, 2 * nh, 1)).astype(BF16)
    qa = jnp.dot(terms, selq_ref[...], preferred_element_type=F32) + cq_ref[...]
    ka = jnp.dot(terms, selk_ref[...], preferred_element_type=F32) + ck_ref[...]
    for h in range(nh):
        qaug_ref[h] = qa[:, h * LANES:(h + 1) * LANES].astype(BF16)
        kaug_ref[h] = ka[:, h * LANES:(h + 1) * LANES].astype(BF16)


def _fox_selectors(nh):
    selq = np.zeros((LANES, nh * LANES), np.float32)
    selk = np.zeros((LANES, nh * LANES), np.float32)
    cq = np.zeros((1, nh * LANES), np.float32)
    ck = np.zeros((1, nh * LANES), np.float32)
    for h in range(nh):
        for term in range(3):
            selq[term * nh + h, h * LANES + term] = 1.0
            selk[term * nh + h, h * LANES + 3 + term] = -1.0
        cq[0, h * LANES + 3:h * LANES + 6] = 1.0
        ck[0, h * LANES:h * LANES + 3] = 1.0
    return (jnp.asarray(selq, dtype=BF16), jnp.asarray(selk, dtype=BF16),
            jnp.asarray(cq), jnp.asarray(ck))


def fox_gate(xb, w_stack, layer, col0, bf_pad, *, tb=512, nh=N_HEADS_A):
    S, K = xb.shape
    assert col0 % LANES == 0
    selq, selk, cq, ck = _fox_selectors(nh)
    blk = (tb * K * 2 + 2 * nh * tb * LANES * 2 + 2 * LANES * nh * LANES * 2)
    aug = jax.ShapeDtypeStruct((nh, S, LANES), BF16)
    fixed = lambda i: (0, 0)
    return pl.pallas_call(
        functools.partial(_fox_gate_kernel, tb=tb, nh=nh, layer=layer, col0=col0),
        grid=(S // tb,),
        in_specs=[pl.BlockSpec((tb, K), lambda i: (i, 0)),
                  pl.BlockSpec(memory_space=pl.ANY),
                  pl.BlockSpec((1, LANES), fixed),
                  pl.BlockSpec(selq.shape, fixed), pl.BlockSpec(selk.shape, fixed),
                  pl.BlockSpec(cq.shape, fixed), pl.BlockSpec(ck.shape, fixed)],
        out_specs=[pl.BlockSpec((nh, tb, LANES), lambda i: (0, i, 0)),
                   pl.BlockSpec((nh, tb, LANES), lambda i: (0, i, 0))],
        out_shape=[aug, aug],
        scratch_shapes=[pltpu.VMEM((1, LANES), F32), pltpu.VMEM((K, LANES), BF16),
                        pltpu.VMEM((K, LANES), F32), pltpu.SemaphoreType.DMA((1,))],
        compiler_params=_params(
            _vmem_limit(blk, K * LANES * 6,
                        temp_bytes=4 * tb * tb * 4 + 4 * tb * nh * LANES * 4), 1),
        name="fox_gate",
    )(xb, w_stack, bf_pad, selq, selk, cq, ck)


def _moba_gate_kernel(rb_ref, q_ref, k_ref, qaug_ref, kmean_ref, *, nb, tr):
    B = MOBA_BLOCK
    h = pl.program_id(0)
    i = pl.program_id(1)
    far_bias = rb_ref[N_BUCKETS - 1, h] * LOG2E

    @pl.when(i == 0)
    def _():
        kf = k_ref[0].astype(F32).reshape(nb, B, HEAD_DIM)
        terms = _split3(jnp.sum(kf, axis=1) * (1.0 / B))
        kmean_ref[...] = jnp.concatenate(terms, axis=0).astype(BF16)

    blk = lax.broadcasted_iota(jnp.int32, (nb, tr), 0)
    blk_f = blk.astype(F32)
    own = i * (tr // B) + lax.broadcasted_iota(jnp.int32, (nb, tr), 1) // B
    g3 = _qk(kmean_ref[...], q_ref[0])
    g = g3[0:nb] + g3[nb:2 * nb] + g3[2 * nb:3 * nb]
    neg_inf = -jnp.inf
    g = jnp.where(blk < own, g, neg_inf)
    mb = jnp.where(blk == own, 0.0, MASK_VALUE).astype(F32)
    for _ in range(MOBA_TOPK):
        mx = jnp.max(g, axis=0, keepdims=True)
        hit = jnp.logical_and(g == mx, mx > neg_inf)
        idx = jnp.min(jnp.where(hit, blk_f, float(nb)), axis=0, keepdims=True)
        pick = blk_f == idx
        mb = jnp.where(pick, far_bias, mb)
        g = jnp.where(pick, neg_inf, g)
    mb_hi = mb.astype(BF16).astype(F32)
    mb_lo = jnp.where(mb > 0.5 * MASK_VALUE, mb - mb_hi, 0.0)
    aug_t = jnp.concatenate([mb_hi, mb_lo, jnp.zeros((LANES - 2 * nb, tr), F32)], axis=0)
    qaug_ref[0] = jnp.transpose(aug_t).astype(BF16)


def moba_gate(heads, rel_bias, *, n_heads, q_off, k_off, tr=1024):
    S = heads.shape[1]
    nb = S // MOBA_BLOCK
    assert 2 * nb <= LANES
    blk = tr * HEAD_DIM * 2 + S * HEAD_DIM * 2 + tr * LANES * 2
    return pl.pallas_call(
        functools.partial(_moba_gate_kernel, nb=nb, tr=tr),
        grid=(n_heads, S // tr),
        in_specs=[pl.BlockSpec(memory_space=pltpu.SMEM),
                  pl.BlockSpec((1, tr, HEAD_DIM), lambda h, i: (h + q_off, i, 0)),
                  pl.BlockSpec((1, S, HEAD_DIM), lambda h, i: (h + k_off, 0, 0))],
        out_specs=pl.BlockSpec((1, tr, LANES), lambda h, i: (h, i, 0)),
        out_shape=jax.ShapeDtypeStruct((n_heads, S, LANES), BF16),
        scratch_shapes=[pltpu.VMEM((3 * nb, HEAD_DIM), BF16)],
        compiler_params=_params(
            _vmem_limit(blk, 3 * nb * HEAD_DIM * 2,
                        temp_bytes=S * HEAD_DIM * 4 + 12 * tr * LANES * 4), 2),
        name="moba_gate",
    )(rel_bias, heads, heads)


def _moba_bias_kernel(rb_ref, own_ref, prev_ref, o_ref):
    B = MOBA_BLOCK
    h = pl.program_id(0)
    far = rb_ref[N_BUCKETS - 1, h]
    own_bucket = own_ref[...]
    prev_bucket = prev_ref[...]
    own = jnp.zeros((B, B), F32)
    prev = jnp.zeros((B, B), F32)
    for b in range(N_BUCKETS):
        val = rb_ref[b, h]
        own = jnp.where(own_bucket == b, val, own)
        prev = jnp.where(prev_bucket == b, val, prev)
    own = jnp.where(own_bucket < 0, MASK_VALUE, own * LOG2E)
    prev = (prev - far) * LOG2E
    o_ref[0] = jnp.zeros(o_ref.shape[1:], F32)
    o_ref[0, 0:B, B:2 * B] = prev
    o_ref[0, 0:B, 2 * B:3 * B] = own
    o_ref[0, B:2 * B, 2 * B:3 * B] = prev
    o_ref[0, B:2 * B, 3 * B:4 * B] = own


def _t5_bucket_table(n):
    rel = np.arange(n)
    max_exact = N_BUCKETS // 2
    nf = np.maximum(rel, 1).astype(np.float32)
    large = max_exact + (np.log(nf / np.float32(max_exact))
                         / np.float32(math.log(MAX_DISTANCE / max_exact))
                         * np.float32(N_BUCKETS - max_exact)).astype(np.int32)
    large = np.minimum(large, N_BUCKETS - 1)
    return np.where(rel < max_exact, rel, large).astype(np.int32)


def moba_bias(rel_bias, *, n_heads, t):
    B = MOBA_BLOCK
    assert t == 2 * B
    table = _t5_bucket_table(2 * B)
    rel = np.arange(B)[:, None] - np.arange(B)[None, :]
    own_bucket = np.where(rel >= 0, table[np.maximum(rel, 0)], -1).astype(np.int32)
    prev_bucket = table[rel + B].astype(np.int32)
    blk = t * 2 * t * 4 + 2 * B * B * 4
    return pl.pallas_call(
        _moba_bias_kernel,
        grid=(n_heads,),
        in_specs=[pl.BlockSpec(memory_space=pltpu.SMEM),
                  pl.BlockSpec((B, B), lambda h: (0, 0)),
                  pl.BlockSpec((B, B), lambda h: (0, 0))],
        out_specs=pl.BlockSpec((1, t, 2 * t), lambda h: (h, 0, 0)),
        out_shape=jax.ShapeDtypeStruct((n_heads, t, 2 * t), F32),
        compiler_params=_params(_vmem_limit(blk, temp_bytes=8 * B * B * 4), 1),
        name="moba_bias",
    )(rel_bias, jnp.asarray(own_bucket), jnp.asarray(prev_bucket))


def _causal_bias(t):
    rows = np.arange(t)[:, None]
    cols = np.arange(2 * t)[None, :] - t
    return jnp.asarray(np.where(cols <= rows, 0.0, MASK_VALUE).astype(np.float32)[None])


FINAL_TILES = 4


def _ones_column(rows):
    lane = lax.broadcasted_iota(jnp.int32, (rows, LANES), 1)
    return jnp.where(lane == 0, 1.0, 0.0).astype(BF16)


def _flash_kernel(*refs, t, nparts, nsplit, group, prev_biased):
    q_refs = refs[:nparts]
    k_refs = refs[nparts:2 * nparts]
    v_ref, b2_ref, o_ref, m_ref, acc_ref = refs[2 * nparts:]
    qi = pl.program_id(1)
    dv = v_ref.shape[-1]
    r = t // nsplit
    chains = [(g, c) for g in range(group) for c in range(nsplit)]

    def head(ref, g):
        return g if ref.shape[0] == group else 0

    qs = [jnp.concatenate([ref[head(ref, g), c * r:(c + 1) * r, :] for ref in q_refs], axis=1)
          for g, c in chains]
    m_ref[...] = jnp.full_like(m_ref, MASK_VALUE)
    acc_ref[...] = jnp.zeros_like(acc_ref)

    def step(tile, width, n_biased=0):
        ks = pl.multiple_of(tile * t, t)
        ones = _ones_column(width)
        bw = n_biased * t
        ss = []
        for n, (g, c) in enumerate(chains):
            k = jnp.concatenate([ref[head(ref, g), pl.ds(ks, width), :] for ref in k_refs],
                                axis=1)
            s = _qk(qs[n], k)
            if bw:
                bias = b2_ref[head(b2_ref, g), c * r:(c + 1) * r, 2 * t - bw:2 * t]
                tail = s[:, width - bw:] + bias
                s = tail if bw == width else jnp.concatenate([s[:, :width - bw], tail], axis=1)
            ss.append(s)
        for n, (g, c) in enumerate(chains):
            v1 = jnp.concatenate([v_ref[g, pl.ds(ks, width), :], ones], axis=1)
            m = m_ref[n]
            m_new = jnp.maximum(m, jnp.max(ss[n], axis=1, keepdims=True))
            p = jnp.exp2(ss[n] - jnp.concatenate([m_new] * (width // LANES), axis=1))
            a = jnp.exp2(m - m_new)
            acc_ref[n] = (jnp.concatenate([a] * ((dv + LANES) // LANES), axis=1) * acc_ref[n]
                          + jnp.dot(p.astype(BF16), v1, preferred_element_type=F32))
            m_ref[n] = m_new

    n_biased = jnp.minimum(2 if prev_biased else 1, qi + 1)
    lead = qi + 1 - n_biased
    quads = lead // FINAL_TILES

    def body(kb, carry):
        step(FINAL_TILES * kb, FINAL_TILES * t)
        return carry

    lax.fori_loop(0, quads, body, 0)
    start = quads * FINAL_TILES
    rest = qi + 1 - start
    if prev_biased:
        @pl.when(rest > FINAL_TILES)
        def _():
            step(start, t)
    spill = (rest > FINAL_TILES).astype(jnp.int32)
    for width_tiles in range(1, FINAL_TILES + 1):
        @pl.when(rest - spill == width_tiles)
        def _(width_tiles=width_tiles):
            step(start + spill, width_tiles * t,
                 n_biased=min(2 if prev_biased else 1, width_tiles))
    for g in range(group):
        out = []
        for c in range(nsplit):
            acc = acc_ref[g * nsplit + c]
            out.append(acc[:, :dv] * (1.0 / acc[:, dv:dv + 1]))
        o_ref[:, g * dv:(g + 1) * dv] = jnp.concatenate(out, axis=0).astype(o_ref.dtype)


def flash_attention(q_parts, k_parts, v_part, b2, *, n_heads, name, t=ATTN_TILE, nsplit=1,
                    group=2, prev_biased=False):
    v, v_off = v_part
    S, dv = v.shape[1], v.shape[2]

    def spec(arr, off, rows):
        g = group if off is not None and arr.shape[0] > 1 else 1
        assert off is None or off % group == 0
        blk_shape = (g, t if rows else arr.shape[1], arr.shape[2])
        if g == 1:
            index = lambda h, i: (0, i if rows else 0, 0)
        else:
            index = lambda h, i: (h + off // group, i if rows else 0, 0)
        return pl.BlockSpec(blk_shape, index), math.prod(blk_shape) * arr.dtype.itemsize

    operands = ([(a, o, True) for a, o in q_parts] + [(a, o, False) for a, o in k_parts]
                + [(v, v_off, False), (b2, 0 if b2.shape[0] > 1 else None, False)])
    in_specs, blk = [], t * group * dv * 2
    for arr, off, rows in operands:
        s, nbytes = spec(arr, off, rows)
        in_specs.append(s)
        blk += nbytes
    nchain = group * nsplit
    r = t // nsplit
    return pl.pallas_call(
        functools.partial(_flash_kernel, t=t, nparts=len(q_parts), nsplit=nsplit, group=group,
                          prev_biased=prev_biased),
        grid=(n_heads // group, S // t),
        in_specs=in_specs,
        out_specs=pl.BlockSpec((t, group * dv), lambda h, i: (i, h)),
        out_shape=jax.ShapeDtypeStruct((S, n_heads * dv), BF16),
        scratch_shapes=[pltpu.VMEM((nchain, r, LANES), F32),
                        pltpu.VMEM((nchain, r, dv + LANES), F32)],
        compiler_params=_params(
            _vmem_limit(blk, nchain * r * (dv + 2 * LANES) * 4,
                        temp_bytes=6 * group * t * t * 4), 2),
        name=name,
    )(*[arr for arr, _, _ in operands])


def _moba_key_onehot(S):
    nb = S // MOBA_BLOCK
    lanes = np.arange(LANES)[None, :]
    blk_of = (np.arange(S) // MOBA_BLOCK)[:, None]
    onehot = (lanes % nb == blk_of) & (lanes < 2 * nb)
    return jnp.asarray(onehot.astype(np.float32)[None], dtype=BF16)


def _mm_res_ln_kernel(*refs, n_a, chunks, layer, kc):
    a_refs = refs[:n_a]
    w_hbm, x_ref, g_ref, b_ref, o_ref, obf_ref, w_ref, stage_ref, sem = refs[n_a:]

    @pl.when(pl.program_id(0) == 0)
    def _():
        n_chunks = w_ref.shape[0] // kc

        def chunk_copy(c):
            return pltpu.make_async_copy(w_hbm.at[layer, pl.ds(c * kc, kc), :],
                                         stage_ref.at[c % 2], sem.at[c % 2])

        chunk_copy(0).start()
        for c in range(n_chunks):
            if c + 1 < n_chunks:
                chunk_copy(c + 1).start()
            chunk_copy(c).wait()
            w_ref[c * kc:(c + 1) * kc, :] = stage_ref[c % 2].astype(BF16)

    rc = x_ref.shape[0] // chunks
    for c in range(chunks):
        rows = slice(c * rc, (c + 1) * rc)
        y, row = None, 0
        for a_ref in a_refs:
            ka = a_ref.shape[1]
            d = jnp.dot(a_ref[rows, :], w_ref[row:row + ka, :], preferred_element_type=F32)
            y = d if y is None else y + d
            row += ka
        z = DEEPNORM_ALPHA * x_ref[rows, :] + y
        mu = jnp.mean(z, axis=-1, keepdims=True)
        zc = z - mu
        var = jnp.mean(zc * zc, axis=-1, keepdims=True)
        out = zc * lax.rsqrt(var + LN_EPS) * g_ref[...] + b_ref[...]
        o_ref[rows, :] = out
        obf_ref[rows, :] = out.astype(BF16)


def mm_res_ln(a_list, w_stack, layer, x, g, b, *, tm, chunks=2, kc=512):
    M = a_list[0].shape[0]
    K = sum(a.shape[1] for a in a_list)
    N = w_stack.shape[2]
    assert w_stack.shape[1] == K and K % kc == 0
    blk = tm * K * 2 + tm * N * 4 + 2 * N * 4 + tm * N * 4 + tm * N * 2
    return pl.pallas_call(
        functools.partial(_mm_res_ln_kernel, n_a=len(a_list), chunks=chunks, layer=layer,
                          kc=kc),
        grid=(M // tm,),
        in_specs=[pl.BlockSpec((tm, a.shape[1]), lambda i: (i, 0)) for a in a_list] + [
            pl.BlockSpec(memory_space=pl.ANY),
            pl.BlockSpec((tm, N), lambda i: (i, 0)),
            pl.BlockSpec((1, N), lambda i: (0, 0)),
            pl.BlockSpec((1, N), lambda i: (0, 0))],
        out_specs=[pl.BlockSpec((tm, N), lambda i: (i, 0)),
                   pl.BlockSpec((tm, N), lambda i: (i, 0))],
        out_shape=[jax.ShapeDtypeStruct((M, N), F32),
                   jax.ShapeDtypeStruct((M, N), BF16)],
        scratch_shapes=[pltpu.VMEM((K, N), BF16), pltpu.VMEM((2, kc, N), F32),
                        pltpu.SemaphoreType.DMA((2,))],
        compiler_params=_params(
            _vmem_limit(blk, K * N * 2 + 2 * kc * N * 4, temp_bytes=3 * tm * N * 4), 1),
        name="mm_res_ln",
    )(*a_list, w_stack, x, g, b)


def _ffn_gu_kernel(x_ref, wg_ref, wu_ref, o_ref, wgb_ref, wub_ref):
    @pl.when(pl.program_id(1) == 0)
    def _():
        wgb_ref[...] = wg_ref[...].astype(BF16)
        wub_ref[...] = wu_ref[...].astype(BF16)

    x = x_ref[...]
    g = jnp.dot(x, wgb_ref[...], preferred_element_type=F32)
    u = jnp.dot(x, wub_ref[...], preferred_element_type=F32)
    o_ref[...] = (g * jax.nn.sigmoid(g) * u).astype(o_ref.dtype)


def ffn_gate_up(xb, wg_stack, wu_stack, layer, *, tm, tf):
    M, K = xb.shape
    F = wg_stack.shape[2]
    blk = tm * K * 2 + 2 * K * tf * 4 + tm * tf * 2
    w_spec = pl.BlockSpec((None, K, tf), lambda j, i: (layer, 0, j))
    return pl.pallas_call(
        _ffn_gu_kernel,
        grid=(F // tf, M // tm),
        in_specs=[pl.BlockSpec((tm, K), lambda j, i: (i, 0)), w_spec, w_spec],
        out_specs=pl.BlockSpec((tm, tf), lambda j, i: (i, j)),
        out_shape=jax.ShapeDtypeStruct((M, F), BF16),
        scratch_shapes=[pltpu.VMEM((K, tf), BF16), pltpu.VMEM((K, tf), BF16)],
        compiler_params=_params(
            _vmem_limit(blk, 2 * K * tf * 2, temp_bytes=4 * tm * tf * 4), 2),
        name="ffn_gate_up",
    )(xb, wg_stack, wu_stack)


def _rope_lanes(r, cos_t, sin_a, sin_b):
    return (r * cos_t + pltpu.roll(r, LANES - MLA_ROPE // 2, 1) * sin_a
            + pltpu.roll(r, MLA_ROPE // 2, 1) * sin_b)


def _rms(x, g):
    return x * lax.rsqrt(jnp.mean(x * x, axis=-1, keepdims=True) + RMS_EPS) * g


def _fetch_weight_once(w_hbm, layer, stage_ref, sem):
    copy = pltpu.make_async_copy(w_hbm.at[layer], stage_ref, sem.at[0])
    copy.start()
    copy.wait()


def _mla_in_kernel(x_ref, w_hbm, gq_ref, gkv_ref, cos_ref, sa_ref, sb_ref,
                   cq_ref, ckv_ref, kr_ref, w_ref, stage_ref, sem, *, layer):
    @pl.when(pl.program_id(0) == 0)
    def _():
        _fetch_weight_once(w_hbm, layer, stage_ref, sem)
        n_in = stage_ref.shape[1]
        w_ref[:, 0:n_in] = stage_ref[...].astype(BF16)
        w_ref[:, n_in:] = jnp.zeros((w_ref.shape[0], w_ref.shape[1] - n_in), BF16)

    half = x_ref.shape[0] // 2
    for rows in (slice(0, half), slice(half, 2 * half)):
        h = jnp.dot(x_ref[rows, :], w_ref[...], preferred_element_type=F32)
        cq_ref[rows, :] = _rms(h[:, :MLA_Q_RANK], gq_ref[...]).astype(BF16)
        ckv_ref[rows, :] = _rms(h[:, MLA_Q_RANK:MLA_Q_RANK + MLA_KV_RANK],
                                gkv_ref[...]).astype(BF16)
        r = h[:, MLA_Q_RANK + MLA_KV_RANK:]
        kr_ref[rows, :] = _rope_lanes(r, cos_ref[rows, :], sa_ref[rows, :],
                                      sb_ref[rows, :]).astype(BF16)


def mla_in(xb, w_stack, layer, gq, gkv, cos_t, sin_a, sin_b, *, tm=512):
    S, K = xb.shape
    n_in = w_stack.shape[2]
    N = MLA_Q_RANK + MLA_KV_RANK + LANES
    blk = (tm * K * 2 + 3 * tm * LANES * 4
           + tm * (MLA_Q_RANK + MLA_KV_RANK + LANES) * 2)
    row = lambda i: (i, 0)
    fixed = lambda i: (0, 0)
    return pl.pallas_call(
        functools.partial(_mla_in_kernel, layer=layer),
        grid=(S // tm,),
        in_specs=[pl.BlockSpec((tm, K), row), pl.BlockSpec(memory_space=pl.ANY),
                  pl.BlockSpec((1, MLA_Q_RANK), fixed), pl.BlockSpec((1, MLA_KV_RANK), fixed),
                  pl.BlockSpec((tm, LANES), row), pl.BlockSpec((tm, LANES), row),
                  pl.BlockSpec((tm, LANES), row)],
        out_specs=[pl.BlockSpec((tm, MLA_Q_RANK), row), pl.BlockSpec((tm, MLA_KV_RANK), row),
                   pl.BlockSpec((tm, LANES), row)],
        out_shape=[jax.ShapeDtypeStruct((S, MLA_Q_RANK), BF16),
                   jax.ShapeDtypeStruct((S, MLA_KV_RANK), BF16),
                   jax.ShapeDtypeStruct((S, LANES), BF16)],
        scratch_shapes=[pltpu.VMEM((K, N), BF16), pltpu.VMEM((K, n_in), F32),
                        pltpu.SemaphoreType.DMA((1,))],
        compiler_params=_params(
            _vmem_limit(blk, K * N * 2 + K * N * 4, temp_bytes=3 * tm * N * 4), 1),
        name="mla_in",
    )(xb, w_stack, gq, gkv, cos_t, sin_a, sin_b)


def _mla_q_kernel(cq_ref, w_hbm, cos_ref, sa_ref, sb_ref, q_ref, w_ref, stage_ref, sem,
                  *, qscale, layer):
    @pl.when(pl.program_id(0) == 0)
    def _():
        _fetch_weight_once(w_hbm, layer, stage_ref, sem)
        wf = stage_ref[...]
        dh = MLA_NOPE + MLA_ROPE
        w_ref[...] = jnp.zeros_like(w_ref)
        for hh in range(q_ref.shape[0]):
            w_ref[:, hh * 2 * LANES:hh * 2 * LANES + dh] = wf[:, hh * dh:(hh + 1) * dh].astype(BF16)

    acc = jnp.dot(cq_ref[...], w_ref[...], preferred_element_type=F32) * qscale
    cos_t, sin_a, sin_b = cos_ref[...], sa_ref[...], sb_ref[...]
    for hh in range(q_ref.shape[0]):
        base = hh * 2 * LANES
        q_ref[hh, :, 0:LANES] = acc[:, base:base + LANES].astype(BF16)
        r = acc[:, base + LANES:base + 2 * LANES]
        q_ref[hh, :, LANES:2 * LANES] = _rope_lanes(r, cos_t, sin_a, sin_b).astype(BF16)


def mla_q(cq, w_stack, layer, cos_t, sin_a, sin_b, *, qscale, tm=512):
    S, K = cq.shape
    n_in = w_stack.shape[2]
    nh = n_in // (MLA_NOPE + MLA_ROPE)
    N = nh * 2 * LANES
    blk = tm * K * 2 + 3 * tm * LANES * 4 + tm * N * 2
    row = lambda i: (i, 0)
    return pl.pallas_call(
        functools.partial(_mla_q_kernel, qscale=qscale, layer=layer),
        grid=(S // tm,),
        in_specs=[pl.BlockSpec((tm, K), row), pl.BlockSpec(memory_space=pl.ANY),
                  pl.BlockSpec((tm, LANES), row), pl.BlockSpec((tm, LANES), row),
                  pl.BlockSpec((tm, LANES), row)],
        out_specs=pl.BlockSpec((nh, tm, 2 * LANES), lambda i: (0, i, 0)),
        out_shape=jax.ShapeDtypeStruct((nh, S, 2 * LANES), BF16),
        scratch_shapes=[pltpu.VMEM((K, N), BF16), pltpu.VMEM((K, n_in), F32),
                        pltpu.SemaphoreType.DMA((1,))],
        compiler_params=_params(
            _vmem_limit(blk, K * N * 2 + K * n_in * 4, temp_bytes=2 * tm * N * 4 + K * n_in * 4),
            1),
        name="mla_q",
    )(cq, w_stack, cos_t, sin_a, sin_b)


def _mla_kv_kernel(ckv_ref, w_hbm, kr_ref, k_ref, v_ref, w_ref, stage_ref, sem, *, layer):
    @pl.when(pl.program_id(0) == 0)
    def _():
        _fetch_weight_once(w_hbm, layer, stage_ref, sem)
        w_ref[...] = stage_ref[...].astype(BF16)

    acc = jnp.dot(ckv_ref[...], w_ref[...], preferred_element_type=F32)
    kr = kr_ref[...]
    for hh in range(k_ref.shape[0]):
        base = hh * (MLA_NOPE + MLA_V)
        k_ref[hh, :, 0:LANES] = acc[:, base:base + MLA_NOPE].astype(BF16)
        k_ref[hh, :, LANES:2 * LANES] = kr
        v_ref[hh] = acc[:, base + MLA_NOPE:base + MLA_NOPE + MLA_V].astype(BF16)


def mla_kv(ckv, w_stack, layer, kr, *, tm=512):
    S, K = ckv.shape
    N = w_stack.shape[2]
    nh = N // (MLA_NOPE + MLA_V)
    blk = tm * K * 2 + tm * LANES * 2 + nh * tm * (2 * LANES + MLA_V) * 2
    row = lambda i: (i, 0)
    return pl.pallas_call(
        functools.partial(_mla_kv_kernel, layer=layer),
        grid=(S // tm,),
        in_specs=[pl.BlockSpec((tm, K), row), pl.BlockSpec(memory_space=pl.ANY),
                  pl.BlockSpec((tm, LANES), row)],
        out_specs=[pl.BlockSpec((nh, tm, 2 * LANES), lambda i: (0, i, 0)),
                   pl.BlockSpec((nh, tm, MLA_V), lambda i: (0, i, 0))],
        out_shape=[jax.ShapeDtypeStruct((nh, S, 2 * LANES), BF16),
                   jax.ShapeDtypeStruct((nh, S, MLA_V), BF16)],
        scratch_shapes=[pltpu.VMEM((K, N), BF16), pltpu.VMEM((K, N), F32),
                        pltpu.SemaphoreType.DMA((1,))],
        compiler_params=_params(_vmem_limit(blk, K * N * 6, temp_bytes=2 * tm * N * 4), 1),
        name="mla_kv",
    )(ckv, w_stack, kr)


def _rope_lane_tables(S):
    half = MLA_ROPE // 2
    inv = ROPE_THETA ** (-np.arange(0, MLA_ROPE, 2, dtype=np.float64) / MLA_ROPE)
    ang = np.arange(S, dtype=np.float64)[:, None] * inv[None, :]
    cos, sin = np.cos(ang), np.sin(ang)
    z = np.zeros((S, half))
    z2 = np.zeros((S, LANES - MLA_ROPE))
    cos_t = np.concatenate([cos, cos, z2], axis=1)
    sin_a = np.concatenate([-sin, z, z2], axis=1)
    sin_b = np.concatenate([z, sin, z2], axis=1)
    return tuple(jnp.asarray(a.astype(np.float32)) for a in (cos_t, sin_a, sin_b))


def _pad_cols(w, n):
    return jnp.pad(w, ((0, 0), (0, n - w.shape[1])))


def kernel(x, ab_w_in, ab_forget_bias, ab_w_out, rel_bias, mla_w_in, mla_q_norm,
           mla_kv_norm, mla_w_uq, mla_w_ukv, mla_w_out, ffn_w_gate, ffn_w_up,
           ffn_w_down, ln_g, ln_b):
    S = x.shape[1]
    xf = x.reshape(S, D_MODEL)
    xb = xf.astype(BF16)
    cos_t, sin_a, sin_b = _rope_lane_tables(S)
    nf = 3 * DA + N_HEADS_A
    qs = HEAD_DIM ** -0.5 * LOG2E
    colscale = np.ones((1, 3 * DA + 3 * DB), np.float32)
    colscale[:, :DA] = qs
    colscale[:, 3 * DA:3 * DA + DB] = qs
    colscale = jnp.asarray(colscale)
    causal_b2 = _causal_bias(ATTN_TILE)
    moba_b2 = moba_bias(rel_bias, n_heads=N_HEADS_B, t=ATTN_TILE)
    moba_kaug = _moba_key_onehot(S)
    hb = 3 * N_HEADS_A

    for layer in range(DEPTH):
        j = layer // 2
        g0, b0 = ln_g[layer, 0][None, :], ln_b[layer, 0][None, :]
        g1, b1 = ln_g[layer, 1][None, :], ln_b[layer, 1][None, :]
        if layer % 2 == 0:
            w_in = ab_w_in[j]
            w_qkv = jnp.concatenate([w_in[:, :3 * DA], w_in[:, nf:]], axis=1).astype(BF16)
            b_f = _pad_cols(ab_forget_bias[j][None, :], LANES)
            heads = mm_heads(xb, w_qkv, colscale, tm=1024, tn=1024)
            qaug, kaug = fox_gate(xb, ab_w_in, j, 3 * DA, b_f)
            ya = flash_attention([(heads, 0), (qaug, 0)], [(heads, N_HEADS_A), (kaug, 0)],
                                 (heads, 2 * N_HEADS_A), causal_b2,
                                 n_heads=N_HEADS_A, name="flash_fox")
            qaug_b = moba_gate(heads, rel_bias, n_heads=N_HEADS_B, q_off=hb,
                               k_off=hb + N_HEADS_B)
            yb = flash_attention([(heads, hb), (qaug_b, 0)],
                                 [(heads, hb + N_HEADS_B), (moba_kaug, None)],
                                 (heads, hb + 2 * N_HEADS_B), moba_b2,
                                 n_heads=N_HEADS_B, name="flash_moba", prev_biased=True)
            y, w_out = [ya, yb], ab_w_out
        else:
            cq, ckv, kr = mla_in(xb, mla_w_in, j, mla_q_norm[j][None, :],
                                 mla_kv_norm[j][None, :], cos_t, sin_a, sin_b)
            q_full = mla_q(cq, mla_w_uq, j, cos_t, sin_a, sin_b,
                           qscale=(MLA_NOPE + MLA_ROPE) ** -0.5 * LOG2E)
            k_full, v = mla_kv(ckv, mla_w_ukv, j, kr)
            y = [flash_attention([(q_full, 0)], [(k_full, 0)], (v, 0), causal_b2,
                                 n_heads=MLA_HEADS, name="flash_mla")]
            w_out = mla_w_out
        xf, xb = mm_res_ln(y, w_out, j, xf, g0, b0, tm=512)
        hmid = ffn_gate_up(xb, ffn_w_gate, ffn_w_up, layer, tm=1024, tf=512)
        xf, xb = mm_res_ln([hmid], ffn_w_down, layer, xf, g1, b1, tm=256)
    return xf.reshape(1, S, D_MODEL)
```

```python
import functools
import math

import numpy as np
import jax
import jax.numpy as jnp
from jax import lax
from jax.experimental import pallas as pl
from jax.experimental.pallas import tpu as pltpu

F32 = jnp.float32
BF16 = jnp.bfloat16

D_MODEL = 2048
DEPTH = 4
HEAD_DIM = 128
N_HEADS_A = 8
N_HEADS_B = 8
MOBA_BLOCK = 256
MOBA_TOPK = 3
N_BUCKETS = 32
MAX_DISTANCE = 128
MLA_HEADS = 16
MLA_Q_RANK = 512
MLA_KV_RANK = 512
MLA_NOPE = 128
MLA_ROPE = 64
MLA_V = 128
ROPE_THETA = 10000.0
DEEPNORM_ALPHA = (2 * DEPTH) ** 0.25
DA = N_HEADS_A * HEAD_DIM
DB = N_HEADS_B * HEAD_DIM
LN_EPS = 1e-5
RMS_EPS = 1e-6
LOG2E = math.log2(math.e)

LANES = 128
VMEM_BUDGET_BYTES = 56 * 2**20
VMEM_FLOOR_BYTES = 16 * 2**20
VMEM_SPILL_BYTES = 4 * 2**20
MASK_VALUE = -1e30
ATTN_TILE = 2 * MOBA_BLOCK


def _vmem_limit(block_bytes, scratch_bytes=0, temp_bytes=0):
    est = 2 * block_bytes + scratch_bytes + temp_bytes + VMEM_SPILL_BYTES
    return int(min(max(est, VMEM_FLOOR_BYTES), VMEM_BUDGET_BYTES))


def _params(vmem_bytes, ngrid):
    return pltpu.CompilerParams(
        dimension_semantics=("arbitrary",) * ngrid, vmem_limit_bytes=vmem_bytes)


def _split3(x):
    hi = x.astype(BF16).astype(F32)
    r1 = x - hi
    lo = r1.astype(BF16).astype(F32)
    return hi, lo, r1 - lo


def _qk(a, b):
    return lax.dot_general(a, b, (((1,), (1,)), ((), ())), preferred_element_type=F32)


def _mm_heads_kernel(x_ref, w_ref, cs_ref, o_ref, *, width):
    acc = jnp.dot(x_ref[...], w_ref[...], preferred_element_type=F32) * cs_ref[...]
    for hh in range(o_ref.shape[0]):
        o_ref[hh] = acc[:, hh * width:(hh + 1) * width].astype(o_ref.dtype)


def mm_heads(x, w, colscale, *, tm, tn, width=HEAD_DIM):
    M, K = x.shape
    N = w.shape[1]
    nh = tn // width
    blk = tm * K * 2 + K * tn * 2 + tm * tn * 2 + tn * 4
    return pl.pallas_call(
        functools.partial(_mm_heads_kernel, width=width),
        grid=(M // tm, N // tn),
        in_specs=[pl.BlockSpec((tm, K), lambda i, j: (i, 0)),
                  pl.BlockSpec((K, tn), lambda i, j: (0, j)),
                  pl.BlockSpec((1, tn), lambda i, j: (0, j))],
        out_specs=pl.BlockSpec((nh, tm, width), lambda i, j: (j, i, 0)),
        out_shape=jax.ShapeDtypeStruct((N // width, M, width), BF16),
        compiler_params=_params(_vmem_limit(blk, temp_bytes=2 * tm * tn * 4), 2),
        name="mm_heads",
    )(x, w, colscale)


def _fox_gate_kernel(x_ref, wf_ref, bf_ref, selq_ref, selk_ref, cq_ref, ck_ref,
                     qaug_ref, kaug_ref, carry_ref, *, tb, nh):
    @pl.when(pl.program_id(0) == 0)
    def _():
        carry_ref[...] = jnp.zeros_like(carry_ref)

    z = jnp.dot(x_ref[...], wf_ref[...], preferred_element_type=F32) + bf_ref[...]
    lf = jnp.minimum(z, 0.0) - jnp.log1p(jnp.exp(-jnp.abs(z)))
    row = lax.broadcasted_iota(jnp.int32, (tb, tb), 0)
    col = lax.broadcasted_iota(jnp.int32, (tb, tb), 1)
    tri = jnp.where(row >= col, 1.0, 0.0).astype(BF16)
    lf_terms = jnp.concatenate(_split3(lf), axis=1).astype(BF16)
    part = jnp.dot(tri, lf_terms, preferred_element_type=F32)
    cs = (part[:, :LANES] + part[:, LANES:2 * LANES] + part[:, 2 * LANES:]) + carry_ref[...]
    carry_ref[...] = cs[tb - 1:tb, :]
    lane = lax.broadcasted_iota(jnp.int32, (tb, LANES), 1)
    hi, lo, lo2 = [jnp.where(lane < nh, term, 0.0) for term in _split3(cs * LOG2E)]
    terms = (hi + pltpu.roll(lo, nh, 1) + pltpu.roll(lo2, 2 * nh, 1)).astype(BF16)
    qa = jnp.dot(terms, selq_ref[...], preferred_element_type=F32) + cq_ref[...]
    ka = jnp.dot(terms, selk_ref[...], preferred_element_type=F32) + ck_ref[...]
    for h in range(nh):
        qaug_ref[h] = qa[:, h * LANES:(h + 1) * LANES].astype(BF16)
        kaug_ref[h] = ka[:, h * LANES:(h + 1) * LANES].astype(BF16)


def _fox_selectors(nh):
    selq = np.zeros((LANES, nh * LANES), np.float32)
    selk = np.zeros((LANES, nh * LANES), np.float32)
    cq = np.zeros((1, nh * LANES), np.float32)
    ck = np.zeros((1, nh * LANES), np.float32)
    for h in range(nh):
        for term in range(3):
            selq[term * nh + h, h * LANES + term] = 1.0
            selk[term * nh + h, h * LANES + 3 + term] = -1.0
        cq[0, h * LANES + 3:h * LANES + 6] = 1.0
        ck[0, h * LANES:h * LANES + 3] = 1.0
    return (jnp.asarray(selq, dtype=BF16), jnp.asarray(selk, dtype=BF16),
            jnp.asarray(cq), jnp.asarray(ck))


def fox_gate(xb, wf_pad, bf_pad, *, tb=512, nh=N_HEADS_A):
    S, K = xb.shape
    selq, selk, cq, ck = _fox_selectors(nh)
    blk = (tb * K * 2 + K * LANES * 2 + 2 * nh * tb * LANES * 2
           + 2 * LANES * nh * LANES * 2)
    aug = jax.ShapeDtypeStruct((nh, S, LANES), BF16)
    fixed = lambda i: (0, 0)
    return pl.pallas_call(
        functools.partial(_fox_gate_kernel, tb=tb, nh=nh),
        grid=(S // tb,),
        in_specs=[pl.BlockSpec((tb, K), lambda i: (i, 0)),
                  pl.BlockSpec((K, LANES), fixed),
                  pl.BlockSpec((1, LANES), fixed),
                  pl.BlockSpec(selq.shape, fixed), pl.BlockSpec(selk.shape, fixed),
                  pl.BlockSpec(cq.shape, fixed), pl.BlockSpec(ck.shape, fixed)],
        out_specs=[pl.BlockSpec((nh, tb, LANES), lambda i: (0, i, 0)),
                   pl.BlockSpec((nh, tb, LANES), lambda i: (0, i, 0))],
        out_shape=[aug, aug],
        scratch_shapes=[pltpu.VMEM((1, LANES), F32)],
        compiler_params=_params(
            _vmem_limit(blk, temp_bytes=4 * tb * tb * 4 + 4 * tb * nh * LANES * 4), 1),
        name="fox_gate",
    )(xb, wf_pad, bf_pad, selq, selk, cq, ck)


def _moba_gate_kernel(rb_ref, q_ref, k_ref, qaug_ref, kmean_ref, *, nb, tr):
    B = MOBA_BLOCK
    h = pl.program_id(0)
    i = pl.program_id(1)
    far_bias = rb_ref[N_BUCKETS - 1, h] * LOG2E

    @pl.when(i == 0)
    def _():
        kf = k_ref[0].astype(F32).reshape(nb, B, HEAD_DIM)
        terms = _split3(jnp.sum(kf, axis=1) * (1.0 / B))
        kmean_ref[...] = jnp.concatenate(terms, axis=0).astype(BF16)

    blk = lax.broadcasted_iota(jnp.int32, (nb, tr), 0)
    blk_f = blk.astype(F32)
    own = i * (tr // B) + lax.broadcasted_iota(jnp.int32, (nb, tr), 1) // B
    g3 = _qk(kmean_ref[...], q_ref[0])
    g = g3[0:nb] + g3[nb:2 * nb] + g3[2 * nb:3 * nb]
    neg_inf = -jnp.inf
    g = jnp.where(blk < own, g, neg_inf)
    mb = jnp.where(blk == own, 0.0, MASK_VALUE).astype(F32)
    for _ in range(MOBA_TOPK):
        mx = jnp.max(g, axis=0, keepdims=True)
        hit = jnp.logical_and(g == mx, mx > neg_inf)
        idx = jnp.min(jnp.where(hit, blk_f, float(nb)), axis=0, keepdims=True)
        pick = blk_f == idx
        mb = jnp.where(pick, far_bias, mb)
        g = jnp.where(pick, neg_inf, g)
    mb_hi = mb.astype(BF16).astype(F32)
    mb_lo = jnp.where(mb > 0.5 * MASK_VALUE, mb - mb_hi, 0.0)
    aug_t = jnp.concatenate([mb_hi, mb_lo, jnp.zeros((LANES - 2 * nb, tr), F32)], axis=0)
    qaug_ref[0] = jnp.transpose(aug_t).astype(BF16)


def moba_gate(heads, rel_bias, *, n_heads, q_off, k_off, tr=1024):
    S = heads.shape[1]
    nb = S // MOBA_BLOCK
    assert 2 * nb <= LANES
    blk = tr * HEAD_DIM * 2 + S * HEAD_DIM * 2 + tr * LANES * 2
    return pl.pallas_call(
        functools.partial(_moba_gate_kernel, nb=nb, tr=tr),
        grid=(n_heads, S // tr),
        in_specs=[pl.BlockSpec(memory_space=pltpu.SMEM),
                  pl.BlockSpec((1, tr, HEAD_DIM), lambda h, i: (h + q_off, i, 0)),
                  pl.BlockSpec((1, S, HEAD_DIM), lambda h, i: (h + k_off, 0, 0))],
        out_specs=pl.BlockSpec((1, tr, LANES), lambda h, i: (h, i, 0)),
        out_shape=jax.ShapeDtypeStruct((n_heads, S, LANES), BF16),
        scratch_shapes=[pltpu.VMEM((3 * nb, HEAD_DIM), BF16)],
        compiler_params=_params(
            _vmem_limit(blk, 3 * nb * HEAD_DIM * 2,
                        temp_bytes=S * HEAD_DIM * 4 + 12 * tr * LANES * 4), 2),
        name="moba_gate",
    )(rel_bias, heads, heads)


def _moba_bias_kernel(rb_ref, own_ref, prev_ref, o_ref):
    B = MOBA_BLOCK
    h = pl.program_id(0)
    far = rb_ref[N_BUCKETS - 1, h]
    own_bucket = own_ref[...]
    prev_bucket = prev_ref[...]
    own = jnp.zeros((B, B), F32)
    prev = jnp.zeros((B, B), F32)
    for b in range(N_BUCKETS):
        val = rb_ref[b, h]
        own = jnp.where(own_bucket == b, val, own)
        prev = jnp.where(prev_bucket == b, val, prev)
    own = jnp.where(own_bucket < 0, MASK_VALUE, own * LOG2E)
    prev = (prev - far) * LOG2E
    o_ref[0] = jnp.zeros(o_ref.shape[1:], F32)
    o_ref[0, 0:B, B:2 * B] = prev
    o_ref[0, 0:B, 2 * B:3 * B] = own
    o_ref[0, B:2 * B, 2 * B:3 * B] = prev
    o_ref[0, B:2 * B, 3 * B:4 * B] = own


def _t5_bucket_table(n):
    rel = np.arange(n)
    max_exact = N_BUCKETS // 2
    nf = np.maximum(rel, 1).astype(np.float32)
    large = max_exact + (np.log(nf / np.float32(max_exact))
                         / np.float32(math.log(MAX_DISTANCE / max_exact))
                         * np.float32(N_BUCKETS - max_exact)).astype(np.int32)
    large = np.minimum(large, N_BUCKETS - 1)
    return np.where(rel < max_exact, rel, large).astype(np.int32)


def moba_bias(rel_bias, *, n_heads, t):
    B = MOBA_BLOCK
    assert t == 2 * B
    table = _t5_bucket_table(2 * B)
    rel = np.arange(B)[:, None] - np.arange(B)[None, :]
    own_bucket = np.where(rel >= 0, table[np.maximum(rel, 0)], -1).astype(np.int32)
    prev_bucket = table[rel + B].astype(np.int32)
    blk = t * 2 * t * 4 + 2 * B * B * 4
    return pl.pallas_call(
        _moba_bias_kernel,
        grid=(n_heads,),
        in_specs=[pl.BlockSpec(memory_space=pltpu.SMEM),
                  pl.BlockSpec((B, B), lambda h: (0, 0)),
                  pl.BlockSpec((B, B), lambda h: (0, 0))],
        out_specs=pl.BlockSpec((1, t, 2 * t), lambda h: (h, 0, 0)),
        out_shape=jax.ShapeDtypeStruct((n_heads, t, 2 * t), F32),
        compiler_params=_params(_vmem_limit(blk, temp_bytes=8 * B * B * 4), 1),
        name="moba_bias",
    )(rel_bias, jnp.asarray(own_bucket), jnp.asarray(prev_bucket))


def _causal_bias(t):
    rows = np.arange(t)[:, None]
    cols = np.arange(2 * t)[None, :] - t
    return jnp.asarray(np.where(cols <= rows, 0.0, MASK_VALUE).astype(np.float32)[None])


FINAL_TILES = 4


def _ones_column(rows):
    lane = lax.broadcasted_iota(jnp.int32, (rows, LANES), 1)
    return jnp.where(lane == 0, 1.0, 0.0).astype(BF16)


def _flash_kernel(*refs, t, nparts, nsplit, group, prev_biased):
    q_refs = refs[:nparts]
    k_refs = refs[nparts:2 * nparts]
    v_ref, b2_ref, o_ref, m_ref, acc_ref = refs[2 * nparts:]
    qi = pl.program_id(1)
    dv = v_ref.shape[-1]
    r = t // nsplit
    chains = [(g, c) for g in range(group) for c in range(nsplit)]

    def head(ref, g):
        return g if ref.shape[0] == group else 0

    qs = [jnp.concatenate([ref[head(ref, g), c * r:(c + 1) * r, :] for ref in q_refs], axis=1)
          for g, c in chains]
    m_ref[...] = jnp.full_like(m_ref, MASK_VALUE)
    acc_ref[...] = jnp.zeros_like(acc_ref)

    def step(tile, width, n_biased=0):
        ks = pl.multiple_of(tile * t, t)
        ones = _ones_column(width)
        bw = n_biased * t
        ss = []
        for n, (g, c) in enumerate(chains):
            k = jnp.concatenate([ref[head(ref, g), pl.ds(ks, width), :] for ref in k_refs],
                                axis=1)
            s = _qk(qs[n], k)
            if bw:
                bias = b2_ref[head(b2_ref, g), c * r:(c + 1) * r, 2 * t - bw:2 * t]
                tail = s[:, width - bw:] + bias
                s = tail if bw == width else jnp.concatenate([s[:, :width - bw], tail], axis=1)
            ss.append(s)
        for n, (g, c) in enumerate(chains):
            v1 = jnp.concatenate([v_ref[g, pl.ds(ks, width), :], ones], axis=1)
            m = m_ref[n]
            m_new = jnp.maximum(m, jnp.max(ss[n], axis=1, keepdims=True))
            p = jnp.exp2(ss[n] - jnp.concatenate([m_new] * (width // LANES), axis=1))
            a = jnp.exp2(m - m_new)
            acc_ref[n] = (jnp.concatenate([a] * ((dv + LANES) // LANES), axis=1) * acc_ref[n]
                          + jnp.dot(p.astype(BF16), v1, preferred_element_type=F32))
            m_ref[n] = m_new

    n_biased = jnp.minimum(2 if prev_biased else 1, qi + 1)
    lead = qi + 1 - n_biased
    quads = lead // FINAL_TILES

    def body(kb, carry):
        step(FINAL_TILES * kb, FINAL_TILES * t)
        return carry

    lax.fori_loop(0, quads, body, 0)
    start = quads * FINAL_TILES
    rest = qi + 1 - start
    if prev_biased:
        @pl.when(rest > FINAL_TILES)
        def _():
            step(start, t)
    spill = (rest > FINAL_TILES).astype(jnp.int32)
    for width_tiles in range(1, FINAL_TILES + 1):
        @pl.when(rest - spill == width_tiles)
        def _(width_tiles=width_tiles):
            step(start + spill, width_tiles * t,
                 n_biased=min(2 if prev_biased else 1, width_tiles))
    for g in range(group):
        out = []
        for c in range(nsplit):
            acc = acc_ref[g * nsplit + c]
            out.append(acc[:, :dv] * (1.0 / acc[:, dv:dv + 1]))
        o_ref[:, g * dv:(g + 1) * dv] = jnp.concatenate(out, axis=0).astype(o_ref.dtype)


def flash_attention(q_parts, k_parts, v_part, b2, *, n_heads, name, t=ATTN_TILE, nsplit=1,
                    group=2, prev_biased=False):
    v, v_off = v_part
    S, dv = v.shape[1], v.shape[2]

    def spec(arr, off, rows):
        g = group if off is not None and arr.shape[0] > 1 else 1
        assert off is None or off % group == 0
        blk_shape = (g, t if rows else arr.shape[1], arr.shape[2])
        if g == 1:
            index = lambda h, i: (0, i if rows else 0, 0)
        else:
            index = lambda h, i: (h + off // group, i if rows else 0, 0)
        return pl.BlockSpec(blk_shape, index), math.prod(blk_shape) * arr.dtype.itemsize

    operands = ([(a, o, True) for a, o in q_parts] + [(a, o, False) for a, o in k_parts]
                + [(v, v_off, False), (b2, 0 if b2.shape[0] > 1 else None, False)])
    in_specs, blk = [], t * group * dv * 2
    for arr, off, rows in operands:
        s, nbytes = spec(arr, off, rows)
        in_specs.append(s)
        blk += nbytes
    nchain = group * nsplit
    r = t // nsplit
    return pl.pallas_call(
        functools.partial(_flash_kernel, t=t, nparts=len(q_parts), nsplit=nsplit, group=group,
                          prev_biased=prev_biased),
        grid=(n_heads // group, S // t),
        in_specs=in_specs,
        out_specs=pl.BlockSpec((t, group * dv), lambda h, i: (i, h)),
        out_shape=jax.ShapeDtypeStruct((S, n_heads * dv), BF16),
        scratch_shapes=[pltpu.VMEM((nchain, r, LANES), F32),
                        pltpu.VMEM((nchain, r, dv + LANES), F32)],
        compiler_params=_params(
            _vmem_limit(blk, nchain * r * (dv + 2 * LANES) * 4,
                        temp_bytes=6 * group * t * t * 4), 2),
        name=name,
    )(*[arr for arr, _, _ in operands])


def _moba_key_onehot(S):
    nb = S // MOBA_BLOCK
    lanes = np.arange(LANES)[None, :]
    blk_of = (np.arange(S) // MOBA_BLOCK)[:, None]
    onehot = (lanes % nb == blk_of) & (lanes < 2 * nb)
    return jnp.asarray(onehot.astype(np.float32)[None], dtype=BF16)


def _mm_res_ln_kernel(*refs, n_a, chunks, layer, kc):
    a_refs = refs[:n_a]
    w_hbm, x_ref, g_ref, b_ref, o_ref, obf_ref, w_ref, stage_ref, sem = refs[n_a:]

    def finish(rows, y):
        z = DEEPNORM_ALPHA * x_ref[rows, :] + y
        mu = jnp.mean(z, axis=-1, keepdims=True)
        zc = z - mu
        var = jnp.mean(zc * zc, axis=-1, keepdims=True)
        out = zc * lax.rsqrt(var + LN_EPS) * g_ref[...] + b_ref[...]
        o_ref[rows, :] = out
        obf_ref[rows, :] = out.astype(BF16)

    @pl.when(pl.program_id(0) == 0)
    def _():
        n_chunks = w_ref.shape[0] // kc

        def chunk_copy(c):
            return pltpu.make_async_copy(w_hbm.at[layer, pl.ds(c * kc, kc), :],
                                         stage_ref.at[c % 2], sem.at[c % 2])

        a_chunks = [(a_ref, off) for a_ref in a_refs for off in range(0, a_ref.shape[1], kc)]
        chunk_copy(0).start()
        y = None
        for c in range(n_chunks):
            if c + 1 < n_chunks:
                chunk_copy(c + 1).start()
            chunk_copy(c).wait()
            wb = stage_ref[c % 2].astype(BF16)
            w_ref[c * kc:(c + 1) * kc, :] = wb
            a_ref, off = a_chunks[c]
            d = jnp.dot(a_ref[:, off:off + kc], wb, preferred_element_type=F32)
            y = d if y is None else y + d
        finish(slice(None), y)

    @pl.when(pl.program_id(0) > 0)
    def _():
        rc = x_ref.shape[0] // chunks
        for c in range(chunks):
            rows = slice(c * rc, (c + 1) * rc)
            y, row = None, 0
            for a_ref in a_refs:
                ka = a_ref.shape[1]
                d = jnp.dot(a_ref[rows, :], w_ref[row:row + ka, :],
                            preferred_element_type=F32)
                y = d if y is None else y + d
                row += ka
            finish(rows, y)


def mm_res_ln(a_list, w_stack, layer, x, g, b, *, tm, chunks=2, kc=512):
    M = a_list[0].shape[0]
    K = sum(a.shape[1] for a in a_list)
    N = w_stack.shape[2]
    assert w_stack.shape[1] == K and K % kc == 0
    blk = tm * K * 2 + tm * N * 4 + 2 * N * 4 + tm * N * 4 + tm * N * 2
    return pl.pallas_call(
        functools.partial(_mm_res_ln_kernel, n_a=len(a_list), chunks=chunks, layer=layer,
                          kc=kc),
        grid=(M // tm,),
        in_specs=[pl.BlockSpec((tm, a.shape[1]), lambda i: (i, 0)) for a in a_list] + [
            pl.BlockSpec(memory_space=pl.ANY),
            pl.BlockSpec((tm, N), lambda i: (i, 0)),
            pl.BlockSpec((1, N), lambda i: (0, 0)),
            pl.BlockSpec((1, N), lambda i: (0, 0))],
        out_specs=[pl.BlockSpec((tm, N), lambda i: (i, 0)),
                   pl.BlockSpec((tm, N), lambda i: (i, 0))],
        out_shape=[jax.ShapeDtypeStruct((M, N), F32),
                   jax.ShapeDtypeStruct((M, N), BF16)],
        scratch_shapes=[pltpu.VMEM((K, N), BF16), pltpu.VMEM((2, kc, N), F32),
                        pltpu.SemaphoreType.DMA((2,))],
        compiler_params=_params(
            _vmem_limit(blk, K * N * 2 + 2 * kc * N * 4, temp_bytes=3 * tm * N * 4), 1),
        name="mm_res_ln",
    )(*a_list, w_stack, x, g, b)


def _ffn_gu_kernel(x_ref, wg_ref, wu_ref, o_ref, wgb_ref, wub_ref):
    @pl.when(pl.program_id(1) == 0)
    def _():
        wgb_ref[...] = wg_ref[...].astype(BF16)
        wub_ref[...] = wu_ref[...].astype(BF16)

    x = x_ref[...]
    g = jnp.dot(x, wgb_ref[...], preferred_element_type=F32)
    u = jnp.dot(x, wub_ref[...], preferred_element_type=F32)
    o_ref[...] = (g * jax.nn.sigmoid(g) * u).astype(o_ref.dtype)


def ffn_gate_up(xb, wg_stack, wu_stack, layer, *, tm, tf):
    M, K = xb.shape
    F = wg_stack.shape[2]
    blk = tm * K * 2 + 2 * K * tf * 4 + tm * tf * 2
    w_spec = pl.BlockSpec((None, K, tf), lambda j, i: (layer, 0, j))
    return pl.pallas_call(
        _ffn_gu_kernel,
        grid=(F // tf, M // tm),
        in_specs=[pl.BlockSpec((tm, K), lambda j, i: (i, 0)), w_spec, w_spec],
        out_specs=pl.BlockSpec((tm, tf), lambda j, i: (i, j)),
        out_shape=jax.ShapeDtypeStruct((M, F), BF16),
        scratch_shapes=[pltpu.VMEM((K, tf), BF16), pltpu.VMEM((K, tf), BF16)],
        compiler_params=_params(
            _vmem_limit(blk, 2 * K * tf * 2, temp_bytes=4 * tm * tf * 4), 2),
        name="ffn_gate_up",
    )(xb, wg_stack, wu_stack)


def _rope_lanes(r, cos_t, sin_a, sin_b):
    return (r * cos_t + pltpu.roll(r, LANES - MLA_ROPE // 2, 1) * sin_a
            + pltpu.roll(r, MLA_ROPE // 2, 1) * sin_b)


def _rms(x, g):
    return x * lax.rsqrt(jnp.mean(x * x, axis=-1, keepdims=True) + RMS_EPS) * g


def _fetch_weight_once(w_hbm, layer, stage_ref, sem):
    copy = pltpu.make_async_copy(w_hbm.at[layer], stage_ref, sem.at[0])
    copy.start()
    copy.wait()


def _mla_in_kernel(x_ref, w_hbm, gq_ref, gkv_ref, cos_ref, sa_ref, sb_ref,
                   cq_ref, ckv_ref, kr_ref, w_ref, stage_ref, sem, *, layer):
    @pl.when(pl.program_id(0) == 0)
    def _():
        _fetch_weight_once(w_hbm, layer, stage_ref, sem)
        n_in = stage_ref.shape[1]
        w_ref[:, 0:n_in] = stage_ref[...].astype(BF16)
        w_ref[:, n_in:] = jnp.zeros((w_ref.shape[0], w_ref.shape[1] - n_in), BF16)

    half = x_ref.shape[0] // 2
    for rows in (slice(0, half), slice(half, 2 * half)):
        h = jnp.dot(x_ref[rows, :], w_ref[...], preferred_element_type=F32)
        cq_ref[rows, :] = _rms(h[:, :MLA_Q_RANK], gq_ref[...]).astype(BF16)
        ckv_ref[rows, :] = _rms(h[:, MLA_Q_RANK:MLA_Q_RANK + MLA_KV_RANK],
                                gkv_ref[...]).astype(BF16)
        r = h[:, MLA_Q_RANK + MLA_KV_RANK:]
        kr_ref[rows, :] = _rope_lanes(r, cos_ref[rows, :], sa_ref[rows, :],
                                      sb_ref[rows, :]).astype(BF16)


def mla_in(xb, w_stack, layer, gq, gkv, cos_t, sin_a, sin_b, *, tm=512):
    S, K = xb.shape
    n_in = w_stack.shape[2]
    N = MLA_Q_RANK + MLA_KV_RANK + LANES
    blk = (tm * K * 2 + 3 * tm * LANES * 4
           + tm * (MLA_Q_RANK + MLA_KV_RANK + LANES) * 2)
    row = lambda i: (i, 0)
    fixed = lambda i: (0, 0)
    return pl.pallas_call(
        functools.partial(_mla_in_kernel, layer=layer),
        grid=(S // tm,),
        in_specs=[pl.BlockSpec((tm, K), row), pl.BlockSpec(memory_space=pl.ANY),
                  pl.BlockSpec((1, MLA_Q_RANK), fixed), pl.BlockSpec((1, MLA_KV_RANK), fixed),
                  pl.BlockSpec((tm, LANES), row), pl.BlockSpec((tm, LANES), row),
                  pl.BlockSpec((tm, LANES), row)],
        out_specs=[pl.BlockSpec((tm, MLA_Q_RANK), row), pl.BlockSpec((tm, MLA_KV_RANK), row),
                   pl.BlockSpec((tm, LANES), row)],
        out_shape=[jax.ShapeDtypeStruct((S, MLA_Q_RANK), BF16),
                   jax.ShapeDtypeStruct((S, MLA_KV_RANK), BF16),
                   jax.ShapeDtypeStruct((S, LANES), BF16)],
        scratch_shapes=[pltpu.VMEM((K, N), BF16), pltpu.VMEM((K, n_in), F32),
                        pltpu.SemaphoreType.DMA((1,))],
        compiler_params=_params(
            _vmem_limit(blk, K * N * 2 + K * N * 4, temp_bytes=3 * tm * N * 4), 1),
        name="mla_in",
    )(xb, w_stack, gq, gkv, cos_t, sin_a, sin_b)


def _mla_q_kernel(cq_ref, w_hbm, cos_ref, sa_ref, sb_ref, q_ref, w_ref, stage_ref, sem,
                  *, qscale, layer):
    @pl.when(pl.program_id(0) == 0)
    def _():
        _fetch_weight_once(w_hbm, layer, stage_ref, sem)
        wf = stage_ref[...]
        dh = MLA_NOPE + MLA_ROPE
        w_ref[...] = jnp.zeros_like(w_ref)
        for hh in range(q_ref.shape[0]):
            w_ref[:, hh * 2 * LANES:hh * 2 * LANES + dh] = wf[:, hh * dh:(hh + 1) * dh].astype(BF16)

    acc = jnp.dot(cq_ref[...], w_ref[...], preferred_element_type=F32) * qscale
    cos_t, sin_a, sin_b = cos_ref[...], sa_ref[...], sb_ref[...]
    for hh in range(q_ref.shape[0]):
        base = hh * 2 * LANES
        q_ref[hh, :, 0:LANES] = acc[:, base:base + LANES].astype(BF16)
        r = acc[:, base + LANES:base + 2 * LANES]
        q_ref[hh, :, LANES:2 * LANES] = _rope_lanes(r, cos_t, sin_a, sin_b).astype(BF16)


def mla_q(cq, w_stack, layer, cos_t, sin_a, sin_b, *, qscale, tm=512):
    S, K = cq.shape
    n_in = w_stack.shape[2]
    nh = n_in // (MLA_NOPE + MLA_ROPE)
    N = nh * 2 * LANES
    blk = tm * K * 2 + 3 * tm * LANES * 4 + tm * N * 2
    row = lambda i: (i, 0)
    return pl.pallas_call(
        functools.partial(_mla_q_kernel, qscale=qscale, layer=layer),
        grid=(S // tm,),
        in_specs=[pl.BlockSpec((tm, K), row), pl.BlockSpec(memory_space=pl.ANY),
                  pl.BlockSpec((tm, LANES), row), pl.BlockSpec((tm, LANES), row),
                  pl.BlockSpec((tm, LANES), row)],
        out_specs=pl.BlockSpec((nh, tm, 2 * LANES), lambda i: (0, i, 0)),
        out_shape=jax.ShapeDtypeStruct((nh, S, 2 * LANES), BF16),
        scratch_shapes=[pltpu.VMEM((K, N), BF16), pltpu.VMEM((K, n_in), F32),
                        pltpu.SemaphoreType.DMA((1,))],
        compiler_params=_params(
            _vmem_limit(blk, K * N * 2 + K * n_in * 4, temp_bytes=2 * tm * N * 4 + K * n_in * 4),
            1),
        name="mla_q",
    )(cq, w_stack, cos_t, sin_a, sin_b)


def _mla_kv_kernel(ckv_ref, w_ref, kr_ref, k_ref, v_ref):
    acc = jnp.dot(ckv_ref[...], w_ref[...], preferred_element_type=F32)
    kr = kr_ref[...]
    for hh in range(k_ref.shape[0]):
        base = hh * (MLA_NOPE + MLA_V)
        k_ref[hh, :, 0:LANES] = acc[:, base:base + MLA_NOPE].astype(BF16)
        k_ref[hh, :, LANES:2 * LANES] = kr
        v_ref[hh] = acc[:, base + MLA_NOPE:base + MLA_NOPE + MLA_V].astype(BF16)


def mla_kv(ckv, w, kr, *, tm=512):
    S, K = ckv.shape
    N = w.shape[1]
    nh = N // (MLA_NOPE + MLA_V)
    blk = tm * K * 2 + K * N * 2 + tm * LANES * 2 + nh * tm * (2 * LANES + MLA_V) * 2
    row = lambda i: (i, 0)
    return pl.pallas_call(
        _mla_kv_kernel,
        grid=(S // tm,),
        in_specs=[pl.BlockSpec((tm, K), row), pl.BlockSpec((K, N), lambda i: (0, 0)),
                  pl.BlockSpec((tm, LANES), row)],
        out_specs=[pl.BlockSpec((nh, tm, 2 * LANES), lambda i: (0, i, 0)),
                   pl.BlockSpec((nh, tm, MLA_V), lambda i: (0, i, 0))],
        out_shape=[jax.ShapeDtypeStruct((nh, S, 2 * LANES), BF16),
                   jax.ShapeDtypeStruct((nh, S, MLA_V), BF16)],
        compiler_params=_params(_vmem_limit(blk, temp_bytes=2 * tm * N * 4), 1),
        name="mla_kv",
    )(ckv, w, kr)


def _rope_lane_tables(S):
    half = MLA_ROPE // 2
    inv = ROPE_THETA ** (-np.arange(0, MLA_ROPE, 2, dtype=np.float64) / MLA_ROPE)
    ang = np.arange(S, dtype=np.float64)[:, None] * inv[None, :]
    cos, sin = np.cos(ang), np.sin(ang)
    z = np.zeros((S, half))
    z2 = np.zeros((S, LANES - MLA_ROPE))
    cos_t = np.concatenate([cos, cos, z2], axis=1)
    sin_a = np.concatenate([-sin, z, z2], axis=1)
    sin_b = np.concatenate([z, sin, z2], axis=1)
    return tuple(jnp.asarray(a.astype(np.float32)) for a in (cos_t, sin_a, sin_b))


def _pad_cols(w, n):
    return jnp.pad(w, ((0, 0), (0, n - w.shape[1])))


def kernel(x, ab_w_in, ab_forget_bias, ab_w_out, rel_bias, mla_w_in, mla_q_norm,
           mla_kv_norm, mla_w_uq, mla_w_ukv, mla_w_out, ffn_w_gate, ffn_w_up,
           ffn_w_down, ln_g, ln_b):
    S = x.shape[1]
    xf = x.reshape(S, D_MODEL)
    xb = xf.astype(BF16)
    cos_t, sin_a, sin_b = _rope_lane_tables(S)
    nf = 3 * DA + N_HEADS_A
    qs = HEAD_DIM ** -0.5 * LOG2E
    colscale = np.ones((1, 3 * DA + 3 * DB), np.float32)
    colscale[:, :DA] = qs
    colscale[:, 3 * DA:3 * DA + DB] = qs
    colscale = jnp.asarray(colscale)
    causal_b2 = _causal_bias(ATTN_TILE)
    moba_b2 = moba_bias(rel_bias, n_heads=N_HEADS_B, t=ATTN_TILE)
    moba_kaug = _moba_key_onehot(S)
    hb = 3 * N_HEADS_A

    for layer in range(DEPTH):
        j = layer // 2
        g0, b0 = ln_g[layer, 0][None, :], ln_b[layer, 0][None, :]
        g1, b1 = ln_g[layer, 1][None, :], ln_b[layer, 1][None, :]
        if layer % 2 == 0:
            w_in = ab_w_in[j]
            w_qkv = jnp.concatenate([w_in[:, :3 * DA], w_in[:, nf:]], axis=1).astype(BF16)
            w_f = _pad_cols(w_in[:, 3 * DA:nf], LANES).astype(BF16)
            b_f = _pad_cols(ab_forget_bias[j][None, :], LANES)
            heads = mm_heads(xb, w_qkv, colscale, tm=1024, tn=1024)
            qaug, kaug = fox_gate(xb, w_f, b_f)
            ya = flash_attention([(heads, 0), (qaug, 0)], [(heads, N_HEADS_A), (kaug, 0)],
                                 (heads, 2 * N_HEADS_A), causal_b2,
                                 n_heads=N_HEADS_A, name="flash_fox")
            qaug_b = moba_gate(heads, rel_bias, n_heads=N_HEADS_B, q_off=hb,
                               k_off=hb + N_HEADS_B)
            yb = flash_attention([(heads, hb), (qaug_b, 0)],
                                 [(heads, hb + N_HEADS_B), (moba_kaug, None)],
                                 (heads, hb + 2 * N_HEADS_B), moba_b2,
                                 n_heads=N_HEADS_B, name="flash_moba", prev_biased=True)
            y, w_out = [ya, yb], ab_w_out
        else:
            cq, ckv, kr = mla_in(xb, mla_w_in, j, mla_q_norm[j][None, :],
                                 mla_kv_norm[j][None, :], cos_t, sin_a, sin_b)
            q_full = mla_q(cq, mla_w_uq, j, cos_t, sin_a, sin_b,
                           qscale=(MLA_NOPE + MLA_ROPE) ** -0.5 * LOG2E)
            k_full, v = mla_kv(ckv, mla_w_ukv[j].astype(BF16), kr)
            y = [flash_attention([(q_full, 0)], [(k_full, 0)], (v, 0), causal_b2,
                                 n_heads=MLA_HEADS, name="flash_mla")]
            w_out = mla_w_out
        xf, xb = mm_res_ln(y, w_out, j, xf, g0, b0, tm=512)
        hmid = ffn_gate_up(xb, ffn_w_gate, ffn_w_up, layer, tm=1024, tf=512)
        xf, xb = mm_res_ln([hmid], ffn_w_down, layer, xf, g1, b1, tm=256)
    return xf.reshape(1, S, D_MODEL)
```
